```python
import jax, jax.numpy as jnp
from jax import lax
import numpy as np

D_MODEL = 1024
BATCH = 8
SEQ = 2048
DEPTH = 2
DEC_BATCH = 128
DEC_SEQ = 1
PAST_LEN = 16384
PAGE_SIZE = 128

N_META = 16
N_MIXERS = 2
N_POOL_LAYERS = (DEPTH + 1) // 2
N_RET_LAYERS = DEPTH // 2
POOL_WINDOWS = (2, 4, 8, 16)
POOL_GROUPS = len(POOL_WINDOWS)
POOL_GROUP_DIM = D_MODEL // POOL_GROUPS
POOL_BUF = max(POOL_WINDOWS) - 1
RET_HEADS = 4
RET_KDIM = D_MODEL // RET_HEADS
RET_VDIM = 2 * D_MODEL // RET_HEADS
RET_CHUNK = 128
ROPE_BASE = 10000.0
D_FF_RAW = -(-8 * D_MODEL // 3)
D_FF = -(-D_FF_RAW // 256) * 256
EPS = 1e-6

kernel_name = 'pool_retention_hybrid_step'


def rmsnorm(x, g):
    xf = x.astype(jnp.float32)
    y = xf * lax.rsqrt(jnp.mean(xf * xf, axis=-1, keepdims=True) + EPS)
    return (y * g.astype(jnp.float32)).astype(x.dtype)


def swiglu(x, w_gate, w_up, w_down):
    return (jax.nn.silu(x @ w_gate) * (x @ w_up)) @ w_down


def multiscale_pool_mix(xn, prev, w_pool, scale):
    b, t, _ = xn.shape
    p = prev.shape[1]
    ext = jnp.concatenate([prev.astype(xn.dtype), xn], axis=1).astype(jnp.float32)
    cs = jnp.concatenate([jnp.zeros((b, 1, D_MODEL), jnp.float32), jnp.cumsum(ext, axis=1)], axis=1)
    end = p + jnp.arange(t) + 1
    groups = []
    for g, w in enumerate(POOL_WINDOWS):
        start = jnp.maximum(end - w, 0)
        sl = slice(g * POOL_GROUP_DIM, (g + 1) * POOL_GROUP_DIM)
        s = cs[:, end, sl] - cs[:, start, sl]
        cnt = (end - start).astype(jnp.float32)
        groups.append(s / cnt[None, :, None])
    pooled = jnp.stack(groups, axis=2)
    diff = (pooled - xn.astype(jnp.float32).reshape(b, t, POOL_GROUPS, POOL_GROUP_DIM)).astype(xn.dtype)
    mixed = jnp.einsum('btgc,gcd->btgd', diff, w_pool).reshape(b, t, D_MODEL)
    return mixed * scale, ext[:, -POOL_BUF:].astype(xn.dtype)


def rotary(x, pos):
    theta = 1.0 / (ROPE_BASE ** jnp.linspace(0.0, 1.0, RET_KDIM // 2, dtype=jnp.float32))
    ang = pos.astype(jnp.float32)[:, None] * theta[None, :]
    cos = jnp.cos(ang)[None, :, None, :]
    sin = jnp.sin(ang)[None, :, None, :]
    xf = x.astype(jnp.float32).reshape(*x.shape[:-1], RET_KDIM // 2, 2)
    x1, x2 = xf[..., 0], xf[..., 1]
    out = jnp.stack([x1 * cos - x2 * sin, x1 * sin + x2 * cos], axis=-1).reshape(x.shape)
    return out.astype(x.dtype)


def ret_chunk(q, k, v, s0, log_g):
    c = q.shape[1]
    idx = jnp.arange(c, dtype=jnp.float32)
    rel = idx[:, None] - idx[None, :]
    dmask = jnp.exp(jnp.where(rel[None] >= 0, rel[None] * log_g[:, None, None], -jnp.inf)).astype(q.dtype)
    scores = jnp.einsum('bihd,bjhd->bhij', q, k) * dmask
    o = jnp.einsum('bhij,bjhe->bihe', scores, v)
    cross = jnp.exp((idx[:, None] + 1.0) * log_g[None, :]).astype(q.dtype)
    o = o + jnp.einsum('bihd,bhde->bihe', q, s0.astype(q.dtype)) * cross[None, :, :, None]
    kdec = jnp.exp((c - 1.0 - idx)[:, None] * log_g[None, :]).astype(k.dtype)
    s_new = (jnp.exp(c * log_g)[None, :, None, None].astype(s0.dtype) * s0
             + jnp.einsum('bjhd,bjhe->bhde', k * kdec[None, :, :, None], v).astype(s0.dtype))
    return o, s_new


def retention_mix(xn, pos, s0, log_g, w_in, gn_w, gn_b, w_out, chunked):
    b, t, _ = xn.shape
    proj = xn @ w_in
    q, k, v, g = jnp.split(proj, [D_MODEL, 2 * D_MODEL, 4 * D_MODEL], axis=-1)
    q = rotary(q.reshape(b, t, RET_HEADS, RET_KDIM), pos) * (RET_KDIM ** -0.5)
    k = rotary(k.reshape(b, t, RET_HEADS, RET_KDIM), pos)
    v = v.reshape(b, t, RET_HEADS, RET_VDIM)
    if chunked:
        o0, s = ret_chunk(q[:, :N_META], k[:, :N_META], v[:, :N_META], s0, log_g)
        t_real = t - N_META
        nc = t_real // RET_CHUNK

        def to_chunks(a):
            return a[:, N_META:].reshape(b, nc, RET_CHUNK, RET_HEADS, a.shape[-1]).swapaxes(0, 1)

        def step(state, qkv):
            oc, state = ret_chunk(qkv[0], qkv[1], qkv[2], state, log_g)
            return state, oc

        s, oc = lax.scan(step, s, (to_chunks(q), to_chunks(k), to_chunks(v)))
        o = jnp.concatenate([o0, oc.swapaxes(0, 1).reshape(b, t_real, RET_HEADS, RET_VDIM)], axis=1)
    else:
        o, s = ret_chunk(q, k, v, s0, log_g)
    of = o.astype(jnp.float32)
    mu = jnp.mean(of, axis=-1, keepdims=True)
    var = jnp.var(of, axis=-1, keepdims=True)
    on = ((of - mu) * lax.rsqrt(var + EPS)).reshape(b, t, RET_HEADS * RET_VDIM)
    on = (on * gn_w.astype(jnp.float32) + gn_b.astype(jnp.float32)).astype(xn.dtype)
    return (jax.nn.silu(g) * on) @ w_out, s


def setup_inputs(seed: int = 0) -> dict:
    key = jax.random.key(seed)
    ks = jax.random.split(key, 18)
    f32 = jnp.float32
    nrm = lambda k, shape, s: jax.random.normal(k, shape, f32) * s
    return {
        'x_prompt': nrm(ks[0], (BATCH, SEQ, D_MODEL), 1.0),
        'x_sample': nrm(ks[1], (DEC_BATCH, DEC_SEQ, D_MODEL), 1.0),
        'state_pool': nrm(ks[2], (N_POOL_LAYERS, DEC_BATCH, POOL_BUF, D_MODEL), 1.0),
        'state_ret': nrm(ks[3], (N_RET_LAYERS, DEC_BATCH, RET_HEADS, RET_KDIM, RET_VDIM), 0.1),
        'meta_tokens': nrm(ks[4], (N_META, D_MODEL), 1.0),
        'norm_mix': 1.0 + nrm(ks[5], (DEPTH, D_MODEL), 0.02),
        'norm_ffn': 1.0 + nrm(ks[6], (DEPTH, D_MODEL), 0.02),
        'norm_final': 1.0 + nrm(ks[7], (D_MODEL,), 0.02),
        'w_pool': nrm(ks[8], (N_POOL_LAYERS, POOL_GROUPS, POOL_GROUP_DIM, POOL_GROUP_DIM), POOL_GROUP_DIM ** -0.5),
        'pool_scale': 1.0 + nrm(ks[9], (N_POOL_LAYERS, D_MODEL), 0.02),
        'w_ret_in': nrm(ks[10], (N_RET_LAYERS, D_MODEL, 6 * D_MODEL), D_MODEL ** -0.5),
        'ret_gn_w': 1.0 + nrm(ks[11], (N_RET_LAYERS, 2 * D_MODEL), 0.02),
        'ret_gn_b': nrm(ks[12], (N_RET_LAYERS, 2 * D_MODEL), 0.02),
        'w_ret_out': nrm(ks[13], (N_RET_LAYERS, 2 * D_MODEL, D_MODEL), (2 * D_MODEL) ** -0.5),
        'w_ffn_gate': nrm(ks[14], (DEPTH, D_MODEL, D_FF), D_MODEL ** -0.5),
        'w_ffn_up': nrm(ks[15], (DEPTH, D_MODEL, D_FF), D_MODEL ** -0.5),
        'w_ffn_down': nrm(ks[16], (DEPTH, D_FF, D_MODEL), D_FF ** -0.5),
    }


def reference(x_prompt, x_sample, state_pool, state_ret, meta_tokens, norm_mix, norm_ffn, norm_final,
              w_pool, pool_scale, w_ret_in, ret_gn_w, ret_gn_b, w_ret_out, w_ffn_gate, w_ffn_up, w_ffn_down):
    log_gamma = jnp.log(1.0 - 2.0 ** (-5.0 - jnp.arange(RET_HEADS, dtype=jnp.float32)))

    def trunk(h, pos, pool_prev, ret_prev, chunked):
        new_pool, new_ret = [], []
        for i in range(DEPTH):
            j = i // N_MIXERS
            xn = rmsnorm(h, norm_mix[i])
            if i % N_MIXERS == 0:
                mix, buf = multiscale_pool_mix(xn, pool_prev[j], w_pool[j], pool_scale[j])
                new_pool.append(buf)
            else:
                mix, s = retention_mix(xn, pos, ret_prev[j], log_gamma, w_ret_in[j], ret_gn_w[j],
                                       ret_gn_b[j], w_ret_out[j], chunked)
                new_ret.append(s)
            h = h + mix
            h = h + swiglu(rmsnorm(h, norm_ffn[i]), w_ffn_gate[i], w_ffn_up[i], w_ffn_down[i])
        return rmsnorm(h, norm_final), jnp.stack(new_pool), jnp.stack(new_ret)

    b = x_prompt.shape[0]
    meta = jnp.broadcast_to(meta_tokens[None].astype(x_prompt.dtype), (b, N_META, D_MODEL))
    h_p = jnp.concatenate([meta, x_prompt], axis=1)
    pos_p = jnp.arange(N_META + x_prompt.shape[1], dtype=jnp.int32)
    pool_prev_p = jnp.zeros((N_POOL_LAYERS, b, 0, D_MODEL), x_prompt.dtype)
    ret_prev_p = jnp.zeros((N_RET_LAYERS, b, RET_HEADS, RET_KDIM, RET_VDIM), state_ret.dtype)
    out_p, new_pool_prompt, new_ret_prompt = trunk(h_p, pos_p, pool_prev_p, ret_prev_p, True)
    y_prompt = out_p[:, N_META:]

    pos_s = PAST_LEN + jnp.arange(x_sample.shape[1], dtype=jnp.int32)
    y_sample, new_pool_sample, new_ret_sample = trunk(x_sample, pos_s, state_pool, state_ret, False)
    return (y_prompt, y_sample, new_pool_prompt, new_pool_sample, new_ret_prompt, new_ret_sample)
```

```python
import functools

import numpy as np
import jax
import jax.numpy as jnp
from jax import lax
from jax.experimental import pallas as pl
from jax.experimental.pallas import tpu as pltpu

D_MODEL = 1024
N_META = 16
PAST_LEN = 16384
POOL_WINDOWS = (2, 4, 8, 16)
POOL_GROUPS = len(POOL_WINDOWS)
POOL_GROUP_DIM = D_MODEL // POOL_GROUPS
POOL_BUF = max(POOL_WINDOWS) - 1
RET_HEADS = 4
RET_KDIM = D_MODEL // RET_HEADS
RET_VDIM = 2 * D_MODEL // RET_HEADS
ROPE_BASE = 10000.0
D_FF = 2816
EPS = 1e-6

BF = jnp.bfloat16
F32 = jnp.float32

SUB = 256
FF_CHUNKS = tuple((c, min(c + 512, D_FF)) for c in range(0, D_FF, 512))
VMEM_LIMIT = 56 * 1024 * 1024


def _dot(a, b):
    return jnp.dot(a, b, preferred_element_type=F32)


def _rms(x, g):
    ms = jnp.mean(x * x, axis=-1, keepdims=True)
    return (x * lax.rsqrt(ms + EPS)) * g


def _silu(x):
    return x * jax.nn.sigmoid(x)


def _const_spec(shape):
    nd = len(shape)
    return pl.BlockSpec(shape, lambda *_: (0,) * nd, pipeline_mode=pl.Buffered(1))


def _params(n_grid):
    return pltpu.CompilerParams(
        dimension_semantics=("arbitrary",) * n_grid, vmem_limit_bytes=VMEM_LIMIT)


def _ffn_kernel(pre, final, *refs):
    refs = list(refs)
    h_ref = refs.pop(0)
    if pre:
        gated_ref = refs.pop(0)
        wo_ref = refs.pop(0)
    nw_ref, wg_ref, wu_ref, wd_ref = refs[:4]
    refs = refs[4:]
    if final:
        nf_ref = refs.pop(0)
    o_ref = refs.pop(0)

    h = h_ref[...]
    if pre:
        h = h + _dot(gated_ref[...], wo_ref[...])
    hn = _rms(h, nw_ref[...]).astype(BF)
    acc = h
    for c0, c1 in FF_CHUNKS:
        gt = _dot(hn, wg_ref[:, c0:c1])
        up = _dot(hn, wu_ref[:, c0:c1])
        act = (_silu(gt) * up).astype(BF)
        acc = acc + _dot(act, wd_ref[c0:c1, :])
    if final:
        acc = _rms(acc, nf_ref[...])
    o_ref[...] = acc


def _ffn(h, nw, wg, wu, wd, tm, gated=None, wo=None, nf=None):
    rows = h.shape[0]
    pre = gated is not None
    final = nf is not None
    row_spec = lambda width: pl.BlockSpec((tm, width), lambda i: (i, 0))
    args, specs = [h], [row_spec(D_MODEL)]
    if pre:
        args += [gated, wo]
        specs += [row_spec(2 * D_MODEL), _const_spec(wo.shape)]
    args += [nw, wg, wu, wd]
    specs += [_const_spec(nw.shape), _const_spec(wg.shape), _const_spec(wu.shape), _const_spec(wd.shape)]
    if final:
        args.append(nf)
        specs.append(_const_spec(nf.shape))
    return pl.pallas_call(
        functools.partial(_ffn_kernel, pre, final),
        grid=(rows // tm,),
        in_specs=specs,
        out_specs=row_spec(D_MODEL),
        out_shape=jax.ShapeDtypeStruct((rows, D_MODEL), F32),
        compiler_params=_params(1),
        name="ffn",
    )(*args)


def _band_matrices():
    t = np.arange(SUB)[:, None]
    s = np.arange(SUB)[None, :]
    sh = np.arange(N_META)[None, :] - N_META
    cur = np.stack([((t - s >= 0) & (t - s < w)) for w in POOL_WINDOWS]).astype(np.float32)
    halo = np.stack([(t - sh < w) for w in POOL_WINDOWS]).astype(np.float32)
    tm = np.arange(N_META)[:, None]
    sm = np.arange(N_META)[None, :]
    meta = np.stack([((tm - sm >= 0) & (tm - sm < w)) for w in POOL_WINDOWS]).astype(np.float32)
    inv_meta = np.concatenate(
        [np.repeat(1.0 / np.minimum(w, tm + 1.0), POOL_GROUP_DIM, axis=1) for w in POOL_WINDOWS], axis=1)
    inv_w = np.concatenate([np.full((1, POOL_GROUP_DIM), 1.0 / w) for w in POOL_WINDOWS], axis=1)
    return (jnp.asarray(cur, BF), jnp.asarray(halo, BF), jnp.asarray(meta, BF),
            jnp.asarray(inv_meta, F32), jnp.asarray(inv_w, F32))


def _group_cols(g):
    return slice(g * POOL_GROUP_DIM, (g + 1) * POOL_GROUP_DIM)


def _pool_mix(pooled_sum, inv_cnt, xn, wp_ref, scale):
    diff = (pooled_sum * inv_cnt - xn).astype(BF)
    mixed = jnp.concatenate(
        [_dot(diff[:, _group_cols(g)], wp_ref[g]) for g in range(POOL_GROUPS)], axis=1)
    return mixed * scale


def _pool_prompt_kernel(meta_ref, x_ref, nw_ref, wp_ref, sc_ref, bc_ref, bh_ref, bm_ref,
                        icm_ref, iw_ref, h_ref, tail_ref):
    nw = nw_ref[...]
    scale = sc_ref[...]

    xm = meta_ref[...]
    xn = _rms(xm, nw)
    xb = xn.astype(BF)
    pooled = jnp.concatenate(
        [_dot(bm_ref[g], xb[:, _group_cols(g)]) for g in range(POOL_GROUPS)], axis=1)
    h_ref[0, 0:N_META, :] = xm + _pool_mix(pooled, icm_ref[...], xn, wp_ref, scale)
    halo = xb

    inv_w = iw_ref[...]
    for j in range(x_ref.shape[1] // SUB):
        r0 = SUB * j
        x = x_ref[0, r0:r0 + SUB, :]
        xn = _rms(x, nw)
        xb = xn.astype(BF)
        pooled = jnp.concatenate(
            [_dot(bc_ref[g], xb[:, _group_cols(g)]) + _dot(bh_ref[g], halo[:, _group_cols(g)])
             for g in range(POOL_GROUPS)], axis=1)
        h_ref[0, N_META + r0:N_META + r0 + SUB, :] = x + _pool_mix(pooled, inv_w, xn, wp_ref, scale)
        halo = xb[SUB - N_META:, :]
    tail_ref[0] = xn[SUB - N_META:, :]


def _pool_prompt(meta, x, nw, wp, scale):
    b, t, _ = x.shape
    bands = _band_matrices()
    consts = (meta, nw, wp, scale) + bands
    return pl.pallas_call(
        _pool_prompt_kernel,
        grid=(b,),
        in_specs=[_const_spec(meta.shape),
                  pl.BlockSpec((1, t, D_MODEL), lambda i: (i, 0, 0))]
                 + [_const_spec(c.shape) for c in consts[1:]],
        out_specs=[pl.BlockSpec((1, N_META + t, D_MODEL), lambda i: (i, 0, 0)),
                   pl.BlockSpec((1, N_META, D_MODEL), lambda i: (i, 0, 0))],
        out_shape=[jax.ShapeDtypeStruct((b, N_META + t, D_MODEL), F32),
                   jax.ShapeDtypeStruct((b, N_META, D_MODEL), F32)],
        compiler_params=_params(1),
        name="pool_prompt",
    )(meta, x, *consts[1:])


def _pool_sample_kernel(x_ref, prev_ref, nw_ref, wp_ref, sc_ref, iw_ref, h_ref, np_ref):
    x = x_ref[...]
    xn = _rms(x, nw_ref[...])
    sums = []
    for g, w in enumerate(POOL_WINDOWS):
        s = xn[:, _group_cols(g)]
        for j in range(1, w):
            s = s + prev_ref[:, POOL_BUF - j, _group_cols(g)]
        sums.append(s)
    pooled = jnp.concatenate(sums, axis=1)
    h_ref[...] = x + _pool_mix(pooled, iw_ref[...], xn, wp_ref, sc_ref[...])
    for r in range(POOL_BUF - 1):
        np_ref[:, r, :] = prev_ref[:, r + 1, :]
    np_ref[:, POOL_BUF - 1, :] = xn


def _pool_sample(x, prev, nw, wp, scale, inv_w, tb=32):
    b = x.shape[0]
    return pl.pallas_call(
        _pool_sample_kernel,
        grid=(b // tb,),
        in_specs=[pl.BlockSpec((tb, D_MODEL), lambda i: (i, 0)),
                  pl.BlockSpec((tb, POOL_BUF, D_MODEL), lambda i: (i, 0, 0)),
                  _const_spec(nw.shape), _const_spec(wp.shape), _const_spec(scale.shape),
                  _const_spec(inv_w.shape)],
        out_specs=[pl.BlockSpec((tb, D_MODEL), lambda i: (i, 0)),
                   pl.BlockSpec((tb, POOL_BUF, D_MODEL), lambda i: (i, 0, 0))],
        out_shape=[jax.ShapeDtypeStruct((b, D_MODEL), F32),
                   jax.ShapeDtypeStruct((b, POOL_BUF, D_MODEL), F32)],
        compiler_params=_params(1),
        name="pool_sample",
    )(x, prev, nw, wp, scale, inv_w)


def _proj_kernel(h_ref, nw_ref, w_ref, cos_ref, sin_ref, q_ref, k_ref, v_ref, g_ref):
    hn = _rms(h_ref[...], nw_ref[...]).astype(BF)
    cos = jnp.concatenate([cos_ref[...]] * RET_HEADS, axis=1)
    sin = jnp.concatenate([sin_ref[...]] * RET_HEADS, axis=1)
    lane = lax.broadcasted_iota(jnp.int32, cos.shape, 1)
    even = (lane & 1) == 0

    def rotary(x):
        partner = jnp.where(even, pltpu.roll(x, D_MODEL - 1, 1), pltpu.roll(x, 1, 1))
        return x * cos + partner * sin

    q = rotary(_dot(hn, w_ref[:, 0:D_MODEL])) * (RET_KDIM ** -0.5)
    q_ref[...] = q.astype(BF)
    k_ref[...] = rotary(_dot(hn, w_ref[:, D_MODEL:2 * D_MODEL])).astype(BF)
    v_ref[...] = _dot(hn, w_ref[:, 2 * D_MODEL:4 * D_MODEL]).astype(BF)
    g_ref[...] = _dot(hn, w_ref[:, 4 * D_MODEL:6 * D_MODEL]).astype(BF)


def _proj(h, nw, w_in, cos, sin, tm):
    rows = h.shape[0]
    ntab = cos.shape[0] // tm
    row_spec = lambda width: pl.BlockSpec((tm, width), lambda i: (i, 0))
    tab_spec = pl.BlockSpec((tm, RET_KDIM), lambda i: (i % ntab, 0))
    return pl.pallas_call(
        _proj_kernel,
        grid=(rows // tm,),
        in_specs=[row_spec(D_MODEL), _const_spec(nw.shape), _const_spec(w_in.shape), tab_spec, tab_spec],
        out_specs=[row_spec(D_MODEL), row_spec(D_MODEL), row_spec(2 * D_MODEL), row_spec(2 * D_MODEL)],
        out_shape=[jax.ShapeDtypeStruct((rows, D_MODEL), BF),
                   jax.ShapeDtypeStruct((rows, D_MODEL), BF),
                   jax.ShapeDtypeStruct((rows, 2 * D_MODEL), BF),
                   jax.ShapeDtypeStruct((rows, 2 * D_MODEL), BF)],
        compiler_params=_params(1),
        name="ret_proj",
    )(h, nw, w_in, cos, sin)


def _group_norm_gate(o, g, gn_w, gn_b):
    mu = jnp.mean(o, axis=-1, keepdims=True)
    cen = o - mu
    var = jnp.mean(cen * cen, axis=-1, keepdims=True)
    on = cen * lax.rsqrt(var + EPS)
    return _silu(g) * (on * gn_w + gn_b)


def _ret_prompt_kernel(lg_ref, q_ref, k_ref, v_ref, g_ref, gw_ref, gb_ref, o_ref, s_ref, state):
    lg = lg_ref[pl.program_id(1)]
    gn_w = gw_ref[...]
    gn_b = gb_ref[...]
    state[...] = jnp.zeros_like(state)

    def decay_tables(c):
        idx = lax.broadcasted_iota(jnp.int32, (c, 1), 0).astype(F32)
        rel = (lax.broadcasted_iota(jnp.int32, (c, c), 0)
               - lax.broadcasted_iota(jnp.int32, (c, c), 1)).astype(F32)
        dmask = jnp.where(rel >= 0, jnp.exp(jnp.maximum(rel, 0.0) * lg), 0.0)
        cross = jnp.exp((idx + 1.0) * lg)
        kdec = jnp.exp((c - 1.0 - idx) * lg)
        total = jnp.exp(jnp.full((1, 1), float(c), F32) * lg)
        return dmask, cross, kdec, total

    def chunk(r0, c, tables):
        dmask, cross, kdec, total = tables
        rows = pl.ds(r0, c)
        q = q_ref[0, rows, :]
        k = k_ref[0, rows, :]
        v = v_ref[0, rows, :]
        s_prev = state[...]
        scores = lax.dot_general(q, k, (((1,), (1,)), ((), ())), preferred_element_type=F32) * dmask
        o = _dot(scores.astype(BF), v) + _dot(q, s_prev.astype(BF)) * cross
        kd = (k.astype(F32) * kdec).astype(BF)
        state[...] = total * s_prev + lax.dot_general(
            kd, v, (((0,), (0,)), ((), ())), preferred_element_type=F32)
        g = g_ref[0, rows, :].astype(F32)
        o_ref[0, rows, :] = _group_norm_gate(o, g, gn_w, gn_b).astype(BF)

    chunk(0, N_META, decay_tables(N_META))
    tables = decay_tables(SUB)

    def body(j, carry):
        chunk(pl.multiple_of(N_META + j * SUB, N_META), SUB, tables)
        return carry

    lax.fori_loop(0, (q_ref.shape[1] - N_META) // SUB, body, 0)
    s_ref[0, 0] = state[...]


def _ret_prompt(log_g, q, k, v, g, gn_w, gn_b):
    b, t, _ = q.shape
    qk_spec = pl.BlockSpec((1, t, RET_KDIM), lambda i, h, lg: (i, 0, h))
    vg_spec = pl.BlockSpec((1, t, RET_VDIM), lambda i, h, lg: (i, 0, h))
    gn_spec = pl.BlockSpec((1, RET_VDIM), lambda i, h, lg: (0, h))
    return pl.pallas_call(
        _ret_prompt_kernel,
        grid_spec=pltpu.PrefetchScalarGridSpec(
            num_scalar_prefetch=1,
            grid=(b, RET_HEADS),
            in_specs=[qk_spec, qk_spec, vg_spec, vg_spec, gn_spec, gn_spec],
            out_specs=[vg_spec,
                       pl.BlockSpec((1, 1, RET_KDIM, RET_VDIM), lambda i, h, lg: (i, h, 0, 0))],
            scratch_shapes=[pltpu.VMEM((RET_KDIM, RET_VDIM), F32)]),
        out_shape=[jax.ShapeDtypeStruct((b, t, RET_HEADS * RET_VDIM), BF),
                   jax.ShapeDtypeStruct((b, RET_HEADS, RET_KDIM, RET_VDIM), F32)],
        compiler_params=_params(2),
        name="ret_prompt",
    )(log_g, q, k, v, g, gn_w, gn_b)


def _ret_sample_kernel(lg_ref, q_ref, k_ref, v_ref, g_ref, gw_ref, gb_ref, s0_ref, o_ref, s1_ref):
    pad = 16
    first_row = lax.broadcasted_iota(jnp.int32, (pad, RET_KDIM), 0) == 0
    for h in range(RET_HEADS):
        gamma = jnp.exp(jnp.full((1, 1), 1.0, F32) * lg_ref[h])
        kcols = slice(h * RET_KDIM, (h + 1) * RET_KDIM)
        vcols = slice(h * RET_VDIM, (h + 1) * RET_VDIM)
        q = q_ref[0, :, kcols]
        k = k_ref[0, :, kcols]
        v = v_ref[0, :, vcols]
        s_prev = s0_ref[0, h]
        qs = _dot(jnp.broadcast_to(q, (pad, RET_KDIM)), s_prev.astype(BF))[0:1, :]
        score = jnp.sum(q.astype(F32) * k.astype(F32), axis=-1, keepdims=True)
        o = score * v.astype(F32) + qs * gamma
        k_pad = jnp.where(first_row, jnp.broadcast_to(k.astype(F32), (pad, RET_KDIM)), 0.0).astype(BF)
        kv = lax.dot_general(k_pad, jnp.broadcast_to(v, (pad, RET_VDIM)),
                             (((0,), (0,)), ((), ())), preferred_element_type=F32)
        s1_ref[0, h] = gamma * s_prev + kv
        g = g_ref[0, :, vcols].astype(F32)
        o_ref[0, :, vcols] = _group_norm_gate(o, g, gw_ref[:, vcols], gb_ref[:, vcols]).astype(BF)


def _ret_sample(log_g, q, k, v, g, gn_w, gn_b, s0):
    b = q.shape[0]
    row3 = lambda width: pl.BlockSpec((1, 1, width), lambda i, lg: (i, 0, 0))
    st_spec = pl.BlockSpec((1, RET_HEADS, RET_KDIM, RET_VDIM), lambda i, lg: (i, 0, 0, 0))
    gn_spec = pl.BlockSpec(gn_w.shape, lambda i, lg: (0, 0))
    return pl.pallas_call(
        _ret_sample_kernel,
        grid_spec=pltpu.PrefetchScalarGridSpec(
            num_scalar_prefetch=1,
            grid=(b,),
            in_specs=[row3(D_MODEL), row3(D_MODEL), row3(2 * D_MODEL), row3(2 * D_MODEL),
                      gn_spec, gn_spec, st_spec],
            out_specs=[row3(2 * D_MODEL), st_spec]),
        out_shape=[jax.ShapeDtypeStruct((b, 1, 2 * D_MODEL), BF),
                   jax.ShapeDtypeStruct(s0.shape, F32)],
        compiler_params=_params(1),
        name="ret_sample",
    )(log_g, q.reshape(b, 1, -1), k.reshape(b, 1, -1), v.reshape(b, 1, -1), g.reshape(b, 1, -1),
      gn_w, gn_b, s0)


def _rotary_tables(pos):
    theta = 1.0 / (ROPE_BASE ** jnp.linspace(0.0, 1.0, RET_KDIM // 2, dtype=F32))
    ang = pos.astype(F32)[:, None] * theta[None, :]
    cos = jnp.repeat(jnp.cos(ang), 2, axis=1)
    sin = jnp.stack([-jnp.sin(ang), jnp.sin(ang)], axis=-1).reshape(ang.shape[0], RET_KDIM)
    return cos, sin


def kernel(x_prompt, x_sample, state_pool, state_ret, meta_tokens, norm_mix, norm_ffn, norm_final,
           w_pool, pool_scale, w_ret_in, ret_gn_w, ret_gn_b, w_ret_out, w_ffn_gate, w_ffn_up, w_ffn_down):
    b, t, _ = x_prompt.shape
    bs = x_sample.shape[0]
    tp = N_META + t
    log_g = jnp.log(1.0 - 2.0 ** (-5.0 - jnp.arange(RET_HEADS, dtype=F32)))

    wp = w_pool[0].astype(BF)
    w_in = w_ret_in[0].astype(BF)
    w_out = w_ret_out[0].astype(BF)
    wg = w_ffn_gate.astype(BF)
    wu = w_ffn_up.astype(BF)
    wd = w_ffn_down.astype(BF)
    nm0, nm1 = norm_mix[0:1], norm_mix[1:2]
    nf0, nf1 = norm_ffn[0:1], norm_ffn[1:2]
    nfin = norm_final[None, :]
    scale = pool_scale[0:1]
    gn_w, gn_b = ret_gn_w[0:1], ret_gn_b[0:1]
    inv_w = _band_matrices()[4]

    tm = tp // 3
    h, tail = _pool_prompt(meta_tokens, x_prompt, nm0, wp, scale)
    h = _ffn(h.reshape(b * tp, D_MODEL), nf0, wg[0], wu[0], wd[0], tm)
    cos, sin = _rotary_tables(jnp.arange(tp, dtype=jnp.int32))
    q, k, v, g = _proj(h, nm1, w_in, cos, sin, tm)
    gated, ret_p = _ret_prompt(log_g, q.reshape(b, tp, -1), k.reshape(b, tp, -1),
                               v.reshape(b, tp, -1), g.reshape(b, tp, -1), gn_w, gn_b)
    y = _ffn(h, nf1, wg[1], wu[1], wd[1], tm, gated=gated.reshape(b * tp, -1), wo=w_out, nf=nfin)
    y_prompt = y.reshape(b, tp, D_MODEL)[:, N_META:]
    new_pool_prompt = tail[:, 1:][None]
    new_ret_prompt = ret_p[None]

    xs = x_sample.reshape(bs, D_MODEL)
    hs, pool_s = _pool_sample(xs, state_pool[0], nm0, wp, scale, inv_w)
    hs = _ffn(hs, nf0, wg[0], wu[0], wd[0], bs)
    pos_s = jnp.full((bs,), PAST_LEN, jnp.int32)
    cos_s, sin_s = _rotary_tables(pos_s)
    qs, ks, vs, gs = _proj(hs, nm1, w_in, cos_s, sin_s, bs)
    gated_s, ret_s = _ret_sample(log_g, qs, ks, vs, gs, gn_w, gn_b, state_ret[0])
    ys = _ffn(hs, nf1, wg[1], wu[1], wd[1], bs, gated=gated_s.reshape(bs, -1), wo=w_out, nf=nfin)
    y_sample = ys.reshape(bs, 1, D_MODEL)

    return (y_prompt, y_sample, new_pool_prompt, pool_s[None], new_ret_prompt, ret_s[None])
```

```python
import functools

import numpy as np
import jax
import jax.numpy as jnp
from jax import lax
from jax.experimental import pallas as pl
from jax.experimental.pallas import tpu as pltpu

D_MODEL = 1024
N_META = 16
PAST_LEN = 16384
POOL_WINDOWS = (2, 4, 8, 16)
POOL_GROUPS = len(POOL_WINDOWS)
POOL_GROUP_DIM = D_MODEL // POOL_GROUPS
POOL_BUF = max(POOL_WINDOWS) - 1
RET_HEADS = 4
RET_KDIM = D_MODEL // RET_HEADS
RET_VDIM = 2 * D_MODEL // RET_HEADS
ROPE_BASE = 10000.0
D_FF = 2816
EPS = 1e-6

BF = jnp.bfloat16
F32 = jnp.float32

SUB = 256
TM = 512
FF_CHUNKS = tuple((c, min(c + 512, D_FF)) for c in range(0, D_FF, 512))
VMEM_LIMIT = 56 * 1024 * 1024


def _dot(a, b):
    return jnp.dot(a, b, preferred_element_type=F32)


def _rms(x, g):
    ms = jnp.mean(x * x, axis=-1, keepdims=True)
    return (x * lax.rsqrt(ms + EPS)) * g


def _silu(x):
    return x * jax.nn.sigmoid(x)


def _const_spec(shape):
    nd = len(shape)
    return pl.BlockSpec(shape, lambda *_: (0,) * nd, pipeline_mode=pl.Buffered(1))


def _params(n_grid):
    return pltpu.CompilerParams(
        dimension_semantics=("arbitrary",) * n_grid, vmem_limit_bytes=VMEM_LIMIT)


def _ffn_kernel(pre, final, *refs):
    refs = list(refs)
    h_ref = refs.pop(0)
    if pre:
        gated_ref = refs.pop(0)
        wo_ref = refs.pop(0)
    nw_ref, wg_ref, wu_ref, wd_ref = refs[:4]
    refs = refs[4:]
    if final:
        nf_ref = refs.pop(0)
    o_ref = refs.pop(0)

    h = h_ref[...]
    if pre:
        h = h + _dot(gated_ref[...], wo_ref[...])
    hn = _rms(h, nw_ref[...]).astype(BF)
    acc = h
    for c0, c1 in FF_CHUNKS:
        gt = _dot(hn, wg_ref[:, c0:c1])
        up = _dot(hn, wu_ref[:, c0:c1])
        act = (_silu(gt) * up).astype(BF)
        acc = acc + _dot(act, wd_ref[c0:c1, :])
    if final:
        acc = _rms(acc, nf_ref[...])
    o_ref[...] = acc


def _ffn(h, nw, wg, wu, wd, tm, rows=None, gated=None, wo=None, nf=None):
    rows = h.shape[0] if rows is None else rows
    pre = gated is not None
    final = nf is not None
    row_spec = lambda width: pl.BlockSpec((tm, width), lambda i: (i, 0))
    args, specs = [h], [row_spec(D_MODEL)]
    if pre:
        args += [gated, wo]
        specs += [row_spec(2 * D_MODEL), _const_spec(wo.shape)]
    args += [nw, wg, wu, wd]
    specs += [_const_spec(nw.shape), _const_spec(wg.shape), _const_spec(wu.shape), _const_spec(wd.shape)]
    if final:
        args.append(nf)
        specs.append(_const_spec(nf.shape))
    return pl.pallas_call(
        functools.partial(_ffn_kernel, pre, final),
        grid=(rows // tm,),
        in_specs=specs,
        out_specs=row_spec(D_MODEL),
        out_shape=jax.ShapeDtypeStruct((rows, D_MODEL), F32),
        compiler_params=_params(1),
        name="ffn",
    )(*args)


def _band_matrices():
    t = np.arange(SUB)[:, None]
    s = np.arange(SUB)[None, :]
    sh = np.arange(N_META)[None, :] - N_META
    cur = np.stack([((t - s >= 0) & (t - s < w)) for w in POOL_WINDOWS]).astype(np.float32)
    halo = np.stack([(t - sh < w) for w in POOL_WINDOWS]).astype(np.float32)
    tm = np.arange(N_META)[:, None]
    sm = np.arange(N_META)[None, :]
    meta = np.stack([((tm - sm >= 0) & (tm - sm < w)) for w in POOL_WINDOWS]).astype(np.float32)
    inv_meta = np.concatenate(
        [np.repeat(1.0 / np.minimum(w, tm + 1.0), POOL_GROUP_DIM, axis=1) for w in POOL_WINDOWS], axis=1)
    inv_w = np.concatenate([np.full((1, POOL_GROUP_DIM), 1.0 / w) for w in POOL_WINDOWS], axis=1)
    return (jnp.asarray(cur, BF), jnp.asarray(halo, BF), jnp.asarray(meta, BF),
            jnp.asarray(inv_meta, F32), jnp.asarray(inv_w, F32))


def _group_cols(g):
    return slice(g * POOL_GROUP_DIM, (g + 1) * POOL_GROUP_DIM)


def _pool_mix(pooled_sum, inv_cnt, xn, wp_ref, scale):
    diff = (pooled_sum * inv_cnt - xn).astype(BF)
    mixed = jnp.concatenate(
        [_dot(diff[:, _group_cols(g)], wp_ref[g]) for g in range(POOL_GROUPS)], axis=1)
    return mixed * scale


def _pool_prompt_kernel(meta_ref, x_ref, nw_ref, wp_ref, sc_ref, bc_ref, bh_ref, bm_ref,
                        icm_ref, iw_ref, h_ref, hm_ref, tail_ref):
    nw = nw_ref[...]
    scale = sc_ref[...]

    xm = meta_ref[...]
    xn = _rms(xm, nw)
    xb = xn.astype(BF)
    pooled = jnp.concatenate(
        [_dot(bm_ref[g], xb[:, _group_cols(g)]) for g in range(POOL_GROUPS)], axis=1)
    hm_ref[...] = xm + _pool_mix(pooled, icm_ref[...], xn, wp_ref, scale)
    halo = xb

    inv_w = iw_ref[...]
    for j in range(x_ref.shape[1] // SUB):
        rows = slice(SUB * j, SUB * (j + 1))
        x = x_ref[0, rows, :]
        xn = _rms(x, nw)
        xb = xn.astype(BF)
        pooled = jnp.concatenate(
            [_dot(bc_ref[g], xb[:, _group_cols(g)]) + _dot(bh_ref[g], halo[:, _group_cols(g)])
             for g in range(POOL_GROUPS)], axis=1)
        h_ref[0, rows, :] = x + _pool_mix(pooled, inv_w, xn, wp_ref, scale)
        halo = xb[SUB - N_META:, :]
    tail_ref[0] = xn[SUB - N_META:, :]


def _pool_prompt(meta, x, nw, wp, scale):
    b, t, _ = x.shape
    bands = _band_matrices()
    consts = (meta, nw, wp, scale) + bands
    seq_spec = pl.BlockSpec((1, t, D_MODEL), lambda i: (i, 0, 0))
    return pl.pallas_call(
        _pool_prompt_kernel,
        grid=(b,),
        in_specs=[_const_spec(meta.shape), seq_spec] + [_const_spec(c.shape) for c in consts[1:]],
        out_specs=[seq_spec,
                   pl.BlockSpec((N_META, D_MODEL), lambda i: (0, 0)),
                   pl.BlockSpec((1, N_META, D_MODEL), lambda i: (i, 0, 0))],
        out_shape=[jax.ShapeDtypeStruct((b, t, D_MODEL), F32),
                   jax.ShapeDtypeStruct((N_META, D_MODEL), F32),
                   jax.ShapeDtypeStruct((b, N_META, D_MODEL), F32)],
        compiler_params=_params(1),
        name="pool_prompt",
    )(meta, x, *consts[1:])


def _pool_sample_kernel(x_ref, prev_ref, nw_ref, wp_ref, sc_ref, iw_ref, h_ref, np_ref):
    x = x_ref[...]
    xn = _rms(x, nw_ref[...])
    sums = []
    for g, w in enumerate(POOL_WINDOWS):
        s = xn[:, _group_cols(g)]
        for j in range(1, w):
            s = s + prev_ref[:, POOL_BUF - j, _group_cols(g)]
        sums.append(s)
    pooled = jnp.concatenate(sums, axis=1)
    h_ref[...] = x + _pool_mix(pooled, iw_ref[...], xn, wp_ref, sc_ref[...])
    for r in range(POOL_BUF - 1):
        np_ref[:, r, :] = prev_ref[:, r + 1, :]
    np_ref[:, POOL_BUF - 1, :] = xn


def _pool_sample(x, prev, nw, wp, scale, inv_w, tb=32):
    b = x.shape[0]
    st_spec = pl.BlockSpec((None, tb, POOL_BUF, D_MODEL), lambda i: (0, i, 0, 0))
    return pl.pallas_call(
        _pool_sample_kernel,
        grid=(b // tb,),
        in_specs=[pl.BlockSpec((tb, D_MODEL), lambda i: (i, 0)), st_spec,
                  _const_spec(nw.shape), _const_spec(wp.shape), _const_spec(scale.shape),
                  _const_spec(inv_w.shape)],
        out_specs=[pl.BlockSpec((tb, D_MODEL), lambda i: (i, 0)), st_spec],
        out_shape=[jax.ShapeDtypeStruct((b, D_MODEL), F32),
                   jax.ShapeDtypeStruct(prev.shape, F32)],
        compiler_params=_params(1),
        name="pool_sample",
    )(x, prev, nw, wp, scale, inv_w)


def _proj_kernel(h_ref, nw_ref, w_ref, cos_ref, sin_ref, q_ref, k_ref, v_ref, g_ref):
    hn = _rms(h_ref[...], nw_ref[...]).astype(BF)
    cos = jnp.concatenate([cos_ref[...]] * RET_HEADS, axis=1)
    sin = jnp.concatenate([sin_ref[...]] * RET_HEADS, axis=1)
    lane = lax.broadcasted_iota(jnp.int32, cos.shape, 1)
    even = (lane & 1) == 0

    def rotary(x):
        partner = jnp.where(even, pltpu.roll(x, D_MODEL - 1, 1), pltpu.roll(x, 1, 1))
        return x * cos + partner * sin

    q = rotary(_dot(hn, w_ref[:, 0:D_MODEL])) * (RET_KDIM ** -0.5)
    q_ref[...] = q.astype(BF)
    k_ref[...] = rotary(_dot(hn, w_ref[:, D_MODEL:2 * D_MODEL])).astype(BF)
    v_ref[...] = _dot(hn, w_ref[:, 2 * D_MODEL:4 * D_MODEL]).astype(BF)
    g_ref[...] = _dot(hn, w_ref[:, 4 * D_MODEL:6 * D_MODEL]).astype(BF)


def _proj(h, nw, w_in, cos, sin, tm):
    rows = h.shape[0]
    ntab = cos.shape[0] // tm
    row_spec = lambda width: pl.BlockSpec((tm, width), lambda i: (i, 0))
    tab_spec = pl.BlockSpec((tm, RET_KDIM), lambda i: (i % ntab, 0))
    return pl.pallas_call(
        _proj_kernel,
        grid=(rows // tm,),
        in_specs=[row_spec(D_MODEL), _const_spec(nw.shape), _const_spec(w_in.shape), tab_spec, tab_spec],
        out_specs=[row_spec(D_MODEL), row_spec(D_MODEL), row_spec(2 * D_MODEL), row_spec(2 * D_MODEL)],
        out_shape=[jax.ShapeDtypeStruct((rows, D_MODEL), BF),
                   jax.ShapeDtypeStruct((rows, D_MODEL), BF),
                   jax.ShapeDtypeStruct((rows, 2 * D_MODEL), BF),
                   jax.ShapeDtypeStruct((rows, 2 * D_MODEL), BF)],
        compiler_params=_params(1),
        name="ret_proj",
    )(h, nw, w_in, cos, sin)


def _group_norm_gate(o, g, gn_w, gn_b):
    mu = jnp.mean(o, axis=-1, keepdims=True)
    cen = o - mu
    var = jnp.mean(cen * cen, axis=-1, keepdims=True)
    on = cen * lax.rsqrt(var + EPS)
    return _silu(g) * (on * gn_w + gn_b)


def _kt_v(k, v):
    return lax.dot_general(k, v, (((0,), (0,)), ((), ())), preferred_element_type=F32)


def _ret_prompt_kernel(lg_ref, km_ref, vm_ref, q_ref, k_ref, v_ref, g_ref, gw_ref, gb_ref,
                       o_ref, s_ref, state):
    lg = lg_ref[pl.program_id(1)]
    gn_w = gw_ref[...]
    gn_b = gb_ref[...]

    def row_index(c):
        return lax.broadcasted_iota(jnp.int32, (c, 1), 0).astype(F32)

    kdec_m = jnp.exp((N_META - 1.0 - row_index(N_META)) * lg)
    state[...] = _kt_v((km_ref[...].astype(F32) * kdec_m).astype(BF), vm_ref[...])

    idx = row_index(SUB)
    rel = (lax.broadcasted_iota(jnp.int32, (SUB, SUB), 0)
           - lax.broadcasted_iota(jnp.int32, (SUB, SUB), 1)).astype(F32)
    dmask = jnp.where(rel >= 0, jnp.exp(jnp.maximum(rel, 0.0) * lg), 0.0)
    cross = jnp.exp((idx + 1.0) * lg)
    kdec = jnp.exp((SUB - 1.0 - idx) * lg)
    total = jnp.exp(jnp.full((1, 1), float(SUB), F32) * lg)

    def body(j, carry):
        rows = pl.ds(pl.multiple_of(j * SUB, SUB), SUB)
        q = q_ref[0, rows, :]
        k = k_ref[0, rows, :]
        v = v_ref[0, rows, :]
        s_prev = state[...]
        scores = lax.dot_general(q, k, (((1,), (1,)), ((), ())), preferred_element_type=F32) * dmask
        o = _dot(scores.astype(BF), v) + _dot(q, s_prev.astype(BF)) * cross
        kd = (k.astype(F32) * kdec).astype(BF)
        state[...] = total * s_prev + _kt_v(kd, v)
        g = g_ref[0, rows, :].astype(F32)
        o_ref[0, rows, :] = _group_norm_gate(o, g, gn_w, gn_b).astype(BF)
        return carry

    lax.fori_loop(0, q_ref.shape[1] // SUB, body, 0)
    s_ref[0, 0] = state[...]


def _ret_prompt(log_g, k_meta, v_meta, q, k, v, g, gn_w, gn_b):
    b, t, _ = q.shape
    qk_spec = pl.BlockSpec((1, t, RET_KDIM), lambda i, h, lg: (i, 0, h))
    vg_spec = pl.BlockSpec((1, t, RET_VDIM), lambda i, h, lg: (i, 0, h))
    gn_spec = pl.BlockSpec((1, RET_VDIM), lambda i, h, lg: (0, h))
    return pl.pallas_call(
        _ret_prompt_kernel,
        grid_spec=pltpu.PrefetchScalarGridSpec(
            num_scalar_prefetch=1,
            grid=(b, RET_HEADS),
            in_specs=[pl.BlockSpec((N_META, RET_KDIM), lambda i, h, lg: (0, h)),
                      pl.BlockSpec((N_META, RET_VDIM), lambda i, h, lg: (0, h)),
                      qk_spec, qk_spec, vg_spec, vg_spec, gn_spec, gn_spec],
            out_specs=[vg_spec,
                       pl.BlockSpec((None, 1, 1, RET_KDIM, RET_VDIM), lambda i, h, lg: (0, i, h, 0, 0))],
            scratch_shapes=[pltpu.VMEM((RET_KDIM, RET_VDIM), F32)]),
        out_shape=[jax.ShapeDtypeStruct((b, t, RET_HEADS * RET_VDIM), BF),
                   jax.ShapeDtypeStruct((1, b, RET_HEADS, RET_KDIM, RET_VDIM), F32)],
        compiler_params=_params(2),
        name="ret_prompt",
    )(log_g, k_meta, v_meta, q, k, v, g, gn_w, gn_b)


def _ret_sample_kernel(lg_ref, q_ref, k_ref, v_ref, g_ref, gw_ref, gb_ref, s0_ref, o_ref, s1_ref):
    pad = 16
    first_row = lax.broadcasted_iota(jnp.int32, (pad, RET_KDIM), 0) == 0
    for i in range(q_ref.shape[0]):
        for h in range(RET_HEADS):
            gamma = jnp.exp(jnp.full((1, 1), 1.0, F32) * lg_ref[h])
            kcols = slice(h * RET_KDIM, (h + 1) * RET_KDIM)
            vcols = slice(h * RET_VDIM, (h + 1) * RET_VDIM)
            q = q_ref[i, :, kcols]
            k = k_ref[i, :, kcols]
            v = v_ref[i, :, vcols]
            s_prev = s0_ref[i, h]
            qs = _dot(jnp.broadcast_to(q, (pad, RET_KDIM)), s_prev.astype(BF))[0:1, :]
            score = jnp.sum(q.astype(F32) * k.astype(F32), axis=-1, keepdims=True)
            o = score * v.astype(F32) + qs * gamma
            k_pad = jnp.where(first_row, jnp.broadcast_to(k.astype(F32), (pad, RET_KDIM)), 0.0).astype(BF)
            kv = _kt_v(k_pad, jnp.broadcast_to(v, (pad, RET_VDIM)))
            s1_ref[i, h] = gamma * s_prev + kv
            g = g_ref[i, :, vcols].astype(F32)
            o_ref[i, :, vcols] = _group_norm_gate(o, g, gw_ref[:, vcols], gb_ref[:, vcols]).astype(BF)


def _ret_sample(log_g, q, k, v, g, gn_w, gn_b, s0, tb=4):
    b = q.shape[0]
    row3 = lambda width: pl.BlockSpec((tb, 1, width), lambda i, lg: (i, 0, 0))
    st_spec = pl.BlockSpec((None, tb, RET_HEADS, RET_KDIM, RET_VDIM), lambda i, lg: (0, i, 0, 0, 0))
    gn_spec = pl.BlockSpec(gn_w.shape, lambda i, lg: (0, 0))
    return pl.pallas_call(
        _ret_sample_kernel,
        grid_spec=pltpu.PrefetchScalarGridSpec(
            num_scalar_prefetch=1,
            grid=(b // tb,),
            in_specs=[row3(D_MODEL), row3(D_MODEL), row3(2 * D_MODEL), row3(2 * D_MODEL),
                      gn_spec, gn_spec, st_spec],
            out_specs=[row3(2 * D_MODEL), st_spec]),
        out_shape=[jax.ShapeDtypeStruct((b, 1, 2 * D_MODEL), BF),
                   jax.ShapeDtypeStruct(s0.shape, F32)],
        compiler_params=_params(1),
        name="ret_sample",
    )(log_g, q.reshape(b, 1, -1), k.reshape(b, 1, -1), v.reshape(b, 1, -1), g.reshape(b, 1, -1),
      gn_w, gn_b, s0)


def _rotary_tables(pos):
    theta = 1.0 / (ROPE_BASE ** jnp.linspace(0.0, 1.0, RET_KDIM // 2, dtype=F32))
    ang = pos.astype(F32)[:, None] * theta[None, :]
    cos = jnp.repeat(jnp.cos(ang), 2, axis=1)
    sin = jnp.stack([-jnp.sin(ang), jnp.sin(ang)], axis=-1).reshape(ang.shape[0], RET_KDIM)
    return cos, sin


def kernel(x_prompt, x_sample, state_pool, state_ret, meta_tokens, norm_mix, norm_ffn, norm_final,
           w_pool, pool_scale, w_ret_in, ret_gn_w, ret_gn_b, w_ret_out, w_ffn_gate, w_ffn_up, w_ffn_down):
    b, t, _ = x_prompt.shape
    bs = x_sample.shape[0]
    log_g = jnp.log(1.0 - 2.0 ** (-5.0 - jnp.arange(RET_HEADS, dtype=F32)))

    wp = w_pool[0].astype(BF)
    w_in = w_ret_in[0].astype(BF)
    w_out = w_ret_out[0].astype(BF)
    wg = [w_ffn_gate[i].astype(BF) for i in range(2)]
    wu = [w_ffn_up[i].astype(BF) for i in range(2)]
    wd = [w_ffn_down[i].astype(BF) for i in range(2)]
    nm0, nm1 = norm_mix[0:1], norm_mix[1:2]
    nf0, nf1 = norm_ffn[0:1], norm_ffn[1:2]
    nfin = norm_final[None, :]
    scale = pool_scale[0:1]
    gn_w, gn_b = ret_gn_w[0:1], ret_gn_b[0:1]
    inv_w = _band_matrices()[4]

    n_small = bs + N_META
    h, h_meta, tail = _pool_prompt(meta_tokens, x_prompt, nm0, wp, scale)
    hs, new_pool_sample = _pool_sample(x_sample.reshape(bs, D_MODEL), state_pool, nm0, wp, scale, inv_w)
    h_small = _ffn(jnp.concatenate([hs, h_meta], axis=0), nf0, wg[0], wu[0], wd[0], n_small)
    h = _ffn(h.reshape(b * t, D_MODEL), nf0, wg[0], wu[0], wd[0], TM)

    pos_small = jnp.concatenate([jnp.full((bs,), PAST_LEN, jnp.int32), jnp.arange(N_META, dtype=jnp.int32)])
    cos_s, sin_s = _rotary_tables(pos_small)
    qs, ks, vs, gs = _proj(h_small, nm1, w_in, cos_s, sin_s, n_small)
    gated_s, new_ret_sample = _ret_sample(log_g, qs[:bs], ks[:bs], vs[:bs], gs[:bs], gn_w, gn_b, state_ret)
    y_sample = _ffn(h_small, nf1, wg[1], wu[1], wd[1], bs, rows=bs,
                    gated=gated_s.reshape(bs, -1), wo=w_out, nf=nfin)

    cos, sin = _rotary_tables(N_META + jnp.arange(t, dtype=jnp.int32))
    q, k, v, g = _proj(h, nm1, w_in, cos, sin, TM)
    gated, new_ret_prompt = _ret_prompt(
        log_g, ks[bs:], vs[bs:], q.reshape(b, t, -1), k.reshape(b, t, -1),
        v.reshape(b, t, -1), g.reshape(b, t, -1), gn_w, gn_b)
    y = _ffn(h, nf1, wg[1], wu[1], wd[1], TM, gated=gated.reshape(b * t, -1), wo=w_out, nf=nfin)

    return (y.reshape(b, t, D_MODEL), y_sample.reshape(bs, 1, D_MODEL), tail[:, 1:][None],
            new_pool_sample, new_ret_prompt, new_ret_sample)
```

```python
import functools

import numpy as np
import jax
import jax.numpy as jnp
from jax import lax
from jax.experimental import pallas as pl
from jax.experimental.pallas import tpu as pltpu

D_MODEL = 1024
N_META = 16
PAST_LEN = 16384
POOL_WINDOWS = (2, 4, 8, 16)
POOL_GROUPS = len(POOL_WINDOWS)
POOL_GROUP_DIM = D_MODEL // POOL_GROUPS
POOL_BUF = max(POOL_WINDOWS) - 1
RET_HEADS = 4
RET_KDIM = D_MODEL // RET_HEADS
RET_VDIM = 2 * D_MODEL // RET_HEADS
ROPE_BASE = 10000.0
D_FF = 2816
EPS = 1e-6

BF = jnp.bfloat16
F32 = jnp.float32

SUB = 256
TM = 512
FF_CHUNKS = tuple((c, min(c + 512, D_FF)) for c in range(0, D_FF, 512))
VMEM_LIMIT = 56 * 1024 * 1024


def _dot(a, b):
    return jnp.dot(a, b, preferred_element_type=F32)


def _rms(x, g):
    ms = jnp.mean(x * x, axis=-1, keepdims=True)
    return (x * lax.rsqrt(ms + EPS)) * g


def _silu(x):
    return x * jax.nn.sigmoid(x)


def _const_spec(shape):
    nd = len(shape)
    return pl.BlockSpec(shape, lambda *_: (0,) * nd, pipeline_mode=pl.Buffered(1))


def _params(n_grid):
    return pltpu.CompilerParams(
        dimension_semantics=("arbitrary",) * n_grid, vmem_limit_bytes=VMEM_LIMIT)


def _swiglu_residual(h, nw_ref, wg_ref, wu_ref, wd_ref):
    hn = _rms(h, nw_ref[...]).astype(BF)
    acc = h
    for c0, c1 in FF_CHUNKS:
        gt = _dot(hn, wg_ref[:, c0:c1])
        up = _dot(hn, wu_ref[:, c0:c1])
        act = (_silu(gt) * up).astype(BF)
        acc = acc + _dot(act, wd_ref[c0:c1, :])
    return acc


def _ffn_kernel(h_ref, nw_ref, wg_ref, wu_ref, wd_ref, o_ref):
    o_ref[...] = _swiglu_residual(h_ref[...], nw_ref, wg_ref, wu_ref, wd_ref)


def _ffn(h, nw, wg, wu, wd, tm):
    rows = h.shape[0]
    row_spec = pl.BlockSpec((tm, D_MODEL), lambda i: (i, 0))
    consts = (nw, wg, wu, wd)
    return pl.pallas_call(
        _ffn_kernel,
        grid=(rows // tm,),
        in_specs=[row_spec] + [_const_spec(c.shape) for c in consts],
        out_specs=row_spec,
        out_shape=jax.ShapeDtypeStruct((rows, D_MODEL), F32),
        compiler_params=_params(1),
        name="ffn",
    )(h, *consts)


def _out_ffn_kernel(h_ref, on_ref, g_ref, wo_ref, nw_ref, wg_ref, wu_ref, wd_ref, nf_ref, o_ref):
    gated = (_silu(g_ref[...].astype(F32)) * on_ref[...].astype(F32)).astype(BF)
    h = h_ref[...] + _dot(gated, wo_ref[...])
    o_ref[...] = _rms(_swiglu_residual(h, nw_ref, wg_ref, wu_ref, wd_ref), nf_ref[...])


def _out_ffn(h, on, g, wo, nw, wg, wu, wd, nf, tm, rows=None):
    rows = h.shape[0] if rows is None else rows
    row_spec = lambda width: pl.BlockSpec((tm, width), lambda i: (i, 0))
    consts = (wo, nw, wg, wu, wd, nf)
    return pl.pallas_call(
        _out_ffn_kernel,
        grid=(rows // tm,),
        in_specs=[row_spec(D_MODEL), row_spec(2 * D_MODEL), row_spec(2 * D_MODEL)]
                 + [_const_spec(c.shape) for c in consts],
        out_specs=row_spec(D_MODEL),
        out_shape=jax.ShapeDtypeStruct((rows, D_MODEL), F32),
        compiler_params=_params(1),
        name="out_ffn",
    )(h, on, g, *consts)


def _band_matrices():
    t = np.arange(SUB)[:, None]
    s = np.arange(SUB)[None, :]
    sh = np.arange(N_META)[None, :] - N_META
    cur = np.stack([((t - s >= 0) & (t - s < w)) for w in POOL_WINDOWS]).astype(np.float32)
    halo = np.stack([(t - sh < w) for w in POOL_WINDOWS]).astype(np.float32)
    tm = np.arange(N_META)[:, None]
    sm = np.arange(N_META)[None, :]
    meta = np.stack([((tm - sm >= 0) & (tm - sm < w)) for w in POOL_WINDOWS]).astype(np.float32)
    inv_meta = np.concatenate(
        [np.repeat(1.0 / np.minimum(w, tm + 1.0), POOL_GROUP_DIM, axis=1) for w in POOL_WINDOWS], axis=1)
    inv_w = np.concatenate([np.full((1, POOL_GROUP_DIM), 1.0 / w) for w in POOL_WINDOWS], axis=1)
    return (jnp.asarray(cur, BF), jnp.asarray(halo, BF), jnp.asarray(meta, BF),
            jnp.asarray(inv_meta, F32), jnp.asarray(inv_w, F32))


def _group_cols(g):
    return slice(g * POOL_GROUP_DIM, (g + 1) * POOL_GROUP_DIM)


def _pool_mix(pooled_sum, inv_cnt, xn, wp_ref, scale):
    diff = (pooled_sum * inv_cnt - xn).astype(BF)
    mixed = jnp.concatenate(
        [_dot(diff[:, _group_cols(g)], wp_ref[g]) for g in range(POOL_GROUPS)], axis=1)
    return mixed * scale


def _pool_prompt_kernel(meta_ref, x_ref, nw_ref, wp_ref, sc_ref, bc_ref, bh_ref, bm_ref,
                        icm_ref, iw_ref, h_ref, hm_ref, tail_ref):
    nw = nw_ref[...]
    scale = sc_ref[...]

    xm = meta_ref[...]
    xn = _rms(xm, nw)
    xb = xn.astype(BF)
    pooled = jnp.concatenate(
        [_dot(bm_ref[g], xb[:, _group_cols(g)]) for g in range(POOL_GROUPS)], axis=1)
    hm_ref[...] = xm + _pool_mix(pooled, icm_ref[...], xn, wp_ref, scale)
    halo = xb

    inv_w = iw_ref[...]
    for j in range(x_ref.shape[1] // SUB):
        rows = slice(SUB * j, SUB * (j + 1))
        x = x_ref[0, rows, :]
        xn = _rms(x, nw)
        xb = xn.astype(BF)
        pooled = jnp.concatenate(
            [_dot(bc_ref[g], xb[:, _group_cols(g)]) + _dot(bh_ref[g], halo[:, _group_cols(g)])
             for g in range(POOL_GROUPS)], axis=1)
        h_ref[0, rows, :] = x + _pool_mix(pooled, inv_w, xn, wp_ref, scale)
        halo = xb[SUB - N_META:, :]
    tail_ref[0] = xn[SUB - N_META:, :]


def _pool_prompt(meta, x, nw, wp, scale):
    b, t, _ = x.shape
    bands = _band_matrices()
    consts = (meta, nw, wp, scale) + bands
    seq_spec = pl.BlockSpec((1, t, D_MODEL), lambda i: (i, 0, 0))
    return pl.pallas_call(
        _pool_prompt_kernel,
        grid=(b,),
        in_specs=[_const_spec(meta.shape), seq_spec] + [_const_spec(c.shape) for c in consts[1:]],
        out_specs=[seq_spec,
                   pl.BlockSpec((N_META, D_MODEL), lambda i: (0, 0)),
                   pl.BlockSpec((1, N_META, D_MODEL), lambda i: (i, 0, 0))],
        out_shape=[jax.ShapeDtypeStruct((b, t, D_MODEL), F32),
                   jax.ShapeDtypeStruct((N_META, D_MODEL), F32),
                   jax.ShapeDtypeStruct((b, N_META, D_MODEL), F32)],
        compiler_params=_params(1),
        name="pool_prompt",
    )(meta, x, *consts[1:])


def _pool_sample_kernel(x_ref, prev_ref, nw_ref, wp_ref, sc_ref, iw_ref, h_ref, np_ref):
    x = x_ref[...]
    xn = _rms(x, nw_ref[...])
    sums = []
    for g, w in enumerate(POOL_WINDOWS):
        s = xn[:, _group_cols(g)]
        for j in range(1, w):
            s = s + prev_ref[POOL_BUF - j, :, _group_cols(g)]
        sums.append(s)
    pooled = jnp.concatenate(sums, axis=1)
    h_ref[...] = x + _pool_mix(pooled, iw_ref[...], xn, wp_ref, sc_ref[...])
    for r in range(POOL_BUF - 1):
        np_ref[r] = prev_ref[r + 1]
    np_ref[POOL_BUF - 1] = xn


def _pool_sample(x, prev, nw, wp, scale, inv_w, tb=32):
    b = x.shape[0]
    st_spec = pl.BlockSpec((None, POOL_BUF, tb, D_MODEL), lambda i: (0, 0, i, 0))
    return pl.pallas_call(
        _pool_sample_kernel,
        grid=(b // tb,),
        in_specs=[pl.BlockSpec((tb, D_MODEL), lambda i: (i, 0)), st_spec,
                  _const_spec(nw.shape), _const_spec(wp.shape), _const_spec(scale.shape),
                  _const_spec(inv_w.shape)],
        out_specs=[pl.BlockSpec((tb, D_MODEL), lambda i: (i, 0)), st_spec],
        out_shape=[jax.ShapeDtypeStruct((b, D_MODEL), F32),
                   jax.ShapeDtypeStruct(prev.shape, F32)],
        compiler_params=_params(1),
        name="pool_sample",
    )(x, prev, nw, wp, scale, inv_w)


def _proj_kernel(h_ref, nw_ref, w_ref, cos_ref, sin_ref, cross_ref, kdec_ref,
                 q_ref, qx_ref, k_ref, kd_ref, v_ref, g_ref):
    hn = _rms(h_ref[...], nw_ref[...]).astype(BF)
    cos = jnp.concatenate([cos_ref[...]] * RET_HEADS, axis=1)
    sin = jnp.concatenate([sin_ref[...]] * RET_HEADS, axis=1)
    lane = lax.broadcasted_iota(jnp.int32, cos.shape, 1)
    even = (lane & 1) == 0

    def rotary(x):
        partner = jnp.where(even, pltpu.roll(x, D_MODEL - 1, 1), pltpu.roll(x, 1, 1))
        return x * cos + partner * sin

    q = rotary(_dot(hn, w_ref[:, 0:D_MODEL])) * (RET_KDIM ** -0.5)
    q_ref[...] = q.astype(BF)
    qx_ref[...] = (q * cross_ref[...]).astype(BF)
    k = rotary(_dot(hn, w_ref[:, D_MODEL:2 * D_MODEL]))
    k_ref[...] = k.astype(BF)
    kd_ref[...] = (k * kdec_ref[...]).astype(BF)
    v_ref[...] = _dot(hn, w_ref[:, 2 * D_MODEL:4 * D_MODEL]).astype(BF)
    g_ref[...] = _dot(hn, w_ref[:, 4 * D_MODEL:6 * D_MODEL]).astype(BF)


def _proj(h, nw, w_in, cos, sin, cross, kdec, tm):
    rows = h.shape[0]
    row_spec = lambda width: pl.BlockSpec((tm, width), lambda i: (i, 0))

    def tab_spec(tab):
        ntab = tab.shape[0] // tm
        return pl.BlockSpec((tm, tab.shape[1]), lambda i: (i % ntab, 0))

    widths = (D_MODEL, D_MODEL, D_MODEL, D_MODEL, 2 * D_MODEL, 2 * D_MODEL)
    return pl.pallas_call(
        _proj_kernel,
        grid=(rows // tm,),
        in_specs=[row_spec(D_MODEL), _const_spec(nw.shape), _const_spec(w_in.shape),
                  tab_spec(cos), tab_spec(sin), tab_spec(cross), tab_spec(kdec)],
        out_specs=[row_spec(w) for w in widths],
        out_shape=[jax.ShapeDtypeStruct((rows, w), BF) for w in widths],
        compiler_params=_params(1),
        name="ret_proj",
    )(h, nw, w_in, cos, sin, cross, kdec)


def _group_norm(o, gn_w, gn_b):
    mu = jnp.mean(o, axis=-1, keepdims=True)
    cen = o - mu
    var = jnp.mean(cen * cen, axis=-1, keepdims=True)
    return (cen * lax.rsqrt(var + EPS)) * gn_w + gn_b


def _kt_v(k, v):
    return lax.dot_general(k, v, (((0,), (0,)), ((), ())), preferred_element_type=F32)


def _head_cols(h, width):
    return slice(h * width, (h + 1) * width)


def _ret_prompt_kernel(tot_ref, kdm_ref, vm_ref, dm_ref, q_ref, qx_ref, k_ref, kd_ref, v_ref,
                       gw_ref, gb_ref, o_ref, s_ref, state):
    t = pl.program_id(1)

    @pl.when(t == 0)
    def _():
        for h in range(RET_HEADS):
            state[h] = _kt_v(kdm_ref[:, _head_cols(h, RET_KDIM)], vm_ref[:, _head_cols(h, RET_VDIM)])

    for c in range(q_ref.shape[1] // SUB):
        rows = slice(c * SUB, (c + 1) * SUB)
        for h in range(RET_HEADS):
            kc = _head_cols(h, RET_KDIM)
            vc = _head_cols(h, RET_VDIM)
            q = q_ref[0, rows, kc]
            v = v_ref[0, rows, vc]
            s_prev = state[h]
            scores = lax.dot_general(q, k_ref[0, rows, kc], (((1,), (1,)), ((), ())),
                                     preferred_element_type=F32) * dm_ref[h]
            o = _dot(scores.astype(BF), v) + _dot(qx_ref[0, rows, kc], s_prev.astype(BF))
            state[h] = tot_ref[h] * s_prev + _kt_v(kd_ref[0, rows, kc], v)
            o_ref[0, rows, vc] = _group_norm(o, gw_ref[:, vc], gb_ref[:, vc]).astype(BF)

    @pl.when(t == pl.num_programs(1) - 1)
    def _():
        s_ref[0] = state[...]


def _ret_prompt(chunk_decay, kd_meta, v_meta, dmask, q, qx, k, kd, v, gn_w, gn_b, tt=2 * SUB):
    b, t, _ = q.shape
    qk_spec = pl.BlockSpec((1, tt, D_MODEL), lambda i, j, s: (i, j, 0))
    v_spec = pl.BlockSpec((1, tt, 2 * D_MODEL), lambda i, j, s: (i, j, 0))
    const = lambda a: pl.BlockSpec(a.shape, lambda i, j, s: (0,) * a.ndim, pipeline_mode=pl.Buffered(1))
    st_shape = (RET_HEADS, RET_KDIM, RET_VDIM)
    return pl.pallas_call(
        _ret_prompt_kernel,
        grid_spec=pltpu.PrefetchScalarGridSpec(
            num_scalar_prefetch=1,
            grid=(b, t // tt),
            in_specs=[const(kd_meta), const(v_meta), const(dmask),
                      qk_spec, qk_spec, qk_spec, qk_spec, v_spec, const(gn_w), const(gn_b)],
            out_specs=[v_spec,
                       pl.BlockSpec((None, 1) + st_shape, lambda i, j, s: (0, i, 0, 0, 0))],
            scratch_shapes=[pltpu.VMEM(st_shape, F32)]),
        out_shape=[jax.ShapeDtypeStruct((b, t, 2 * D_MODEL), BF),
                   jax.ShapeDtypeStruct((1, b) + st_shape, F32)],
        compiler_params=_params(2),
        name="ret_prompt",
    )(chunk_decay, kd_meta, v_meta, dmask, q, qx, k, kd, v, gn_w, gn_b)


def _ret_sample_kernel(lg_ref, q_ref, k_ref, v_ref, gw_ref, gb_ref, s0_ref, o_ref, s1_ref):
    pad = 16
    first_row = lax.broadcasted_iota(jnp.int32, (pad, RET_KDIM), 0) == 0
    for i in range(q_ref.shape[0]):
        for h in range(RET_HEADS):
            gamma = jnp.exp(jnp.full((1, 1), 1.0, F32) * lg_ref[h])
            kcols = slice(h * RET_KDIM, (h + 1) * RET_KDIM)
            vcols = slice(h * RET_VDIM, (h + 1) * RET_VDIM)
            q = q_ref[i, :, kcols]
            k = k_ref[i, :, kcols]
            v = v_ref[i, :, vcols]
            s_prev = s0_ref[i, h]
            qs = _dot(jnp.broadcast_to(q, (pad, RET_KDIM)), s_prev.astype(BF))[0:1, :]
            score = jnp.sum(q.astype(F32) * k.astype(F32), axis=-1, keepdims=True)
            o = score * v.astype(F32) + qs * gamma
            k_pad = jnp.where(first_row, jnp.broadcast_to(k.astype(F32), (pad, RET_KDIM)), 0.0).astype(BF)
            kv = _kt_v(k_pad, jnp.broadcast_to(v, (pad, RET_VDIM)))
            s1_ref[i, h] = gamma * s_prev + kv
            o_ref[i, :, vcols] = _group_norm(o, gw_ref[:, vcols], gb_ref[:, vcols]).astype(BF)


def _ret_sample(log_g, q, k, v, gn_w, gn_b, s0, tb=4):
    b = q.shape[0]
    row3 = lambda width: pl.BlockSpec((tb, 1, width), lambda i, lg: (i, 0, 0))
    st_spec = pl.BlockSpec((None, tb, RET_HEADS, RET_KDIM, RET_VDIM), lambda i, lg: (0, i, 0, 0, 0))
    gn_spec = pl.BlockSpec(gn_w.shape, lambda i, lg: (0, 0))
    return pl.pallas_call(
        _ret_sample_kernel,
        grid_spec=pltpu.PrefetchScalarGridSpec(
            num_scalar_prefetch=1,
            grid=(b // tb,),
            in_specs=[row3(D_MODEL), row3(D_MODEL), row3(2 * D_MODEL), gn_spec, gn_spec, st_spec],
            out_specs=[row3(2 * D_MODEL), st_spec]),
        out_shape=[jax.ShapeDtypeStruct((b, 1, 2 * D_MODEL), BF),
                   jax.ShapeDtypeStruct(s0.shape, F32)],
        compiler_params=_params(1),
        name="ret_sample",
    )(log_g, q.reshape(b, 1, -1), k.reshape(b, 1, -1), v.reshape(b, 1, -1), gn_w, gn_b, s0)


def _rotary_tables(pos):
    theta = 1.0 / (ROPE_BASE ** jnp.linspace(0.0, 1.0, RET_KDIM // 2, dtype=F32))
    ang = pos.astype(F32)[:, None] * theta[None, :]
    cos = jnp.repeat(jnp.cos(ang), 2, axis=1)
    sin = jnp.stack([-jnp.sin(ang), jnp.sin(ang)], axis=-1).reshape(ang.shape[0], RET_KDIM)
    return cos, sin


def _decay_to_end(log_g, c):
    idx = jnp.arange(c, dtype=F32)
    return jnp.repeat(jnp.exp((c - 1.0 - idx)[:, None] * log_g[None, :]), RET_KDIM, axis=1)


def kernel(x_prompt, x_sample, state_pool, state_ret, meta_tokens, norm_mix, norm_ffn, norm_final,
           w_pool, pool_scale, w_ret_in, ret_gn_w, ret_gn_b, w_ret_out, w_ffn_gate, w_ffn_up, w_ffn_down):
    b, t, _ = x_prompt.shape
    bs = x_sample.shape[0]
    log_g = jnp.log(1.0 - 2.0 ** (-5.0 - jnp.arange(RET_HEADS, dtype=F32)))

    wp = w_pool[0].astype(BF)
    w_in = w_ret_in[0].astype(BF)
    w_out = w_ret_out[0].astype(BF)
    wg = [w_ffn_gate[i].astype(BF) for i in range(2)]
    wu = [w_ffn_up[i].astype(BF) for i in range(2)]
    wd = [w_ffn_down[i].astype(BF) for i in range(2)]
    nm0, nm1 = norm_mix[0:1], norm_mix[1:2]
    nf0, nf1 = norm_ffn[0:1], norm_ffn[1:2]
    nfin = norm_final[None, :]
    scale = pool_scale[0:1]
    gn_w, gn_b = ret_gn_w[0:1], ret_gn_b[0:1]
    inv_w = _band_matrices()[4]

    n_small = bs + N_META
    h, h_meta, tail = _pool_prompt(meta_tokens, x_prompt, nm0, wp, scale)
    hs, new_pool_sample = _pool_sample(x_sample.reshape(bs, D_MODEL), state_pool.transpose(0, 2, 1, 3),
                                       nm0, wp, scale, inv_w)
    new_pool_sample = new_pool_sample.transpose(0, 2, 1, 3)
    h_small = _ffn(jnp.concatenate([hs, h_meta], axis=0), nf0, wg[0], wu[0], wd[0], n_small)
    h = _ffn(h.reshape(b * t, D_MODEL), nf0, wg[0], wu[0], wd[0], TM)

    pos_small = jnp.concatenate([jnp.full((bs,), PAST_LEN, jnp.int32), jnp.arange(N_META, dtype=jnp.int32)])
    cos_s, sin_s = _rotary_tables(pos_small)
    ones_s = jnp.ones((bs, D_MODEL), F32)
    cross_s = jnp.ones((n_small, D_MODEL), F32)
    kdec_s = jnp.concatenate([ones_s, _decay_to_end(log_g, N_META)], axis=0)
    qs, _, ks, kds, vs, gs = _proj(h_small, nm1, w_in, cos_s, sin_s, cross_s, kdec_s, n_small)
    on_s, new_ret_sample = _ret_sample(log_g, qs[:bs], ks[:bs], vs[:bs], gn_w, gn_b, state_ret)
    y_sample = _out_ffn(h_small, on_s.reshape(bs, -1), gs, w_out, nf1, wg[1], wu[1], wd[1], nfin, bs, rows=bs)

    cos, sin = _rotary_tables(N_META + jnp.arange(t, dtype=jnp.int32))
    idx = jnp.arange(SUB, dtype=F32)
    rel = idx[:, None] - idx[None, :]
    dmask = jnp.exp(jnp.where(rel[None] >= 0, rel[None] * log_g[:, None, None], -jnp.inf))
    cross = jnp.repeat(jnp.exp((idx[:, None] + 1.0) * log_g[None, :]), RET_KDIM, axis=1)
    reps = (TM // SUB, 1)
    q, qx, k, kd, v, g = _proj(h, nm1, w_in, cos, sin, jnp.tile(cross, reps),
                               jnp.tile(_decay_to_end(log_g, SUB), reps), TM)
    seq = lambda a: a.reshape(b, t, -1)
    on, new_ret_prompt = _ret_prompt(jnp.exp(SUB * log_g), kds[bs:], vs[bs:], dmask,
                                     seq(q), seq(qx), seq(k), seq(kd), seq(v), gn_w, gn_b)
    y = _out_ffn(h, on.reshape(b * t, -1), g, w_out, nf1, wg[1], wu[1], wd[1], nfin, TM)

    return (y.reshape(b, t, D_MODEL), y_sample.reshape(bs, 1, D_MODEL), tail[:, 1:][None],
            new_pool_sample, new_ret_prompt, new_ret_sample)
```

```python
import functools

import numpy as np
import jax
import jax.numpy as jnp
from jax import lax
from jax.experimental import pallas as pl
from jax.experimental.pallas import tpu as pltpu

D_MODEL = 1024
N_META = 16
PAST_LEN = 16384
POOL_WINDOWS = (2, 4, 8, 16)
POOL_GROUPS = len(POOL_WINDOWS)
POOL_GROUP_DIM = D_MODEL // POOL_GROUPS
POOL_BUF = max(POOL_WINDOWS) - 1
RET_HEADS = 4
RET_KDIM = D_MODEL // RET_HEADS
RET_VDIM = 2 * D_MODEL // RET_HEADS
ROPE_BASE = 10000.0
D_FF = 2816
EPS = 1e-6

BF = jnp.bfloat16
F32 = jnp.float32

SUB = 256
TM = 512
FF_CHUNKS = tuple((c, min(c + 512, D_FF)) for c in range(0, D_FF, 512))
VMEM_LIMIT = 56 * 1024 * 1024


def _dot(a, b):
    return jnp.dot(a, b, preferred_element_type=F32)


def _rms(x, g):
    ms = jnp.mean(x * x, axis=-1, keepdims=True)
    return (x * lax.rsqrt(ms + EPS)) * g


def _silu(x):
    return x * jax.nn.sigmoid(x)


def _const_spec(shape):
    nd = len(shape)
    return pl.BlockSpec(shape, lambda *_: (0,) * nd, pipeline_mode=pl.Buffered(1))


def _params(n_grid):
    return pltpu.CompilerParams(
        dimension_semantics=("arbitrary",) * n_grid, vmem_limit_bytes=VMEM_LIMIT)


def _swiglu_chunk(hn, cols, wg_ref, wu_ref, wd_ref):
    c0, c1 = cols
    gt = _dot(hn, wg_ref[:, c0:c1])
    up = _dot(hn, wu_ref[:, c0:c1])
    act = (_silu(gt) * up).astype(BF)
    return _dot(act, wd_ref[c0:c1, :])


def _swiglu_residual(h, nw_ref, wg_ref, wu_ref, wd_ref):
    hn = _rms(h, nw_ref[...]).astype(BF)
    acc = h
    for cols in FF_CHUNKS:
        acc = acc + _swiglu_chunk(hn, cols, wg_ref, wu_ref, wd_ref)
    return acc


def _ffn_kernel(h_ref, nw_ref, wg_ref, wu_ref, wd_ref, o_ref):
    o_ref[...] = _swiglu_residual(h_ref[...], nw_ref, wg_ref, wu_ref, wd_ref)


def _ffn(h, nw, wg, wu, wd, tm):
    rows = h.shape[0]
    row_spec = pl.BlockSpec((tm, D_MODEL), lambda i: (i, 0))
    consts = (nw, wg, wu, wd)
    return pl.pallas_call(
        _ffn_kernel,
        grid=(rows // tm,),
        in_specs=[row_spec] + [_const_spec(c.shape) for c in consts],
        out_specs=row_spec,
        out_shape=jax.ShapeDtypeStruct((rows, D_MODEL), F32),
        compiler_params=_params(1),
        name="ffn",
    )(h, *consts)


def _out_ffn_kernel(h_ref, on_ref, g_ref, wo_ref, nw_ref, wg_ref, wu_ref, wd_ref, nf_ref, o_ref):
    gated = (_silu(g_ref[...].astype(F32)) * on_ref[...].astype(F32)).astype(BF)
    h = h_ref[...] + _dot(gated, wo_ref[...])
    o_ref[...] = _rms(_swiglu_residual(h, nw_ref, wg_ref, wu_ref, wd_ref), nf_ref[...])


def _out_ffn(h, on, g, wo, nw, wg, wu, wd, nf, tm, rows=None):
    rows = h.shape[0] if rows is None else rows
    row_spec = lambda width: pl.BlockSpec((tm, width), lambda i: (i, 0))
    consts = (wo, nw, wg, wu, wd, nf)
    return pl.pallas_call(
        _out_ffn_kernel,
        grid=(rows // tm,),
        in_specs=[row_spec(D_MODEL), row_spec(2 * D_MODEL), row_spec(2 * D_MODEL)]
                 + [_const_spec(c.shape) for c in consts],
        out_specs=row_spec(D_MODEL),
        out_shape=jax.ShapeDtypeStruct((rows, D_MODEL), F32),
        compiler_params=_params(1),
        name="out_ffn",
    )(h, on, g, *consts)


def _band_matrices():
    t = np.arange(SUB)[:, None]
    s = np.arange(SUB)[None, :]
    sh = np.arange(N_META)[None, :] - N_META
    cur = np.stack([((t - s >= 0) & (t - s < w)) for w in POOL_WINDOWS]).astype(np.float32)
    halo = np.stack([(t - sh < w) for w in POOL_WINDOWS]).astype(np.float32)
    tm = np.arange(N_META)[:, None]
    sm = np.arange(N_META)[None, :]
    meta = np.stack([((tm - sm >= 0) & (tm - sm < w)) for w in POOL_WINDOWS]).astype(np.float32)
    inv_meta = np.concatenate(
        [np.repeat(1.0 / np.minimum(w, tm + 1.0), POOL_GROUP_DIM, axis=1) for w in POOL_WINDOWS], axis=1)
    inv_w = np.concatenate([np.full((1, POOL_GROUP_DIM), 1.0 / w) for w in POOL_WINDOWS], axis=1)
    return (jnp.asarray(cur, BF), jnp.asarray(halo, BF), jnp.asarray(meta, BF),
            jnp.asarray(inv_meta, F32), jnp.asarray(inv_w, F32))


def _group_cols(g):
    return slice(g * POOL_GROUP_DIM, (g + 1) * POOL_GROUP_DIM)


def _pool_mix(pooled_sum, inv_cnt, xn, wp_ref, scale):
    diff = (pooled_sum * inv_cnt - xn).astype(BF)
    mixed = jnp.concatenate(
        [_dot(diff[:, _group_cols(g)], wp_ref[g]) for g in range(POOL_GROUPS)], axis=1)
    return mixed * scale


def _pool_prompt_kernel(meta_ref, x_ref, nw_ref, wp_ref, sc_ref, bc_ref, bh_ref, bm_ref,
                        icm_ref, iw_ref, h_ref, hm_ref, tail_ref):
    nw = nw_ref[...]
    scale = sc_ref[...]

    xm = meta_ref[...]
    xn = _rms(xm, nw)
    xb = xn.astype(BF)
    pooled = jnp.concatenate(
        [_dot(bm_ref[g], xb[:, _group_cols(g)]) for g in range(POOL_GROUPS)], axis=1)
    hm_ref[...] = xm + _pool_mix(pooled, icm_ref[...], xn, wp_ref, scale)
    halo = xb

    inv_w = iw_ref[...]
    for j in range(x_ref.shape[1] // SUB):
        rows = slice(SUB * j, SUB * (j + 1))
        x = x_ref[0, rows, :]
        xn = _rms(x, nw)
        xb = xn.astype(BF)
        pooled = jnp.concatenate(
            [_dot(bc_ref[g], xb[:, _group_cols(g)]) + _dot(bh_ref[g], halo[:, _group_cols(g)])
             for g in range(POOL_GROUPS)], axis=1)
        h_ref[0, rows, :] = x + _pool_mix(pooled, inv_w, xn, wp_ref, scale)
        halo = xb[SUB - N_META:, :]
    tail_ref[0] = xn[SUB - N_META:, :]


def _pool_prompt(meta, x, nw, wp, scale):
    b, t, _ = x.shape
    bands = _band_matrices()
    consts = (meta, nw, wp, scale) + bands
    seq_spec = pl.BlockSpec((1, t, D_MODEL), lambda i: (i, 0, 0))
    return pl.pallas_call(
        _pool_prompt_kernel,
        grid=(b,),
        in_specs=[_const_spec(meta.shape), seq_spec] + [_const_spec(c.shape) for c in consts[1:]],
        out_specs=[seq_spec,
                   pl.BlockSpec((N_META, D_MODEL), lambda i: (0, 0)),
                   pl.BlockSpec((1, N_META, D_MODEL), lambda i: (i, 0, 0))],
        out_shape=[jax.ShapeDtypeStruct((b, t, D_MODEL), F32),
                   jax.ShapeDtypeStruct((N_META, D_MODEL), F32),
                   jax.ShapeDtypeStruct((b, N_META, D_MODEL), F32)],
        compiler_params=_params(1),
        name="pool_prompt",
    )(meta, x, *consts[1:])


def _pool_sample_kernel(x_ref, prev_ref, nw_ref, wp_ref, sc_ref, iw_ref, h_ref, np_ref):
    x = x_ref[...]
    xn = _rms(x, nw_ref[...])
    sums = []
    for g, w in enumerate(POOL_WINDOWS):
        s = xn[:, _group_cols(g)]
        for j in range(1, w):
            s = s + prev_ref[POOL_BUF - j, :, _group_cols(g)]
        sums.append(s)
    pooled = jnp.concatenate(sums, axis=1)
    h_ref[...] = x + _pool_mix(pooled, iw_ref[...], xn, wp_ref, sc_ref[...])
    for r in range(POOL_BUF - 1):
        np_ref[r] = prev_ref[r + 1]
    np_ref[POOL_BUF - 1] = xn


def _pool_sample(x, prev, nw, wp, scale, inv_w, tb=32):
    b = x.shape[0]
    st_spec = pl.BlockSpec((None, POOL_BUF, tb, D_MODEL), lambda i: (0, 0, i, 0))
    return pl.pallas_call(
        _pool_sample_kernel,
        grid=(b // tb,),
        in_specs=[pl.BlockSpec((tb, D_MODEL), lambda i: (i, 0)), st_spec,
                  _const_spec(nw.shape), _const_spec(wp.shape), _const_spec(scale.shape),
                  _const_spec(inv_w.shape)],
        out_specs=[pl.BlockSpec((tb, D_MODEL), lambda i: (i, 0)), st_spec],
        out_shape=[jax.ShapeDtypeStruct((b, D_MODEL), F32),
                   jax.ShapeDtypeStruct(prev.shape, F32)],
        compiler_params=_params(1),
        name="pool_sample",
    )(x, prev, nw, wp, scale, inv_w)


def _proj_kernel(h_ref, nw_ref, w_ref, cos_ref, sin_ref, cross_ref, kdec_ref,
                 q_ref, qx_ref, k_ref, kd_ref, v_ref, g_ref):
    hn = _rms(h_ref[...], nw_ref[...]).astype(BF)
    cos = jnp.concatenate([cos_ref[...]] * RET_HEADS, axis=1)
    sin = jnp.concatenate([sin_ref[...]] * RET_HEADS, axis=1)
    lane = lax.broadcasted_iota(jnp.int32, cos.shape, 1)
    even = (lane & 1) == 0

    def rotary(x):
        partner = jnp.where(even, pltpu.roll(x, D_MODEL - 1, 1), pltpu.roll(x, 1, 1))
        return x * cos + partner * sin

    q = rotary(_dot(hn, w_ref[:, 0:D_MODEL])) * (RET_KDIM ** -0.5)
    q_ref[...] = q.astype(BF)
    qx_ref[...] = (q * cross_ref[...]).astype(BF)
    k = rotary(_dot(hn, w_ref[:, D_MODEL:2 * D_MODEL]))
    k_ref[...] = k.astype(BF)
    kd_ref[...] = (k * kdec_ref[...]).astype(BF)
    v_ref[...] = _dot(hn, w_ref[:, 2 * D_MODEL:4 * D_MODEL]).astype(BF)
    g_ref[...] = _dot(hn, w_ref[:, 4 * D_MODEL:6 * D_MODEL]).astype(BF)


def _proj(h, nw, w_in, cos, sin, cross, kdec, tm):
    rows = h.shape[0]
    row_spec = lambda width: pl.BlockSpec((tm, width), lambda i: (i, 0))

    def tab_spec(tab):
        ntab = tab.shape[0] // tm
        return pl.BlockSpec((tm, tab.shape[1]), lambda i: (i % ntab, 0))

    widths = (D_MODEL, D_MODEL, D_MODEL, D_MODEL, 2 * D_MODEL, 2 * D_MODEL)
    return pl.pallas_call(
        _proj_kernel,
        grid=(rows // tm,),
        in_specs=[row_spec(D_MODEL), _const_spec(nw.shape), _const_spec(w_in.shape),
                  tab_spec(cos), tab_spec(sin), tab_spec(cross), tab_spec(kdec)],
        out_specs=[row_spec(w) for w in widths],
        out_shape=[jax.ShapeDtypeStruct((rows, w), BF) for w in widths],
        compiler_params=_params(1),
        name="ret_proj",
    )(h, nw, w_in, cos, sin, cross, kdec)


def _group_norm(o, gn_w, gn_b):
    mu = jnp.mean(o, axis=-1, keepdims=True)
    cen = o - mu
    var = jnp.mean(cen * cen, axis=-1, keepdims=True)
    return (cen * lax.rsqrt(var + EPS)) * gn_w + gn_b


def _kt_v(k, v):
    return lax.dot_general(k, v, (((0,), (0,)), ((), ())), preferred_element_type=F32)


def _head_cols(h, width):
    return slice(h * width, (h + 1) * width)


def _ret_prompt_kernel(tot_ref, kdm_ref, vm_ref, dm_ref, q_ref, qx_ref, k_ref, kd_ref, v_ref,
                       gw_ref, gb_ref, o_ref, s_ref, state):
    t = pl.program_id(1)

    @pl.when(t == 0)
    def _():
        for h in range(RET_HEADS):
            state[h] = _kt_v(kdm_ref[:, _head_cols(h, RET_KDIM)], vm_ref[:, _head_cols(h, RET_VDIM)])

    for c in range(q_ref.shape[1] // SUB):
        rows = slice(c * SUB, (c + 1) * SUB)
        for h in range(RET_HEADS):
            kc = _head_cols(h, RET_KDIM)
            vc = _head_cols(h, RET_VDIM)
            q = q_ref[0, rows, kc]
            v = v_ref[0, rows, vc]
            s_prev = state[h]
            scores = lax.dot_general(q, k_ref[0, rows, kc], (((1,), (1,)), ((), ())),
                                     preferred_element_type=F32) * dm_ref[h]
            o = _dot(scores.astype(BF), v) + _dot(qx_ref[0, rows, kc], s_prev.astype(BF))
            state[h] = tot_ref[h] * s_prev + _kt_v(kd_ref[0, rows, kc], v)
            o_ref[0, rows, vc] = _group_norm(o, gw_ref[:, vc], gb_ref[:, vc]).astype(BF)

    @pl.when(t == pl.num_programs(1) - 1)
    def _():
        s_ref[0] = state[...]


def _ret_prompt(chunk_decay, kd_meta, v_meta, dmask, q, qx, k, kd, v, gn_w, gn_b, tt=2 * SUB):
    b, t, _ = q.shape
    qk_spec = pl.BlockSpec((1, tt, D_MODEL), lambda i, j, s: (i, j, 0))
    v_spec = pl.BlockSpec((1, tt, 2 * D_MODEL), lambda i, j, s: (i, j, 0))
    const = lambda a: pl.BlockSpec(a.shape, lambda i, j, s: (0,) * a.ndim, pipeline_mode=pl.Buffered(1))
    st_shape = (RET_HEADS, RET_KDIM, RET_VDIM)
    return pl.pallas_call(
        _ret_prompt_kernel,
        grid_spec=pltpu.PrefetchScalarGridSpec(
            num_scalar_prefetch=1,
            grid=(b, t // tt),
            in_specs=[const(kd_meta), const(v_meta), const(dmask),
                      qk_spec, qk_spec, qk_spec, qk_spec, v_spec, const(gn_w), const(gn_b)],
            out_specs=[v_spec,
                       pl.BlockSpec((None, 1) + st_shape, lambda i, j, s: (0, i, 0, 0, 0))],
            scratch_shapes=[pltpu.VMEM(st_shape, F32)]),
        out_shape=[jax.ShapeDtypeStruct((b, t, 2 * D_MODEL), BF),
                   jax.ShapeDtypeStruct((1, b) + st_shape, F32)],
        compiler_params=_params(2),
        name="ret_prompt",
    )(chunk_decay, kd_meta, v_meta, dmask, q, qx, k, kd, v, gn_w, gn_b)


STREAM_TB = 4


def _sample_state_update(s, lg_ref, qt_ref, kt_ref, v_ref, gw_ref, gb_ref, s_in, s_out, on_ref):
    for h in range(RET_HEADS):
        gamma = jnp.exp(jnp.full((1, 1), 1.0, F32) * lg_ref[h])
        kc = _head_cols(h, RET_KDIM)
        vc = _head_cols(h, RET_VDIM)
        q = qt_ref[kc, s:s + 1]
        k = kt_ref[kc, s:s + 1]
        v = v_ref[s:s + 1, vc]
        s_prev = s_in[s, h]
        qs = jnp.sum(q * s_prev, axis=0, keepdims=True)
        score = jnp.sum(q * k, axis=0, keepdims=True)
        o = score * v + qs * gamma
        s_out[s, h] = gamma * s_prev + k * v
        on_ref[s:s + 1, vc] = _group_norm(o, gw_ref[:, vc], gb_ref[:, vc])


def _ffn_stream_kernel(lg_ref, h_ref, nw_ref, wg_ref, wu_ref, wd_ref, qt_ref, kt_ref, v_ref,
                       gw_ref, gb_ref, s_hbm, o_ref, on_ref, so_hbm, s_in, s_out, in_sem, out_sem):
    i = pl.program_id(0)
    n = pl.num_programs(0)

    def in_copy(step, s):
        return pltpu.make_async_copy(s_hbm.at[0, step * STREAM_TB + s], s_in.at[s], in_sem.at[s])

    def out_copy(step, s):
        return pltpu.make_async_copy(s_out.at[s], so_hbm.at[0, step * STREAM_TB + s], out_sem.at[s])

    @pl.when(i == 0)
    def _():
        for s in range(STREAM_TB):
            in_copy(0, s).start()

    h = h_ref[...]
    hn = _rms(h, nw_ref[...]).astype(BF)
    acc = h
    chunks_after = np.array_split(np.arange(len(FF_CHUNKS)), STREAM_TB)
    for s in range(STREAM_TB):
        in_copy(i, s).wait()

        @pl.when(i > 0)
        def _():
            out_copy(i - 1, s).wait()

        _sample_state_update(s, lg_ref, qt_ref, kt_ref, v_ref, gw_ref, gb_ref, s_in, s_out, on_ref)
        out_copy(i, s).start()

        @pl.when(i + 1 < n)
        def _():
            in_copy(i + 1, s).start()

        for c in chunks_after[s]:
            acc = acc + _swiglu_chunk(hn, FF_CHUNKS[c], wg_ref, wu_ref, wd_ref)
    o_ref[...] = acc

    @pl.when(i == n - 1)
    def _():
        for s in range(STREAM_TB):
            out_copy(i, s).wait()


def _ffn_stream(h, nw, wg, wu, wd, tm, log_g, q_s, k_s, v_s, gn_w, gn_b, s0):
    rows = h.shape[0]
    n = rows // tm
    bs = q_s.shape[0]
    assert bs == n * STREAM_TB
    cols = lambda a: a.astype(F32).reshape(n, STREAM_TB, -1).transpose(0, 2, 1)
    v3 = v_s.astype(F32).reshape(n, STREAM_TB, -1)
    row_spec = pl.BlockSpec((tm, D_MODEL), lambda i, lg: (i, 0))
    col_spec = pl.BlockSpec((None, D_MODEL, STREAM_TB), lambda i, lg: (i, 0, 0))
    v_spec = pl.BlockSpec((None, STREAM_TB, 2 * D_MODEL), lambda i, lg: (i, 0, 0))
    const = lambda a: pl.BlockSpec(a.shape, lambda i, lg: (0,) * a.ndim, pipeline_mode=pl.Buffered(1))
    hbm = pl.BlockSpec(memory_space=pl.ANY)
    blk = (STREAM_TB, RET_HEADS, RET_KDIM, RET_VDIM)
    h_out, on_s, s1 = pl.pallas_call(
        _ffn_stream_kernel,
        grid_spec=pltpu.PrefetchScalarGridSpec(
            num_scalar_prefetch=1,
            grid=(n,),
            in_specs=[row_spec, const(nw), const(wg), const(wu), const(wd),
                      col_spec, col_spec, v_spec, const(gn_w), const(gn_b), hbm],
            out_specs=[row_spec, v_spec, hbm],
            scratch_shapes=[pltpu.VMEM(blk, F32), pltpu.VMEM(blk, F32),
                            pltpu.SemaphoreType.DMA((STREAM_TB,)), pltpu.SemaphoreType.DMA((STREAM_TB,))]),
        out_shape=[jax.ShapeDtypeStruct((rows, D_MODEL), F32),
                   jax.ShapeDtypeStruct(v3.shape, F32),
                   jax.ShapeDtypeStruct(s0.shape, F32)],
        compiler_params=_params(1),
        name="ffn_stream",
    )(log_g, h, nw, wg, wu, wd, cols(q_s), cols(k_s), v3, gn_w, gn_b, s0)
    return h_out, on_s.reshape(bs, -1), s1


def _rotary_tables(pos):
    theta = 1.0 / (ROPE_BASE ** jnp.linspace(0.0, 1.0, RET_KDIM // 2, dtype=F32))
    ang = pos.astype(F32)[:, None] * theta[None, :]
    cos = jnp.repeat(jnp.cos(ang), 2, axis=1)
    sin = jnp.stack([-jnp.sin(ang), jnp.sin(ang)], axis=-1).reshape(ang.shape[0], RET_KDIM)
    return cos, sin


def _decay_to_end(log_g, c):
    idx = jnp.arange(c, dtype=F32)
    return jnp.repeat(jnp.exp((c - 1.0 - idx)[:, None] * log_g[None, :]), RET_KDIM, axis=1)


def kernel(x_prompt, x_sample, state_pool, state_ret, meta_tokens, norm_mix, norm_ffn, norm_final,
           w_pool, pool_scale, w_ret_in, ret_gn_w, ret_gn_b, w_ret_out, w_ffn_gate, w_ffn_up, w_ffn_down):
    b, t, _ = x_prompt.shape
    bs = x_sample.shape[0]
    log_g = jnp.log(1.0 - 2.0 ** (-5.0 - jnp.arange(RET_HEADS, dtype=F32)))

    wp = w_pool[0].astype(BF)
    w_in = w_ret_in[0].astype(BF)
    w_out = w_ret_out[0].astype(BF)
    wg = [w_ffn_gate[i].astype(BF) for i in range(2)]
    wu = [w_ffn_up[i].astype(BF) for i in range(2)]
    wd = [w_ffn_down[i].astype(BF) for i in range(2)]
    nm0, nm1 = norm_mix[0:1], norm_mix[1:2]
    nf0, nf1 = norm_ffn[0:1], norm_ffn[1:2]
    nfin = norm_final[None, :]
    scale = pool_scale[0:1]
    gn_w, gn_b = ret_gn_w[0:1], ret_gn_b[0:1]
    inv_w = _band_matrices()[4]

    n_small = bs + N_META
    h, h_meta, tail = _pool_prompt(meta_tokens, x_prompt, nm0, wp, scale)
    hs, new_pool_sample = _pool_sample(x_sample.reshape(bs, D_MODEL), state_pool.transpose(0, 2, 1, 3),
                                       nm0, wp, scale, inv_w)
    new_pool_sample = new_pool_sample.transpose(0, 2, 1, 3)
    h_small = _ffn(jnp.concatenate([hs, h_meta], axis=0), nf0, wg[0], wu[0], wd[0], n_small)

    pos_small = jnp.concatenate([jnp.full((bs,), PAST_LEN, jnp.int32), jnp.arange(N_META, dtype=jnp.int32)])
    cos_s, sin_s = _rotary_tables(pos_small)
    ones_s = jnp.ones((bs, D_MODEL), F32)
    cross_s = jnp.ones((n_small, D_MODEL), F32)
    kdec_s = jnp.concatenate([ones_s, _decay_to_end(log_g, N_META)], axis=0)
    qs, _, ks, kds, vs, gs = _proj(h_small, nm1, w_in, cos_s, sin_s, cross_s, kdec_s, n_small)
    h, on_s, new_ret_sample = _ffn_stream(h.reshape(b * t, D_MODEL), nf0, wg[0], wu[0], wd[0], TM,
                                          log_g, qs[:bs], ks[:bs], vs[:bs], gn_w, gn_b, state_ret)
    y_sample = _out_ffn(h_small, on_s.astype(BF), gs, w_out, nf1, wg[1], wu[1], wd[1], nfin, bs, rows=bs)

    cos, sin = _rotary_tables(N_META + jnp.arange(t, dtype=jnp.int32))
    idx = jnp.arange(SUB, dtype=F32)
    rel = idx[:, None] - idx[None, :]
    dmask = jnp.exp(jnp.where(rel[None] >= 0, rel[None] * log_g[:, None, None], -jnp.inf))
    cross = jnp.repeat(jnp.exp((idx[:, None] + 1.0) * log_g[None, :]), RET_KDIM, axis=1)
    reps = (TM // SUB, 1)
    q, qx, k, kd, v, g = _proj(h, nm1, w_in, cos, sin, jnp.tile(cross, reps),
                               jnp.tile(_decay_to_end(log_g, SUB), reps), TM)
    seq = lambda a: a.reshape(b, t, -1)
    on, new_ret_prompt = _ret_prompt(jnp.exp(SUB * log_g), kds[bs:], vs[bs:], dmask,
                                     seq(q), seq(qx), seq(k), seq(kd), seq(v), gn_w, gn_b)
    y = _out_ffn(h, on.reshape(b * t, -1), g, w_out, nf1, wg[1], wu[1], wd[1], nfin, TM)

    return (y.reshape(b, t, D_MODEL), y_sample.reshape(bs, 1, D_MODEL), tail[:, 1:][None],
            new_pool_sample, new_ret_prompt, new_ret_sample)
```

```python
import functools

import numpy as np
import jax
import jax.numpy as jnp
from jax import lax
from jax.experimental import pallas as pl
from jax.experimental.pallas import tpu as pltpu

D_MODEL = 1024
N_META = 16
PAST_LEN = 16384
POOL_WINDOWS = (2, 4, 8, 16)
POOL_GROUPS = len(POOL_WINDOWS)
POOL_GROUP_DIM = D_MODEL // POOL_GROUPS
POOL_BUF = max(POOL_WINDOWS) - 1
RET_HEADS = 4
RET_KDIM = D_MODEL // RET_HEADS
RET_VDIM = 2 * D_MODEL // RET_HEADS
ROPE_BASE = 10000.0
D_FF = 2816
EPS = 1e-6

BF = jnp.bfloat16
F32 = jnp.float32

SUB = 256
TM = 512
FF_CHUNKS = tuple((c, min(c + 512, D_FF)) for c in range(0, D_FF, 512))
VMEM_LIMIT = 56 * 1024 * 1024


def _dot(a, b):
    return jnp.dot(a, b, preferred_element_type=F32)


def _rms(x, g):
    ms = jnp.mean(x * x, axis=-1, keepdims=True)
    return (x * lax.rsqrt(ms + EPS)) * g


def _silu(x):
    return x * jax.nn.sigmoid(x)


def _const_spec(shape):
    nd = len(shape)
    return pl.BlockSpec(shape, lambda *_: (0,) * nd, pipeline_mode=pl.Buffered(1))


def _layer_spec(stack, layer):
    nd = stack.ndim - 1
    return pl.BlockSpec((None,) + stack.shape[1:], lambda *_: (layer,) + (0,) * nd,
                        pipeline_mode=pl.Buffered(1))


def _params(n_grid):
    return pltpu.CompilerParams(
        dimension_semantics=("arbitrary",) * n_grid, vmem_limit_bytes=VMEM_LIMIT)


def _swiglu_chunk(hn, cols, wg_ref, wu_ref, wd_ref):
    c0, c1 = cols
    gt = _dot(hn, wg_ref[:, c0:c1])
    up = _dot(hn, wu_ref[:, c0:c1])
    act = (_silu(gt) * up).astype(BF)
    return _dot(act, wd_ref[c0:c1, :])


def _swiglu_residual(h, nw_ref, wg_ref, wu_ref, wd_ref):
    hn = _rms(h, nw_ref[...]).astype(BF)
    acc = h
    for cols in FF_CHUNKS:
        acc = acc + _swiglu_chunk(hn, cols, wg_ref, wu_ref, wd_ref)
    return acc


def _ffn_kernel(h_ref, nw_ref, wg_ref, wu_ref, wd_ref, o_ref):
    o_ref[...] = _swiglu_residual(h_ref[...], nw_ref, wg_ref, wu_ref, wd_ref)


def _ffn(h, nw, ffn_w, layer, tm):
    rows = h.shape[0]
    row_spec = pl.BlockSpec((tm, D_MODEL), lambda i: (i, 0))
    return pl.pallas_call(
        _ffn_kernel,
        grid=(rows // tm,),
        in_specs=[row_spec, _const_spec(nw.shape)] + [_layer_spec(w, layer) for w in ffn_w],
        out_specs=row_spec,
        out_shape=jax.ShapeDtypeStruct((rows, D_MODEL), F32),
        compiler_params=_params(1),
        name="ffn",
    )(h, nw, *ffn_w)


def _out_ffn_kernel(h_ref, on_ref, g_ref, wo_ref, nw_ref, wg_ref, wu_ref, wd_ref, nf_ref, o_ref):
    gated = (_silu(g_ref[...].astype(F32)) * on_ref[...].astype(F32)).astype(BF)
    h = h_ref[...] + _dot(gated, wo_ref[...])
    o_ref[...] = _rms(_swiglu_residual(h, nw_ref, wg_ref, wu_ref, wd_ref), nf_ref[...])


def _out_ffn(h, on, g, wo, nw, ffn_w, layer, nf, tm, rows=None):
    rows = h.shape[0] if rows is None else rows
    row_spec = lambda width: pl.BlockSpec((tm, width), lambda i: (i, 0))
    return pl.pallas_call(
        _out_ffn_kernel,
        grid=(rows // tm,),
        in_specs=[row_spec(D_MODEL), row_spec(2 * D_MODEL), row_spec(2 * D_MODEL),
                  _const_spec(wo.shape), _const_spec(nw.shape)]
                 + [_layer_spec(w, layer) for w in ffn_w] + [_const_spec(nf.shape)],
        out_specs=row_spec(D_MODEL),
        out_shape=jax.ShapeDtypeStruct((rows, D_MODEL), F32),
        compiler_params=_params(1),
        name="out_ffn",
    )(h, on, g, wo, nw, *ffn_w, nf)


def _band_matrices():
    t = np.arange(SUB)[:, None]
    s = np.arange(SUB)[None, :]
    sh = np.arange(N_META)[None, :] - N_META
    cur = np.stack([((t - s >= 0) & (t - s < w)) for w in POOL_WINDOWS]).astype(np.float32)
    halo = np.stack([(t - sh < w) for w in POOL_WINDOWS]).astype(np.float32)
    tm = np.arange(N_META)[:, None]
    sm = np.arange(N_META)[None, :]
    meta = np.stack([((tm - sm >= 0) & (tm - sm < w)) for w in POOL_WINDOWS]).astype(np.float32)
    inv_meta = np.concatenate(
        [np.repeat(1.0 / np.minimum(w, tm + 1.0), POOL_GROUP_DIM, axis=1) for w in POOL_WINDOWS], axis=1)
    inv_w = np.concatenate([np.full((1, POOL_GROUP_DIM), 1.0 / w) for w in POOL_WINDOWS], axis=1)
    return (jnp.asarray(cur, BF), jnp.asarray(halo, BF), jnp.asarray(meta, BF),
            jnp.asarray(inv_meta, F32), jnp.asarray(inv_w, F32))


def _group_cols(g):
    return slice(g * POOL_GROUP_DIM, (g + 1) * POOL_GROUP_DIM)


def _pool_mix(pooled_sum, inv_cnt, xn, wp_ref, scale):
    diff = (pooled_sum * inv_cnt - xn).astype(BF)
    mixed = jnp.concatenate(
        [_dot(diff[:, _group_cols(g)], wp_ref[g]) for g in range(POOL_GROUPS)], axis=1)
    return mixed * scale


def _pool_prompt_kernel(meta_ref, x_ref, nw_ref, wp_ref, sc_ref, bc_ref, bh_ref, bm_ref,
                        icm_ref, iw_ref, h_ref, hm_ref, tail_ref):
    nw = nw_ref[...]
    scale = sc_ref[...]

    xm = meta_ref[...]
    xn = _rms(xm, nw)
    xb = xn.astype(BF)
    pooled = jnp.concatenate(
        [_dot(bm_ref[g], xb[:, _group_cols(g)]) for g in range(POOL_GROUPS)], axis=1)
    hm_ref[...] = xm + _pool_mix(pooled, icm_ref[...], xn, wp_ref, scale)
    halo = xb

    inv_w = iw_ref[...]
    for j in range(x_ref.shape[1] // SUB):
        rows = slice(SUB * j, SUB * (j + 1))
        x = x_ref[0, rows, :]
        xn = _rms(x, nw)
        xb = xn.astype(BF)
        pooled = jnp.concatenate(
            [_dot(bc_ref[g], xb[:, _group_cols(g)]) + _dot(bh_ref[g], halo[:, _group_cols(g)])
             for g in range(POOL_GROUPS)], axis=1)
        h_ref[0, rows, :] = x + _pool_mix(pooled, inv_w, xn, wp_ref, scale)
        halo = xb[SUB - N_META:, :]
    tail_ref[0] = xn[SUB - N_META:, :]


def _pool_prompt(meta, x, nw, wp, scale):
    b, t, _ = x.shape
    bands = _band_matrices()
    consts = (meta, nw, wp, scale) + bands
    seq_spec = pl.BlockSpec((1, t, D_MODEL), lambda i: (i, 0, 0))
    return pl.pallas_call(
        _pool_prompt_kernel,
        grid=(b,),
        in_specs=[_const_spec(meta.shape), seq_spec] + [_const_spec(c.shape) for c in consts[1:]],
        out_specs=[seq_spec,
                   pl.BlockSpec((N_META, D_MODEL), lambda i: (0, 0)),
                   pl.BlockSpec((1, N_META, D_MODEL), lambda i: (i, 0, 0))],
        out_shape=[jax.ShapeDtypeStruct((b, t, D_MODEL), F32),
                   jax.ShapeDtypeStruct((N_META, D_MODEL), F32),
                   jax.ShapeDtypeStruct((b, N_META, D_MODEL), F32)],
        compiler_params=_params(1),
        name="pool_prompt",
    )(meta, x, *consts[1:])


def _pool_sample_kernel(x_ref, prev_ref, nw_ref, wp_ref, sc_ref, iw_ref, h_ref, np_ref):
    x = x_ref[...]
    xn = _rms(x, nw_ref[...])
    sums = []
    for g, w in enumerate(POOL_WINDOWS):
        s = xn[:, _group_cols(g)]
        for j in range(1, w):
            s = s + prev_ref[POOL_BUF - j, :, _group_cols(g)]
        sums.append(s)
    pooled = jnp.concatenate(sums, axis=1)
    h_ref[...] = x + _pool_mix(pooled, iw_ref[...], xn, wp_ref, sc_ref[...])
    for r in range(POOL_BUF - 1):
        np_ref[r] = prev_ref[r + 1]
    np_ref[POOL_BUF - 1] = xn


def _pool_sample(x, prev, nw, wp, scale, inv_w, tb=32):
    b = x.shape[0]
    st_spec = pl.BlockSpec((None, POOL_BUF, tb, D_MODEL), lambda i: (0, 0, i, 0))
    return pl.pallas_call(
        _pool_sample_kernel,
        grid=(b // tb,),
        in_specs=[pl.BlockSpec((tb, D_MODEL), lambda i: (i, 0)), st_spec,
                  _const_spec(nw.shape), _const_spec(wp.shape), _const_spec(scale.shape),
                  _const_spec(inv_w.shape)],
        out_specs=[pl.BlockSpec((tb, D_MODEL), lambda i: (i, 0)), st_spec],
        out_shape=[jax.ShapeDtypeStruct((b, D_MODEL), F32),
                   jax.ShapeDtypeStruct(prev.shape, F32)],
        compiler_params=_params(1),
        name="pool_sample",
    )(x, prev, nw, wp, scale, inv_w)


def _proj_kernel(h_ref, nw_ref, w_ref, cos_ref, sin_ref, cross_ref, kdec_ref,
                 q_ref, qx_ref, k_ref, kd_ref, v_ref, g_ref):
    hn = _rms(h_ref[...], nw_ref[...]).astype(BF)
    cos = jnp.concatenate([cos_ref[...]] * RET_HEADS, axis=1)
    sin = jnp.concatenate([sin_ref[...]] * RET_HEADS, axis=1)
    lane = lax.broadcasted_iota(jnp.int32, cos.shape, 1)
    even = (lane & 1) == 0

    def rotary(x):
        partner = jnp.where(even, pltpu.roll(x, D_MODEL - 1, 1), pltpu.roll(x, 1, 1))
        return x * cos + partner * sin

    q = rotary(_dot(hn, w_ref[:, 0:D_MODEL])) * (RET_KDIM ** -0.5)
    q_ref[...] = q.astype(BF)
    qx_ref[...] = (q * cross_ref[...]).astype(BF)
    k = rotary(_dot(hn, w_ref[:, D_MODEL:2 * D_MODEL]))
    k_ref[...] = k.astype(BF)
    kd_ref[...] = (k * kdec_ref[...]).astype(BF)
    v_ref[...] = _dot(hn, w_ref[:, 2 * D_MODEL:4 * D_MODEL]).astype(BF)
    g_ref[...] = _dot(hn, w_ref[:, 4 * D_MODEL:6 * D_MODEL]).astype(BF)


def _proj(h, nw, w_in, cos, sin, cross, kdec, tm):
    rows = h.shape[0]
    row_spec = lambda width: pl.BlockSpec((tm, width), lambda i: (i, 0))

    def tab_spec(tab):
        ntab = tab.shape[0] // tm
        return pl.BlockSpec((tm, tab.shape[1]), lambda i: (i % ntab, 0))

    widths = (D_MODEL, D_MODEL, D_MODEL, D_MODEL, 2 * D_MODEL, 2 * D_MODEL)
    return pl.pallas_call(
        _proj_kernel,
        grid=(rows // tm,),
        in_specs=[row_spec(D_MODEL), _const_spec(nw.shape), _const_spec(w_in.shape),
                  tab_spec(cos), tab_spec(sin), tab_spec(cross), tab_spec(kdec)],
        out_specs=[row_spec(w) for w in widths],
        out_shape=[jax.ShapeDtypeStruct((rows, w), BF) for w in widths],
        compiler_params=_params(1),
        name="ret_proj",
    )(h, nw, w_in, cos, sin, cross, kdec)


def _group_norm(o, gn_w, gn_b):
    mu = jnp.mean(o, axis=-1, keepdims=True)
    cen = o - mu
    var = jnp.mean(cen * cen, axis=-1, keepdims=True)
    return (cen * lax.rsqrt(var + EPS)) * gn_w + gn_b


def _kt_v(k, v):
    return lax.dot_general(k, v, (((0,), (0,)), ((), ())), preferred_element_type=F32)


def _head_cols(h, width):
    return slice(h * width, (h + 1) * width)


def _ret_prompt_kernel(tot_ref, kdm_ref, vm_ref, dm_ref, q_ref, qx_ref, k_ref, kd_ref, v_ref,
                       gw_ref, gb_ref, o_ref, s_ref, state):
    t = pl.program_id(1)

    @pl.when(t == 0)
    def _():
        for h in range(RET_HEADS):
            state[h] = _kt_v(kdm_ref[:, _head_cols(h, RET_KDIM)], vm_ref[:, _head_cols(h, RET_VDIM)])

    for c in range(q_ref.shape[1] // SUB):
        rows = slice(c * SUB, (c + 1) * SUB)
        for h in range(RET_HEADS):
            kc = _head_cols(h, RET_KDIM)
            vc = _head_cols(h, RET_VDIM)
            q = q_ref[0, rows, kc]
            v = v_ref[0, rows, vc]
            s_prev = state[h]
            scores = lax.dot_general(q, k_ref[0, rows, kc], (((1,), (1,)), ((), ())),
                                     preferred_element_type=F32) * dm_ref[h]
            o = _dot(scores.astype(BF), v) + _dot(qx_ref[0, rows, kc], s_prev.astype(BF))
            state[h] = tot_ref[h] * s_prev + _kt_v(kd_ref[0, rows, kc], v)
            o_ref[0, rows, vc] = _group_norm(o, gw_ref[:, vc], gb_ref[:, vc]).astype(BF)

    @pl.when(t == pl.num_programs(1) - 1)
    def _():
        s_ref[0] = state[...]


def _ret_prompt(chunk_decay, kd_meta, v_meta, dmask, q, qx, k, kd, v, gn_w, gn_b, tt=2 * SUB):
    b, t, _ = q.shape
    qk_spec = pl.BlockSpec((1, tt, D_MODEL), lambda i, j, s: (i, j, 0))
    v_spec = pl.BlockSpec((1, tt, 2 * D_MODEL), lambda i, j, s: (i, j, 0))
    const = lambda a: pl.BlockSpec(a.shape, lambda i, j, s: (0,) * a.ndim, pipeline_mode=pl.Buffered(1))
    st_shape = (RET_HEADS, RET_KDIM, RET_VDIM)
    return pl.pallas_call(
        _ret_prompt_kernel,
        grid_spec=pltpu.PrefetchScalarGridSpec(
            num_scalar_prefetch=1,
            grid=(b, t // tt),
            in_specs=[const(kd_meta), const(v_meta), const(dmask),
                      qk_spec, qk_spec, qk_spec, qk_spec, v_spec, const(gn_w), const(gn_b)],
            out_specs=[v_spec,
                       pl.BlockSpec((None, 1) + st_shape, lambda i, j, s: (0, i, 0, 0, 0))],
            scratch_shapes=[pltpu.VMEM(st_shape, F32)]),
        out_shape=[jax.ShapeDtypeStruct((b, t, 2 * D_MODEL), BF),
                   jax.ShapeDtypeStruct((1, b) + st_shape, F32)],
        compiler_params=_params(2),
        name="ret_prompt",
    )(chunk_decay, kd_meta, v_meta, dmask, q, qx, k, kd, v, gn_w, gn_b)


STREAM_TB = 4


def _sample_state_update(s, lg_ref, qt, kt, v_ref, gw_ref, gb_ref, s_in, s_out, on_ref):
    for h in range(RET_HEADS):
        gamma = jnp.exp(jnp.full((1, 1), 1.0, F32) * lg_ref[h])
        kc = _head_cols(h, RET_KDIM)
        vc = _head_cols(h, RET_VDIM)
        q = qt[kc, s:s + 1]
        k = kt[kc, s:s + 1]
        v = v_ref[s:s + 1, vc]
        s_prev = s_in[s, h]
        qs = jnp.sum(q * s_prev, axis=0, keepdims=True)
        score = jnp.sum(q * k, axis=0, keepdims=True)
        o = score * v + qs * gamma
        s_out[s, h] = gamma * s_prev + k * v
        on_ref[s:s + 1, vc] = _group_norm(o, gw_ref[:, vc], gb_ref[:, vc])


def _ffn_stream_kernel(lg_ref, h_ref, nw_ref, wg_ref, wu_ref, wd_ref, qt_ref, kt_ref, v_ref,
                       gw_ref, gb_ref, s_hbm, o_ref, on_ref, so_hbm, s_in, s_out, in_sem, out_sem):
    i = pl.program_id(0)
    n = pl.num_programs(0)

    def in_copy(step, s):
        return pltpu.make_async_copy(s_hbm.at[0, step * STREAM_TB + s], s_in.at[s], in_sem.at[s])

    def out_copy(step, s):
        return pltpu.make_async_copy(s_out.at[s], so_hbm.at[0, step * STREAM_TB + s], out_sem.at[s])

    @pl.when(i == 0)
    def _():
        for s in range(STREAM_TB):
            in_copy(0, s).start()

    lanes = qt_ref.shape[1]
    shift = (lanes - i * STREAM_TB) % lanes
    qt = pltpu.roll(qt_ref[...], shift, 1)
    kt = pltpu.roll(kt_ref[...], shift, 1)

    h = h_ref[...]
    hn = _rms(h, nw_ref[...]).astype(BF)
    acc = h
    chunks_after = np.array_split(np.arange(len(FF_CHUNKS)), STREAM_TB)
    for s in range(STREAM_TB):
        in_copy(i, s).wait()

        @pl.when(i > 0)
        def _():
            out_copy(i - 1, s).wait()

        _sample_state_update(s, lg_ref, qt, kt, v_ref, gw_ref, gb_ref, s_in, s_out, on_ref)
        out_copy(i, s).start()

        @pl.when(i + 1 < n)
        def _():
            in_copy(i + 1, s).start()

        for c in chunks_after[s]:
            acc = acc + _swiglu_chunk(hn, FF_CHUNKS[c], wg_ref, wu_ref, wd_ref)
    o_ref[...] = acc

    @pl.when(i == n - 1)
    def _():
        for s in range(STREAM_TB):
            out_copy(i, s).wait()


def _ffn_stream(h, nw, ffn_w, layer, tm, log_g, q_s, k_s, v_s, gn_w, gn_b, s0):
    rows = h.shape[0]
    n = rows // tm
    bs = q_s.shape[0]
    assert bs == n * STREAM_TB
    qt = q_s.astype(F32).T
    kt = k_s.astype(F32).T
    v3 = v_s.astype(F32).reshape(n, STREAM_TB, -1)
    row_spec = pl.BlockSpec((tm, D_MODEL), lambda i, lg: (i, 0))
    v_spec = pl.BlockSpec((None, STREAM_TB, 2 * D_MODEL), lambda i, lg: (i, 0, 0))
    hbm = pl.BlockSpec(memory_space=pl.ANY)
    blk = (STREAM_TB, RET_HEADS, RET_KDIM, RET_VDIM)
    h_out, on_s, s1 = pl.pallas_call(
        _ffn_stream_kernel,
        grid_spec=pltpu.PrefetchScalarGridSpec(
            num_scalar_prefetch=1,
            grid=(n,),
            in_specs=[row_spec, _const_spec(nw.shape)] + [_layer_spec(w, layer) for w in ffn_w]
                     + [_const_spec(qt.shape), _const_spec(kt.shape), v_spec,
                        _const_spec(gn_w.shape), _const_spec(gn_b.shape), hbm],
            out_specs=[row_spec, v_spec, hbm],
            scratch_shapes=[pltpu.VMEM(blk, F32), pltpu.VMEM(blk, F32),
                            pltpu.SemaphoreType.DMA((STREAM_TB,)), pltpu.SemaphoreType.DMA((STREAM_TB,))]),
        out_shape=[jax.ShapeDtypeStruct((rows, D_MODEL), F32),
                   jax.ShapeDtypeStruct(v3.shape, F32),
                   jax.ShapeDtypeStruct(s0.shape, F32)],
        compiler_params=_params(1),
        name="ffn_stream",
    )(log_g, h, nw, *ffn_w, qt, kt, v3, gn_w, gn_b, s0)
    return h_out, on_s.reshape(bs, -1), s1


def _rotary_tables(pos):
    theta = 1.0 / (ROPE_BASE ** jnp.linspace(0.0, 1.0, RET_KDIM // 2, dtype=F32))
    ang = pos.astype(F32)[:, None] * theta[None, :]
    cos = jnp.repeat(jnp.cos(ang), 2, axis=1)
    sin = jnp.stack([-jnp.sin(ang), jnp.sin(ang)], axis=-1).reshape(ang.shape[0], RET_KDIM)
    return cos, sin


def _decay_to_end(log_g, c):
    idx = jnp.arange(c, dtype=F32)
    return jnp.repeat(jnp.exp((c - 1.0 - idx)[:, None] * log_g[None, :]), RET_KDIM, axis=1)


def kernel(x_prompt, x_sample, state_pool, state_ret, meta_tokens, norm_mix, norm_ffn, norm_final,
           w_pool, pool_scale, w_ret_in, ret_gn_w, ret_gn_b, w_ret_out, w_ffn_gate, w_ffn_up, w_ffn_down):
    b, t, _ = x_prompt.shape
    bs = x_sample.shape[0]
    log_g = jnp.log(1.0 - 2.0 ** (-5.0 - jnp.arange(RET_HEADS, dtype=F32)))

    wp = w_pool[0].astype(BF)
    w_in = w_ret_in[0].astype(BF)
    w_out = w_ret_out[0].astype(BF)
    ffn_w = (w_ffn_gate.astype(BF), w_ffn_up.astype(BF), w_ffn_down.astype(BF))
    nm0, nm1 = norm_mix[0:1], norm_mix[1:2]
    nf0, nf1 = norm_ffn[0:1], norm_ffn[1:2]
    nfin = norm_final[None, :]
    scale = pool_scale[0:1]
    gn_w, gn_b = ret_gn_w[0:1], ret_gn_b[0:1]
    inv_w = _band_matrices()[4]

    n_small = bs + N_META
    h, h_meta, tail = _pool_prompt(meta_tokens, x_prompt, nm0, wp, scale)
    hs, new_pool_sample = _pool_sample(x_sample.reshape(bs, D_MODEL), state_pool.transpose(0, 2, 1, 3),
                                       nm0, wp, scale, inv_w)
    new_pool_sample = new_pool_sample.transpose(0, 2, 1, 3)
    h_small = _ffn(jnp.concatenate([hs, h_meta], axis=0), nf0, ffn_w, 0, n_small)

    pos_small = jnp.concatenate([jnp.full((bs,), PAST_LEN, jnp.int32), jnp.arange(N_META, dtype=jnp.int32)])
    cos_s, sin_s = _rotary_tables(pos_small)
    ones_s = jnp.ones((bs, D_MODEL), F32)
    cross_s = jnp.ones((n_small, D_MODEL), F32)
    kdec_s = jnp.concatenate([ones_s, _decay_to_end(log_g, N_META)], axis=0)
    qs, _, ks, kds, vs, gs = _proj(h_small, nm1, w_in, cos_s, sin_s, cross_s, kdec_s, n_small)
    h, on_s, new_ret_sample = _ffn_stream(h.reshape(b * t, D_MODEL), nf0, ffn_w, 0, TM,
                                          log_g, qs[:bs], ks[:bs], vs[:bs], gn_w, gn_b, state_ret)
    y_sample = _out_ffn(h_small, on_s.astype(BF), gs, w_out, nf1, ffn_w, 1, nfin, bs, rows=bs)

    cos, sin = _rotary_tables(N_META + jnp.arange(t, dtype=jnp.int32))
    idx = jnp.arange(SUB, dtype=F32)
    rel = idx[:, None] - idx[None, :]
    dmask = jnp.exp(jnp.where(rel[None] >= 0, rel[None] * log_g[:, None, None], -jnp.inf))
    cross = jnp.repeat(jnp.exp((idx[:, None] + 1.0) * log_g[None, :]), RET_KDIM, axis=1)
    reps = (TM // SUB, 1)
    q, qx, k, kd, v, g = _proj(h, nm1, w_in, cos, sin, jnp.tile(cross, reps),
                               jnp.tile(_decay_to_end(log_g, SUB), reps), TM)
    seq = lambda a: a.reshape(b, t, -1)
    on, new_ret_prompt = _ret_prompt(jnp.exp(SUB * log_g), kds[bs:], vs[bs:], dmask,
                                     seq(q), seq(qx), seq(k), seq(kd), seq(v), gn_w, gn_b)
    y = _out_ffn(h, on.reshape(b * t, -1), g, w_out, nf1, ffn_w, 1, nfin, TM)

    return (y.reshape(b, t, D_MODEL), y_sample.reshape(bs, 1, D_MODEL), tail[:, 1:][None],
            new_pool_sample, new_ret_prompt, new_ret_sample)
```

```python
import functools

import numpy as np
import jax
import jax.numpy as jnp
from jax import lax
from jax.experimental import pallas as pl
from jax.experimental.pallas import tpu as pltpu

D_MODEL = 1024
N_META = 16
PAST_LEN = 16384
POOL_WINDOWS = (2, 4, 8, 16)
POOL_GROUPS = len(POOL_WINDOWS)
POOL_GROUP_DIM = D_MODEL // POOL_GROUPS
POOL_BUF = max(POOL_WINDOWS) - 1
RET_HEADS = 4
RET_KDIM = D_MODEL // RET_HEADS
RET_VDIM = 2 * D_MODEL // RET_HEADS
ROPE_BASE = 10000.0
D_FF = 2816
EPS = 1e-6

BF = jnp.bfloat16
F32 = jnp.float32

SUB = 256
TM = 512
FF_CHUNKS = tuple((c, min(c + 512, D_FF)) for c in range(0, D_FF, 512))
VMEM_LIMIT = 56 * 1024 * 1024


def _dot(a, b):
    return jnp.dot(a, b, preferred_element_type=F32)


def _rms(x, g):
    ms = jnp.mean(x * x, axis=-1, keepdims=True)
    return (x * lax.rsqrt(ms + EPS)) * g


def _silu(x):
    return x * jax.nn.sigmoid(x)


def _const_spec(shape):
    nd = len(shape)
    return pl.BlockSpec(shape, lambda *_: (0,) * nd, pipeline_mode=pl.Buffered(1))


def _layer_spec(stack, layer):
    nd = stack.ndim - 1
    return pl.BlockSpec((None,) + stack.shape[1:], lambda *_: (layer,) + (0,) * nd,
                        pipeline_mode=pl.Buffered(1))


def _params(n_grid):
    return pltpu.CompilerParams(
        dimension_semantics=("arbitrary",) * n_grid, vmem_limit_bytes=VMEM_LIMIT)


def _swiglu_chunk(hn, cols, wg_ref, wu_ref, wd_ref):
    c0, c1 = cols
    gt = _dot(hn, wg_ref[:, c0:c1])
    up = _dot(hn, wu_ref[:, c0:c1])
    act = (_silu(gt) * up).astype(BF)
    return _dot(act, wd_ref[c0:c1, :])


def _swiglu_residual(h, nw_ref, wg_ref, wu_ref, wd_ref):
    hn = _rms(h, nw_ref[...]).astype(BF)
    acc = h
    for cols in FF_CHUNKS:
        acc = acc + _swiglu_chunk(hn, cols, wg_ref, wu_ref, wd_ref)
    return acc


def _ffn_kernel(h_ref, nw_ref, wg_ref, wu_ref, wd_ref, o_ref):
    o_ref[...] = _swiglu_residual(h_ref[...], nw_ref, wg_ref, wu_ref, wd_ref)


def _ffn(h, nw, ffn_w, layer, tm):
    rows = h.shape[0]
    row_spec = pl.BlockSpec((tm, D_MODEL), lambda i: (i, 0))
    return pl.pallas_call(
        _ffn_kernel,
        grid=(rows // tm,),
        in_specs=[row_spec, _const_spec(nw.shape)] + [_layer_spec(w, layer) for w in ffn_w],
        out_specs=row_spec,
        out_shape=jax.ShapeDtypeStruct((rows, D_MODEL), F32),
        compiler_params=_params(1),
        name="ffn",
    )(h, nw, *ffn_w)


def _out_ffn_kernel(h_ref, on_ref, g_ref, wo_ref, nw_ref, wg_ref, wu_ref, wd_ref, nf_ref, o_ref):
    gated = (_silu(g_ref[...].astype(F32)) * on_ref[...].astype(F32)).astype(BF)
    h = h_ref[...] + _dot(gated, wo_ref[...])
    o_ref[...] = _rms(_swiglu_residual(h, nw_ref, wg_ref, wu_ref, wd_ref), nf_ref[...])


def _out_ffn(h, on, g, wo, nw, ffn_w, layer, nf, tm, rows=None):
    rows = h.shape[0] if rows is None else rows
    row_spec = lambda width: pl.BlockSpec((tm, width), lambda i: (i, 0))
    return pl.pallas_call(
        _out_ffn_kernel,
        grid=(rows // tm,),
        in_specs=[row_spec(D_MODEL), row_spec(2 * D_MODEL), row_spec(2 * D_MODEL),
                  _const_spec(wo.shape), _const_spec(nw.shape)]
                 + [_layer_spec(w, layer) for w in ffn_w] + [_const_spec(nf.shape)],
        out_specs=row_spec(D_MODEL),
        out_shape=jax.ShapeDtypeStruct((rows, D_MODEL), F32),
        compiler_params=_params(1),
        name="out_ffn",
    )(h, on, g, wo, nw, *ffn_w, nf)


def _band_matrices():
    t = np.arange(SUB)[:, None]
    s = np.arange(SUB)[None, :]
    sh = np.arange(N_META)[None, :] - N_META
    cur = np.stack([((t - s >= 0) & (t - s < w)) for w in POOL_WINDOWS]).astype(np.float32)
    halo = np.stack([(t - sh < w) for w in POOL_WINDOWS]).astype(np.float32)
    tm = np.arange(N_META)[:, None]
    sm = np.arange(N_META)[None, :]
    meta = np.stack([((tm - sm >= 0) & (tm - sm < w)) for w in POOL_WINDOWS]).astype(np.float32)
    inv_meta = np.concatenate(
        [np.repeat(1.0 / np.minimum(w, tm + 1.0), POOL_GROUP_DIM, axis=1) for w in POOL_WINDOWS], axis=1)
    inv_w = np.concatenate([np.full((1, POOL_GROUP_DIM), 1.0 / w) for w in POOL_WINDOWS], axis=1)
    return (jnp.asarray(cur, BF), jnp.asarray(halo, BF), jnp.asarray(meta, BF),
            jnp.asarray(inv_meta, F32), jnp.asarray(inv_w, F32))


def _group_cols(g):
    return slice(g * POOL_GROUP_DIM, (g + 1) * POOL_GROUP_DIM)


def _pool_mix(pooled_sum, inv_cnt, xn, wp_ref, scale):
    diff = (pooled_sum * inv_cnt - xn).astype(BF)
    mixed = jnp.concatenate(
        [_dot(diff[:, _group_cols(g)], wp_ref[g]) for g in range(POOL_GROUPS)], axis=1)
    return mixed * scale


def _pool_prompt_tile(x_ref, halo_ref, nw_ref, wp_ref, sc_ref, bc_ref, bh_ref, iw_ref, tail_ref):
    nw = nw_ref[...]
    scale = sc_ref[...]
    inv_w = iw_ref[...]
    halo = halo_ref[...]
    out = []
    for j in range(x_ref.shape[0] // SUB):
        x = x_ref[SUB * j:SUB * (j + 1), :]
        xn = _rms(x, nw)
        xb = xn.astype(BF)
        pooled = jnp.concatenate(
            [_dot(bc_ref[g], xb[:, _group_cols(g)]) + _dot(bh_ref[g], halo[:, _group_cols(g)])
             for g in range(POOL_GROUPS)], axis=1)
        out.append(x + _pool_mix(pooled, inv_w, xn, wp_ref, scale))
        halo = xb[SUB - N_META:, :]
    halo_ref[...] = halo
    tail_ref[0] = xn[SUB - N_META:, :]
    return jnp.concatenate(out, axis=0)


def _pool_sample_kernel(x_ref, prev_ref, meta_ref, nw_ref, wp_ref, sc_ref, iw_ref, bm_ref, icm_ref,
                        h_ref, np_ref, hm_ref):
    @pl.when(pl.program_id(0) == 0)
    def _():
        xm = meta_ref[...]
        xnm = _rms(xm, nw_ref[...])
        xbm = xnm.astype(BF)
        pooled_m = jnp.concatenate(
            [_dot(bm_ref[g], xbm[:, _group_cols(g)]) for g in range(POOL_GROUPS)], axis=1)
        hm_ref[...] = xm + _pool_mix(pooled_m, icm_ref[...], xnm, wp_ref, sc_ref[...])

    x = x_ref[...]
    xn = _rms(x, nw_ref[...])
    sums = []
    for g, w in enumerate(POOL_WINDOWS):
        s = xn[:, _group_cols(g)]
        for j in range(1, w):
            s = s + prev_ref[POOL_BUF - j, :, _group_cols(g)]
        sums.append(s)
    pooled = jnp.concatenate(sums, axis=1)
    h_ref[...] = x + _pool_mix(pooled, iw_ref[...], xn, wp_ref, sc_ref[...])
    for r in range(POOL_BUF - 1):
        np_ref[r] = prev_ref[r + 1]
    np_ref[POOL_BUF - 1] = xn


def _pool_sample(x, prev, meta, nw, wp, scale, tb=32):
    b = x.shape[0]
    _, _, band_meta, inv_meta, inv_w = _band_matrices()
    consts = (meta, nw, wp, scale, inv_w, band_meta, inv_meta)
    st_spec = pl.BlockSpec((None, POOL_BUF, tb, D_MODEL), lambda i: (0, 0, i, 0))
    return pl.pallas_call(
        _pool_sample_kernel,
        grid=(b // tb,),
        in_specs=[pl.BlockSpec((tb, D_MODEL), lambda i: (i, 0)), st_spec]
                 + [_const_spec(c.shape) for c in consts],
        out_specs=[pl.BlockSpec((tb, D_MODEL), lambda i: (i, 0)), st_spec,
                   pl.BlockSpec(meta.shape, lambda i: (0, 0))],
        out_shape=[jax.ShapeDtypeStruct((b, D_MODEL), F32),
                   jax.ShapeDtypeStruct(prev.shape, F32),
                   jax.ShapeDtypeStruct(meta.shape, F32)],
        compiler_params=_params(1),
        name="pool_sample",
    )(x, prev, *consts)


def _proj_kernel(h_ref, nw_ref, w_ref, cos_ref, sin_ref, cross_ref, kdec_ref,
                 q_ref, qx_ref, k_ref, kd_ref, v_ref, g_ref):
    hn = _rms(h_ref[...], nw_ref[...]).astype(BF)
    cos = jnp.concatenate([cos_ref[...]] * RET_HEADS, axis=1)
    sin = jnp.concatenate([sin_ref[...]] * RET_HEADS, axis=1)
    lane = lax.broadcasted_iota(jnp.int32, cos.shape, 1)
    even = (lane & 1) == 0

    def rotary(x):
        partner = jnp.where(even, pltpu.roll(x, D_MODEL - 1, 1), pltpu.roll(x, 1, 1))
        return x * cos + partner * sin

    q = rotary(_dot(hn, w_ref[:, 0:D_MODEL])) * (RET_KDIM ** -0.5)
    q_ref[...] = q.astype(BF)
    qx_ref[...] = (q * cross_ref[...]).astype(BF)
    k = rotary(_dot(hn, w_ref[:, D_MODEL:2 * D_MODEL]))
    k_ref[...] = k.astype(BF)
    kd_ref[...] = (k * kdec_ref[...]).astype(BF)
    v_ref[...] = _dot(hn, w_ref[:, 2 * D_MODEL:4 * D_MODEL]).astype(BF)
    g_ref[...] = _dot(hn, w_ref[:, 4 * D_MODEL:6 * D_MODEL]).astype(BF)


def _proj(h, nw, w_in, cos, sin, cross, kdec, tm):
    rows = h.shape[0]
    row_spec = lambda width: pl.BlockSpec((tm, width), lambda i: (i, 0))

    def tab_spec(tab):
        ntab = tab.shape[0] // tm
        return pl.BlockSpec((tm, tab.shape[1]), lambda i: (i % ntab, 0))

    widths = (D_MODEL, D_MODEL, D_MODEL, D_MODEL, 2 * D_MODEL, 2 * D_MODEL)
    return pl.pallas_call(
        _proj_kernel,
        grid=(rows // tm,),
        in_specs=[row_spec(D_MODEL), _const_spec(nw.shape), _const_spec(w_in.shape),
                  tab_spec(cos), tab_spec(sin), tab_spec(cross), tab_spec(kdec)],
        out_specs=[row_spec(w) for w in widths],
        out_shape=[jax.ShapeDtypeStruct((rows, w), BF) for w in widths],
        compiler_params=_params(1),
        name="ret_proj",
    )(h, nw, w_in, cos, sin, cross, kdec)


def _group_norm(o, gn_w, gn_b):
    mu = jnp.mean(o, axis=-1, keepdims=True)
    cen = o - mu
    var = jnp.mean(cen * cen, axis=-1, keepdims=True)
    return (cen * lax.rsqrt(var + EPS)) * gn_w + gn_b


def _kt_v(k, v):
    return lax.dot_general(k, v, (((0,), (0,)), ((), ())), preferred_element_type=F32)


def _head_cols(h, width):
    return slice(h * width, (h + 1) * width)


def _ret_prompt_kernel(tot_ref, kdm_ref, vm_ref, dm_ref, q_ref, qx_ref, k_ref, kd_ref, v_ref,
                       gw_ref, gb_ref, o_ref, s_ref, state):
    t = pl.program_id(1)

    @pl.when(t == 0)
    def _():
        for h in range(RET_HEADS):
            state[h] = _kt_v(kdm_ref[:, _head_cols(h, RET_KDIM)], vm_ref[:, _head_cols(h, RET_VDIM)])

    for c in range(q_ref.shape[1] // SUB):
        rows = slice(c * SUB, (c + 1) * SUB)
        for h in range(RET_HEADS):
            kc = _head_cols(h, RET_KDIM)
            vc = _head_cols(h, RET_VDIM)
            q = q_ref[0, rows, kc]
            v = v_ref[0, rows, vc]
            s_prev = state[h]
            scores = lax.dot_general(q, k_ref[0, rows, kc], (((1,), (1,)), ((), ())),
                                     preferred_element_type=F32) * dm_ref[h]
            o = _dot(scores.astype(BF), v) + _dot(qx_ref[0, rows, kc], s_prev.astype(BF))
            state[h] = tot_ref[h] * s_prev + _kt_v(kd_ref[0, rows, kc], v)
            o_ref[0, rows, vc] = _group_norm(o, gw_ref[:, vc], gb_ref[:, vc]).astype(BF)

    @pl.when(t == pl.num_programs(1) - 1)
    def _():
        s_ref[0] = state[...]


def _ret_prompt(chunk_decay, kd_meta, v_meta, dmask, q, qx, k, kd, v, gn_w, gn_b, tt=2 * SUB):
    b, t, _ = q.shape
    qk_spec = pl.BlockSpec((1, tt, D_MODEL), lambda i, j, s: (i, j, 0))
    v_spec = pl.BlockSpec((1, tt, 2 * D_MODEL), lambda i, j, s: (i, j, 0))
    const = lambda a: pl.BlockSpec(a.shape, lambda i, j, s: (0,) * a.ndim, pipeline_mode=pl.Buffered(1))
    st_shape = (RET_HEADS, RET_KDIM, RET_VDIM)
    return pl.pallas_call(
        _ret_prompt_kernel,
        grid_spec=pltpu.PrefetchScalarGridSpec(
            num_scalar_prefetch=1,
            grid=(b, t // tt),
            in_specs=[const(kd_meta), const(v_meta), const(dmask),
                      qk_spec, qk_spec, qk_spec, qk_spec, v_spec, const(gn_w), const(gn_b)],
            out_specs=[v_spec,
                       pl.BlockSpec((None, 1) + st_shape, lambda i, j, s: (0, i, 0, 0, 0))],
            scratch_shapes=[pltpu.VMEM(st_shape, F32)]),
        out_shape=[jax.ShapeDtypeStruct((b, t, 2 * D_MODEL), BF),
                   jax.ShapeDtypeStruct((1, b) + st_shape, F32)],
        compiler_params=_params(2),
        name="ret_prompt",
    )(chunk_decay, kd_meta, v_meta, dmask, q, qx, k, kd, v, gn_w, gn_b)


STREAM_TB = 4


def _head_state_update(s, h, lg_ref, qt, kt, v_ref, gw_ref, gb_ref, s_in, s_out, on_ref):
    gamma = jnp.exp(jnp.full((1, 1), 1.0, F32) * lg_ref[h])
    kc = _head_cols(h, RET_KDIM)
    vc = _head_cols(h, RET_VDIM)
    q = qt[kc, s:s + 1]
    k = kt[kc, s:s + 1]
    v = v_ref[s:s + 1, vc]
    s_prev = s_in[s, h]
    qs = jnp.sum(q * s_prev, axis=0, keepdims=True)
    score = jnp.sum(q * k, axis=0, keepdims=True)
    o = score * v + qs * gamma
    s_out[s, h] = gamma * s_prev + k * v
    on_ref[s:s + 1, vc] = _group_norm(o, gw_ref[:, vc], gb_ref[:, vc])


def _layer0_stream_kernel(tiles_per_seq, lg_ref, x_ref, meta_ref, nm_ref, wp_ref, sc_ref, bc_ref, bh_ref,
                          iw_ref, nw_ref, wg_ref, wu_ref, wd_ref, qt_ref, kt_ref, v_ref, gw_ref, gb_ref,
                          s_hbm, o_ref, tail_ref, on_ref, so_hbm, halo, s_in, s_out, in_sem, out_sem):
    i = pl.program_id(0)
    n = pl.num_programs(0)

    @pl.when(i % tiles_per_seq == 0)
    def _():
        halo[...] = _rms(meta_ref[...], nm_ref[...]).astype(BF)

    def in_copy(step, s):
        return pltpu.make_async_copy(s_hbm.at[0, step * STREAM_TB + s], s_in.at[s], in_sem.at[s])

    def out_copy(step, s):
        return pltpu.make_async_copy(s_out.at[s], so_hbm.at[0, step * STREAM_TB + s], out_sem.at[s])

    @pl.when(i == 0)
    def _():
        for s in range(STREAM_TB):
            in_copy(0, s).start()

    lanes = qt_ref.shape[1]
    shift = (lanes - i * STREAM_TB) % lanes
    qt = pltpu.roll(qt_ref[...], shift, 1)
    kt = pltpu.roll(kt_ref[...], shift, 1)

    h = _pool_prompt_tile(x_ref, halo, nm_ref, wp_ref, sc_ref, bc_ref, bh_ref, iw_ref, tail_ref)
    hn = _rms(h, nw_ref[...]).astype(BF)

    ffn = {"acc": h, "act": None}

    def up_unit(cols):
        c0, c1 = cols
        gt = _dot(hn, wg_ref[:, c0:c1])
        up = _dot(hn, wu_ref[:, c0:c1])
        ffn["act"] = (_silu(gt) * up).astype(BF)

    def down_unit(cols):
        c0, c1 = cols
        ffn["acc"] = ffn["acc"] + _dot(ffn["act"], wd_ref[c0:c1, :])

    units = [functools.partial(u, cols) for cols in FF_CHUNKS for u in (up_unit, down_unit)]
    per_sample = np.array_split(np.arange(len(units)), STREAM_TB)
    for s in range(STREAM_TB):
        in_copy(i, s).wait()

        @pl.when(i > 0)
        def _():
            out_copy(i - 1, s).wait()

        mine = list(per_sample[s])
        for hd in range(RET_HEADS):
            _head_state_update(s, hd, lg_ref, qt, kt, v_ref, gw_ref, gb_ref, s_in, s_out, on_ref)
            if hd == RET_HEADS - 1:
                out_copy(i, s).start()

                @pl.when(i + 1 < n)
                def _():
                    in_copy(i + 1, s).start()
            take = len(mine) - (len(mine) * (RET_HEADS - 1 - hd)) // RET_HEADS
            for u in mine[:take]:
                units[u]()
            mine = mine[take:]
    o_ref[...] = ffn["acc"]

    @pl.when(i == n - 1)
    def _():
        for s in range(STREAM_TB):
            out_copy(i, s).wait()


def _layer0_stream(x, meta, nm, wp, scale, nw, ffn_w, layer, tm, log_g, q_s, k_s, v_s, gn_w, gn_b, s0):
    b, t, _ = x.shape
    rows = b * t
    n = rows // tm
    bs = q_s.shape[0]
    assert bs == n * STREAM_TB and t % tm == 0
    tiles_per_seq = t // tm
    band_cur, band_halo, _, _, inv_w = _band_matrices()
    qt = q_s.astype(F32).T
    kt = k_s.astype(F32).T
    v3 = v_s.astype(F32).reshape(n, STREAM_TB, -1)
    row_spec = pl.BlockSpec((tm, D_MODEL), lambda i, lg: (i, 0))
    tail_spec = pl.BlockSpec((1, N_META, D_MODEL), lambda i, lg: (i // tiles_per_seq, 0, 0))
    v_spec = pl.BlockSpec((None, STREAM_TB, 2 * D_MODEL), lambda i, lg: (i, 0, 0))
    hbm = pl.BlockSpec(memory_space=pl.ANY)
    blk = (STREAM_TB, RET_HEADS, RET_KDIM, RET_VDIM)
    pool_consts = (meta, nm, wp, scale, band_cur, band_halo, inv_w, nw)
    stream_consts = (qt, kt)
    h_out, tail, on_s, s1 = pl.pallas_call(
        functools.partial(_layer0_stream_kernel, tiles_per_seq),
        grid_spec=pltpu.PrefetchScalarGridSpec(
            num_scalar_prefetch=1,
            grid=(n,),
            in_specs=[row_spec] + [_const_spec(c.shape) for c in pool_consts]
                     + [_layer_spec(w, layer) for w in ffn_w]
                     + [_const_spec(c.shape) for c in stream_consts]
                     + [v_spec, _const_spec(gn_w.shape), _const_spec(gn_b.shape), hbm],
            out_specs=[row_spec, tail_spec, v_spec, hbm],
            scratch_shapes=[pltpu.VMEM((N_META, D_MODEL), BF),
                            pltpu.VMEM(blk, F32), pltpu.VMEM(blk, F32),
                            pltpu.SemaphoreType.DMA((STREAM_TB,)), pltpu.SemaphoreType.DMA((STREAM_TB,))]),
        out_shape=[jax.ShapeDtypeStruct((rows, D_MODEL), F32),
                   jax.ShapeDtypeStruct((b, N_META, D_MODEL), F32),
                   jax.ShapeDtypeStruct(v3.shape, F32),
                   jax.ShapeDtypeStruct(s0.shape, F32)],
        compiler_params=_params(1),
        name="layer0_stream",
    )(log_g, x.reshape(rows, D_MODEL), *pool_consts, *ffn_w, *stream_consts, v3, gn_w, gn_b, s0)
    return h_out, tail, on_s.reshape(bs, -1), s1


def _rotary_tables(pos):
    theta = 1.0 / (ROPE_BASE ** jnp.linspace(0.0, 1.0, RET_KDIM // 2, dtype=F32))
    ang = pos.astype(F32)[:, None] * theta[None, :]
    cos = jnp.repeat(jnp.cos(ang), 2, axis=1)
    sin = jnp.stack([-jnp.sin(ang), jnp.sin(ang)], axis=-1).reshape(ang.shape[0], RET_KDIM)
    return cos, sin


LOG_GAMMA = np.log(1.0 - 2.0 ** (-5.0 - np.arange(RET_HEADS)))


def _per_head_cols(table):
    return jnp.asarray(np.repeat(table, RET_KDIM, axis=1), F32)


def _decay_to_end(c):
    return np.exp((c - 1.0 - np.arange(c))[:, None] * LOG_GAMMA[None, :])


def _decay_from_start(c):
    return np.exp((np.arange(c) + 1.0)[:, None] * LOG_GAMMA[None, :])


def _decay_mask(c):
    rel = np.arange(c)[:, None] - np.arange(c)[None, :]
    return np.where(rel[None] >= 0, np.exp(np.maximum(rel, 0)[None] * LOG_GAMMA[:, None, None]), 0.0)


def kernel(x_prompt, x_sample, state_pool, state_ret, meta_tokens, norm_mix, norm_ffn, norm_final,
           w_pool, pool_scale, w_ret_in, ret_gn_w, ret_gn_b, w_ret_out, w_ffn_gate, w_ffn_up, w_ffn_down):
    b, t, _ = x_prompt.shape
    bs = x_sample.shape[0]
    log_g = jnp.asarray(LOG_GAMMA, F32)

    wp = w_pool[0].astype(BF)
    w_in = w_ret_in[0].astype(BF)
    w_out = w_ret_out[0].astype(BF)
    ffn_w = (w_ffn_gate.astype(BF), w_ffn_up.astype(BF), w_ffn_down.astype(BF))
    nm0, nm1 = norm_mix[0:1], norm_mix[1:2]
    nf0, nf1 = norm_ffn[0:1], norm_ffn[1:2]
    nfin = norm_final[None, :]
    scale = pool_scale[0:1]
    gn_w, gn_b = ret_gn_w[0:1], ret_gn_b[0:1]

    n_small = bs + N_META
    hs, new_pool_sample, h_meta = _pool_sample(
        x_sample.reshape(bs, D_MODEL), state_pool.transpose(0, 2, 1, 3), meta_tokens, nm0, wp, scale)
    new_pool_sample = new_pool_sample.transpose(0, 2, 1, 3)
    h_small = _ffn(jnp.concatenate([hs, h_meta], axis=0), nf0, ffn_w, 0, n_small)

    pos_small = jnp.concatenate([jnp.full((bs,), PAST_LEN, jnp.int32), jnp.arange(N_META, dtype=jnp.int32)])
    cos_s, sin_s = _rotary_tables(pos_small)
    cross_s = jnp.ones((n_small, D_MODEL), F32)
    kdec_s = _per_head_cols(np.concatenate([np.ones((bs, RET_HEADS)), _decay_to_end(N_META)], axis=0))
    qs, _, ks, kds, vs, gs = _proj(h_small, nm1, w_in, cos_s, sin_s, cross_s, kdec_s, n_small)
    h, tail, on_s, new_ret_sample = _layer0_stream(
        x_prompt, meta_tokens, nm0, wp, scale, nf0, ffn_w, 0, TM,
        log_g, qs[:bs], ks[:bs], vs[:bs], gn_w, gn_b, state_ret)
    y_sample = _out_ffn(h_small, on_s.astype(BF), gs, w_out, nf1, ffn_w, 1, nfin, bs, rows=bs)

    cos, sin = _rotary_tables(N_META + jnp.arange(t, dtype=jnp.int32))
    reps = (TM // SUB, 1)
    q, qx, k, kd, v, g = _proj(h, nm1, w_in, cos, sin,
                               _per_head_cols(np.tile(_decay_from_start(SUB), reps)),
                               _per_head_cols(np.tile(_decay_to_end(SUB), reps)), TM)
    seq = lambda a: a.reshape(b, t, -1)
    on, new_ret_prompt = _ret_prompt(jnp.asarray(np.exp(SUB * LOG_GAMMA), F32), kds[bs:], vs[bs:],
                                     jnp.asarray(_decay_mask(SUB), F32),
                                     seq(q), seq(qx), seq(k), seq(kd), seq(v), gn_w, gn_b)
    y = _out_ffn(h, on.reshape(b * t, -1), g, w_out, nf1, ffn_w, 1, nfin, TM)

    return (y.reshape(b, t, D_MODEL), y_sample.reshape(bs, 1, D_MODEL), tail[:, 1:][None],
            new_pool_sample, new_ret_prompt, new_ret_sample)
```

```python
import functools

import numpy as np
import jax
import jax.numpy as jnp
from jax import lax
from jax.experimental import pallas as pl
from jax.experimental.pallas import tpu as pltpu

D_MODEL = 1024
N_META = 16
PAST_LEN = 16384
POOL_WINDOWS = (2, 4, 8, 16)
POOL_GROUPS = len(POOL_WINDOWS)
POOL_GROUP_DIM = D_MODEL // POOL_GROUPS
POOL_BUF = max(POOL_WINDOWS) - 1
RET_HEADS = 4
RET_KDIM = D_MODEL // RET_HEADS
RET_VDIM = 2 * D_MODEL // RET_HEADS
ROPE_BASE = 10000.0
D_FF = 2816
EPS = 1e-6

BF = jnp.bfloat16
F32 = jnp.float32

SUB = 256
TM = 512
FF_CHUNKS = tuple((c, min(c + 512, D_FF)) for c in range(0, D_FF, 512))
FF_STREAM = 256
PROJ_STREAM = 512
assert D_FF % FF_STREAM == 0 and D_MODEL % PROJ_STREAM == 0
V7X_VMEM_BYTES = 64 * 1024 * 1024
VMEM_LIMIT = V7X_VMEM_BYTES * 7 // 8

LOG_GAMMA = np.log(1.0 - 2.0 ** (-5.0 - np.arange(RET_HEADS)))


def _dot(a, b):
    return jnp.dot(a, b, preferred_element_type=F32)


def _kt_v(k, v):
    return lax.dot_general(k, v, (((0,), (0,)), ((), ())), preferred_element_type=F32)


def _rms(x, g):
    ms = jnp.mean(x * x, axis=-1, keepdims=True)
    return (x * lax.rsqrt(ms + EPS)) * g


def _silu(x):
    return x * jax.nn.sigmoid(x)


def _group_norm(o, gn_w, gn_b):
    mu = jnp.mean(o, axis=-1, keepdims=True)
    cen = o - mu
    var = jnp.mean(cen * cen, axis=-1, keepdims=True)
    return (cen * lax.rsqrt(var + EPS)) * gn_w + gn_b


def _rotary(x, cos, sin):
    lane = lax.broadcasted_iota(jnp.int32, x.shape, 1)
    partner = jnp.where((lane & 1) == 0, pltpu.roll(x, x.shape[1] - 1, 1), pltpu.roll(x, 1, 1))
    return x * cos + partner * sin


def _head_cols(h, width):
    return slice(h * width, (h + 1) * width)


def _group_cols(g):
    return slice(g * POOL_GROUP_DIM, (g + 1) * POOL_GROUP_DIM)


def _const_spec(shape):
    nd = len(shape)
    return pl.BlockSpec(shape, lambda *_: (0,) * nd, pipeline_mode=pl.Buffered(1))


def _params(n_grid):
    return pltpu.CompilerParams(
        dimension_semantics=("arbitrary",) * n_grid, vmem_limit_bytes=VMEM_LIMIT)


def _band_matrices():
    t = np.arange(SUB)[:, None]
    s = np.arange(SUB)[None, :]
    sh = np.arange(N_META)[None, :] - N_META
    cur = np.stack([((t - s >= 0) & (t - s < w)) for w in POOL_WINDOWS]).astype(np.float32)
    halo = np.stack([(t - sh < w) for w in POOL_WINDOWS]).astype(np.float32)
    tm = np.arange(N_META)[:, None]
    sm = np.arange(N_META)[None, :]
    meta = np.stack([((tm - sm >= 0) & (tm - sm < w)) for w in POOL_WINDOWS]).astype(np.float32)
    inv_meta = np.concatenate(
        [np.repeat(1.0 / np.minimum(w, tm + 1.0), POOL_GROUP_DIM, axis=1) for w in POOL_WINDOWS], axis=1)
    inv_w = np.concatenate([np.full((1, POOL_GROUP_DIM), 1.0 / w) for w in POOL_WINDOWS], axis=1)
    return (jnp.asarray(cur, BF), jnp.asarray(halo, BF), jnp.asarray(meta, BF),
            jnp.asarray(inv_meta, F32), jnp.asarray(inv_w, F32))


def _pool_mix(pooled_sum, inv_cnt, xn, wp_ref, scale):
    diff = (pooled_sum * inv_cnt - xn).astype(BF)
    mixed = jnp.concatenate(
        [_dot(diff[:, _group_cols(g)], wp_ref[g]) for g in range(POOL_GROUPS)], axis=1)
    return mixed * scale


def _pool_prompt_tile(x_ref, halo_ref, nw_ref, wp_ref, sc_ref, bc_ref, bh_ref, iw_ref, tail_ref):
    nw = nw_ref[...]
    scale = sc_ref[...]
    inv_w = iw_ref[...]
    halo = halo_ref[...]
    out = []
    for j in range(x_ref.shape[0] // SUB):
        x = x_ref[SUB * j:SUB * (j + 1), :]
        xn = _rms(x, nw)
        xb = xn.astype(BF)
        pooled = jnp.concatenate(
            [_dot(bc_ref[g], xb[:, _group_cols(g)]) + _dot(bh_ref[g], halo[:, _group_cols(g)])
             for g in range(POOL_GROUPS)], axis=1)
        out.append(x + _pool_mix(pooled, inv_w, xn, wp_ref, scale))
        halo = xb[SUB - N_META:, :]
    halo_ref[...] = halo
    tail_ref[0] = xn[SUB - N_META:, :]
    return jnp.concatenate(out, axis=0)


def _pool_sample_kernel(x_ref, prev_ref, meta_ref, nw_ref, wp_ref, sc_ref, iw_ref, bm_ref, icm_ref,
                        h_ref, np_ref, hm_ref):
    @pl.when(pl.program_id(0) == 0)
    def _():
        xm = meta_ref[...]
        xnm = _rms(xm, nw_ref[...])
        xbm = xnm.astype(BF)
        pooled_m = jnp.concatenate(
            [_dot(bm_ref[g], xbm[:, _group_cols(g)]) for g in range(POOL_GROUPS)], axis=1)
        hm_ref[...] = xm + _pool_mix(pooled_m, icm_ref[...], xnm, wp_ref, sc_ref[...])

    x = x_ref[...]
    xn = _rms(x, nw_ref[...])
    sums = []
    for g, w in enumerate(POOL_WINDOWS):
        s = xn[:, _group_cols(g)]
        for j in range(1, w):
            s = s + prev_ref[POOL_BUF - j, :, _group_cols(g)]
        sums.append(s)
    pooled = jnp.concatenate(sums, axis=1)
    h_ref[...] = x + _pool_mix(pooled, iw_ref[...], xn, wp_ref, sc_ref[...])
    for r in range(POOL_BUF - 1):
        np_ref[r] = prev_ref[r + 1]
    np_ref[POOL_BUF - 1] = xn


def _pool_sample(x, prev, meta, nw, wp, scale, tb=32):
    b = x.shape[0]
    _, _, band_meta, inv_meta, inv_w = _band_matrices()
    consts = (meta, nw, wp, scale, inv_w, band_meta, inv_meta)
    st_spec = pl.BlockSpec((None, POOL_BUF, tb, D_MODEL), lambda i: (0, 0, i, 0))
    return pl.pallas_call(
        _pool_sample_kernel,
        grid=(b // tb,),
        in_specs=[pl.BlockSpec((tb, D_MODEL), lambda i: (i, 0)), st_spec]
                 + [_const_spec(c.shape) for c in consts],
        out_specs=[pl.BlockSpec((tb, D_MODEL), lambda i: (i, 0)), st_spec,
                   pl.BlockSpec(meta.shape, lambda i: (0, 0))],
        out_shape=[jax.ShapeDtypeStruct((b, D_MODEL), F32),
                   jax.ShapeDtypeStruct(prev.shape, F32),
                   jax.ShapeDtypeStruct(meta.shape, F32)],
        compiler_params=_params(1),
        name="pool_sample",
    )(x, prev, *consts)


def _swiglu_stream_step(hn, wg_ref, wu_ref, wd_ref, wg_bf, wu_bf, wd_bf):
    wg = wg_ref[...].astype(BF)
    wu = wu_ref[...].astype(BF)
    wd = wd_ref[...].astype(BF)
    wg_bf[...] = wg
    wu_bf[...] = wu
    wd_bf[...] = wd
    act = (_silu(_dot(hn, wg)) * _dot(hn, wu)).astype(BF)
    return _dot(act, wd)


def _small_ffn_kernel(h_ref, nw_ref, wg_ref, wu_ref, wd_ref, o_ref, wg_bf, wu_bf, wd_bf, hn):
    @pl.when(pl.program_id(0) == 0)
    def _():
        h = h_ref[...]
        hn[...] = _rms(h, nw_ref[...]).astype(BF)
        o_ref[...] = h

    o_ref[...] += _swiglu_stream_step(hn[...], wg_ref, wu_ref, wd_ref, wg_bf, wu_bf, wd_bf)


def _ffn_weight_specs(layer, first):
    chunk = lambda c: jnp.maximum(c - first, 0)
    in_specs = [pl.BlockSpec((None, D_MODEL, FF_STREAM), lambda c: (layer, 0, chunk(c))),
                pl.BlockSpec((None, D_MODEL, FF_STREAM), lambda c: (layer, 0, chunk(c))),
                pl.BlockSpec((None, FF_STREAM, D_MODEL), lambda c: (layer, chunk(c), 0))]
    out_specs = [pl.BlockSpec((D_MODEL, FF_STREAM), lambda c: (0, chunk(c))),
                 pl.BlockSpec((D_MODEL, FF_STREAM), lambda c: (0, chunk(c))),
                 pl.BlockSpec((FF_STREAM, D_MODEL), lambda c: (chunk(c), 0))]
    out_shape = [jax.ShapeDtypeStruct((D_MODEL, D_FF), BF), jax.ShapeDtypeStruct((D_MODEL, D_FF), BF),
                 jax.ShapeDtypeStruct((D_FF, D_MODEL), BF)]
    return in_specs, out_specs, out_shape


def _small_ffn(h, nw, ffn_w32, layer):
    rows = h.shape[0]
    w_in_specs, w_out_specs, w_out_shape = _ffn_weight_specs(layer, 0)
    whole = pl.BlockSpec((rows, D_MODEL), lambda c: (0, 0))
    out = pl.pallas_call(
        _small_ffn_kernel,
        grid=(D_FF // FF_STREAM,),
        in_specs=[whole, _const_spec(nw.shape)] + w_in_specs,
        out_specs=[whole] + w_out_specs,
        out_shape=[jax.ShapeDtypeStruct((rows, D_MODEL), F32)] + w_out_shape,
        scratch_shapes=[pltpu.VMEM((rows, D_MODEL), BF)],
        compiler_params=_params(1),
        name="small_ffn",
    )(h, nw, *ffn_w32)
    return out[0], tuple(out[1:])


def _small_proj_kernel(h_ref, nw_ref, w_ref, cos_ref, sin_ref, kdec_ref, p_ref, kd_ref, w_bf, hn):
    c = pl.program_id(0)
    q_chunks = D_MODEL // PROJ_STREAM

    @pl.when(c == 0)
    def _():
        hn[...] = _rms(h_ref[...], nw_ref[...]).astype(BF)

    w = w_ref[...].astype(BF)
    w_bf[...] = w
    y = _dot(hn[...], w)

    def rotated():
        reps = PROJ_STREAM // RET_KDIM
        return _rotary(y, jnp.concatenate([cos_ref[...]] * reps, axis=1),
                       jnp.concatenate([sin_ref[...]] * reps, axis=1))

    @pl.when(c < q_chunks)
    def _():
        p_ref[...] = (rotated() * (RET_KDIM ** -0.5)).astype(BF)

    @pl.when((c >= q_chunks) & (c < 2 * q_chunks))
    def _():
        k = rotated()
        p_ref[...] = k.astype(BF)
        kd_ref[...] = (k * kdec_ref[...]).astype(BF)

    @pl.when(c >= 2 * q_chunks)
    def _():
        p_ref[...] = y.astype(BF)


def _small_proj(h, nw, w_in32, cos, sin, kdec):
    rows = h.shape[0]
    q_chunks = D_MODEL // PROJ_STREAM
    k_chunk = lambda c: jnp.clip(c - q_chunks, 0, q_chunks - 1)
    whole = pl.BlockSpec((rows, D_MODEL), lambda c: (0, 0))
    return pl.pallas_call(
        _small_proj_kernel,
        grid=(6 * D_MODEL // PROJ_STREAM,),
        in_specs=[whole, _const_spec(nw.shape),
                  pl.BlockSpec((None, D_MODEL, PROJ_STREAM), lambda c: (0, 0, c)),
                  _const_spec(cos.shape), _const_spec(sin.shape),
                  pl.BlockSpec((rows, PROJ_STREAM), lambda c: (0, k_chunk(c)))],
        out_specs=[pl.BlockSpec((rows, PROJ_STREAM), lambda c: (0, c)),
                   pl.BlockSpec((rows, PROJ_STREAM), lambda c: (0, k_chunk(c))),
                   pl.BlockSpec((D_MODEL, PROJ_STREAM), lambda c: (0, c))],
        out_shape=[jax.ShapeDtypeStruct((rows, 6 * D_MODEL), BF),
                   jax.ShapeDtypeStruct((rows, D_MODEL), BF),
                   jax.ShapeDtypeStruct((D_MODEL, 6 * D_MODEL), BF)],
        scratch_shapes=[pltpu.VMEM((rows, D_MODEL), BF)],
        compiler_params=_params(1),
        name="small_proj",
    )(h, nw, w_in32, cos, sin, kdec)


def _small_out_ffn_kernel(h_ref, on_ref, g_ref, wo_ref, nw_ref, wg_ref, wu_ref, wd_ref, nf_ref,
                          o_ref, wo_bf, wg_bf, wu_bf, wd_bf, acc, hn):
    c = pl.program_id(0)
    out_steps = 2 * D_MODEL // PROJ_STREAM

    @pl.when(c == 0)
    def _():
        acc[...] = h_ref[...]

    @pl.when(c < out_steps)
    def _():
        wo = wo_ref[...].astype(BF)
        wo_bf[...] = wo
        gated = (_silu(g_ref[...].astype(F32)) * on_ref[...]).astype(BF)
        acc[...] += _dot(gated, wo)

    @pl.when(c == out_steps)
    def _():
        hn[...] = _rms(acc[...], nw_ref[...]).astype(BF)

    @pl.when(c >= out_steps)
    def _():
        acc[...] += _swiglu_stream_step(hn[...], wg_ref, wu_ref, wd_ref, wg_bf, wu_bf, wd_bf)

    @pl.when(c == pl.num_programs(0) - 1)
    def _():
        o_ref[...] = _rms(acc[...], nf_ref[...])


def _small_out_ffn(h, on, proj, w_out32, nw, ffn_w32, layer, nf, rows):
    out_steps = 2 * D_MODEL // PROJ_STREAM
    gate_col0 = 4 * D_MODEL // PROJ_STREAM
    w_in_specs, w_out_specs, w_out_shape = _ffn_weight_specs(layer, out_steps)
    k_chunk = lambda c: jnp.minimum(c, out_steps - 1)
    whole = pl.BlockSpec((rows, D_MODEL), lambda c: (0, 0))
    out = pl.pallas_call(
        _small_out_ffn_kernel,
        grid=(out_steps + D_FF // FF_STREAM,),
        in_specs=[whole,
                  pl.BlockSpec((rows, PROJ_STREAM), lambda c: (0, k_chunk(c))),
                  pl.BlockSpec((rows, PROJ_STREAM), lambda c: (0, gate_col0 + k_chunk(c))),
                  pl.BlockSpec((None, PROJ_STREAM, D_MODEL), lambda c: (0, k_chunk(c), 0)),
                  _const_spec(nw.shape)] + w_in_specs + [_const_spec(nf.shape)],
        out_specs=[whole, pl.BlockSpec((PROJ_STREAM, D_MODEL), lambda c: (k_chunk(c), 0))] + w_out_specs,
        out_shape=[jax.ShapeDtypeStruct((rows, D_MODEL), F32),
                   jax.ShapeDtypeStruct((2 * D_MODEL, D_MODEL), BF)] + w_out_shape,
        scratch_shapes=[pltpu.VMEM((rows, D_MODEL), F32), pltpu.VMEM((rows, D_MODEL), BF)],
        compiler_params=_params(1),
        name="small_out_ffn",
    )(h, on, proj, w_out32, nw, *ffn_w32, nf)
    return out[0], out[1], tuple(out[2:])


STREAM_TB = 4


def _head_state_update(s, h, lg_ref, qt, kt, v_ref, gw_ref, gb_ref, s_in, s_out, on_ref):
    gamma = jnp.exp(jnp.full((1, 1), 1.0, F32) * lg_ref[h])
    kc = _head_cols(h, RET_KDIM)
    vc = _head_cols(h, RET_VDIM)
    q = qt[kc, s:s + 1]
    k = kt[kc, s:s + 1]
    v = v_ref[s:s + 1, vc]
    s_prev = s_in[s, h]
    qs = jnp.sum(q * s_prev, axis=0, keepdims=True)
    score = jnp.sum(q * k, axis=0, keepdims=True)
    o = score * v + qs * gamma
    s_out[s, h] = gamma * s_prev + k * v
    on_ref[s:s + 1, vc] = _group_norm(o, gw_ref[:, vc], gb_ref[:, vc])


def _layer0_stream_kernel(tiles_per_seq, lg_ref, x_ref, meta_ref, nm_ref, wp_ref, sc_ref, bc_ref, bh_ref,
                          iw_ref, nw_ref, wg_ref, wu_ref, wd_ref, qt_ref, kt_ref, v_ref, gw_ref, gb_ref,
                          s_hbm, o_ref, tail_ref, on_ref, so_hbm, halo, s_in, s_out, in_sem, out_sem):
    i = pl.program_id(0)
    n = pl.num_programs(0)

    @pl.when(i % tiles_per_seq == 0)
    def _():
        halo[...] = _rms(meta_ref[...], nm_ref[...]).astype(BF)

    def in_copy(step, s):
        return pltpu.make_async_copy(s_hbm.at[0, step * STREAM_TB + s], s_in.at[s], in_sem.at[s])

    def out_copy(step, s):
        return pltpu.make_async_copy(s_out.at[s], so_hbm.at[0, step * STREAM_TB + s], out_sem.at[s])

    @pl.when(i == 0)
    def _():
        for s in range(STREAM_TB):
            in_copy(0, s).start()

    lanes = qt_ref.shape[1]
    shift = (lanes - i * STREAM_TB) % lanes
    qt = pltpu.roll(qt_ref[...], shift, 1)
    kt = pltpu.roll(kt_ref[...], shift, 1)

    h = _pool_prompt_tile(x_ref, halo, nm_ref, wp_ref, sc_ref, bc_ref, bh_ref, iw_ref, tail_ref)
    hn = _rms(h, nw_ref[...]).astype(BF)

    ffn = {"acc": h, "act": None}

    def up_unit(cols):
        c0, c1 = cols
        gt = _dot(hn, wg_ref[:, c0:c1])
        up = _dot(hn, wu_ref[:, c0:c1])
        ffn["act"] = (_silu(gt) * up).astype(BF)

    def down_unit(cols):
        c0, c1 = cols
        ffn["acc"] = ffn["acc"] + _dot(ffn["act"], wd_ref[c0:c1, :])

    units = [functools.partial(u, cols) for cols in FF_CHUNKS for u in (up_unit, down_unit)]
    per_sample = np.array_split(np.arange(len(units)), STREAM_TB)
    for s in range(STREAM_TB):
        in_copy(i, s).wait()

        @pl.when(i > 0)
        def _():
            out_copy(i - 1, s).wait()

        mine = list(per_sample[s])
        for hd in range(RET_HEADS):
            _head_state_update(s, hd, lg_ref, qt, kt, v_ref, gw_ref, gb_ref, s_in, s_out, on_ref)
            if hd == RET_HEADS - 1:
                out_copy(i, s).start()

                @pl.when(i + 1 < n)
                def _():
                    in_copy(i + 1, s).start()
            take = len(mine) - (len(mine) * (RET_HEADS - 1 - hd)) // RET_HEADS
            for u in mine[:take]:
                units[u]()
            mine = mine[take:]
    o_ref[...] = ffn["acc"]

    @pl.when(i == n - 1)
    def _():
        for s in range(STREAM_TB):
            out_copy(i, s).wait()


def _layer0_stream(x, meta, nm, wp, scale, nw, ffn_w, tm, log_g, q_s, k_s, v_s, gn_w, gn_b, s0):
    b, t, _ = x.shape
    rows = b * t
    n = rows // tm
    bs = q_s.shape[0]
    assert bs == n * STREAM_TB and t % tm == 0
    tiles_per_seq = t // tm
    band_cur, band_halo, _, _, inv_w = _band_matrices()
    qt = q_s.astype(F32).T
    kt = k_s.astype(F32).T
    v3 = v_s.astype(F32).reshape(n, STREAM_TB, -1)
    row_spec = pl.BlockSpec((tm, D_MODEL), lambda i, lg: (i, 0))
    tail_spec = pl.BlockSpec((1, N_META, D_MODEL), lambda i, lg: (i // tiles_per_seq, 0, 0))
    v_spec = pl.BlockSpec((None, STREAM_TB, 2 * D_MODEL), lambda i, lg: (i, 0, 0))
    hbm = pl.BlockSpec(memory_space=pl.ANY)
    blk = (STREAM_TB, RET_HEADS, RET_KDIM, RET_VDIM)
    consts = (meta, nm, wp, scale, band_cur, band_halo, inv_w, nw) + tuple(ffn_w) + (qt, kt)
    h_out, tail, on_s, s1 = pl.pallas_call(
        functools.partial(_layer0_stream_kernel, tiles_per_seq),
        grid_spec=pltpu.PrefetchScalarGridSpec(
            num_scalar_prefetch=1,
            grid=(n,),
            in_specs=[row_spec] + [_const_spec(c.shape) for c in consts]
                     + [v_spec, _const_spec(gn_w.shape), _const_spec(gn_b.shape), hbm],
            out_specs=[row_spec, tail_spec, v_spec, hbm],
            scratch_shapes=[pltpu.VMEM((N_META, D_MODEL), BF),
                            pltpu.VMEM(blk, F32), pltpu.VMEM(blk, F32),
                            pltpu.SemaphoreType.DMA((STREAM_TB,)), pltpu.SemaphoreType.DMA((STREAM_TB,))]),
        out_shape=[jax.ShapeDtypeStruct((rows, D_MODEL), F32),
                   jax.ShapeDtypeStruct((b, N_META, D_MODEL), F32),
                   jax.ShapeDtypeStruct(v3.shape, F32),
                   jax.ShapeDtypeStruct(s0.shape, F32)],
        compiler_params=_params(1),
        name="layer0_stream",
    )(log_g, x.reshape(rows, D_MODEL), *consts, v3, gn_w, gn_b, s0)
    return h_out, tail, on_s.reshape(bs, -1), s1


def _proj_kernel(h_ref, nw_ref, w_ref, cos_ref, sin_ref, cross_ref, kdec_ref,
                 q_ref, qx_ref, k_ref, kd_ref, v_ref, g_ref):
    hn = _rms(h_ref[...], nw_ref[...]).astype(BF)
    cos = jnp.concatenate([cos_ref[...]] * RET_HEADS, axis=1)
    sin = jnp.concatenate([sin_ref[...]] * RET_HEADS, axis=1)

    q = _rotary(_dot(hn, w_ref[:, 0:D_MODEL]), cos, sin) * (RET_KDIM ** -0.5)
    q_ref[...] = q.astype(BF)
    qx_ref[...] = (q * cross_ref[...]).astype(BF)
    k = _rotary(_dot(hn, w_ref[:, D_MODEL:2 * D_MODEL]), cos, sin)
    k_ref[...] = k.astype(BF)
    kd_ref[...] = (k * kdec_ref[...]).astype(BF)
    v_ref[...] = _dot(hn, w_ref[:, 2 * D_MODEL:4 * D_MODEL]).astype(BF)
    g_ref[...] = _dot(hn, w_ref[:, 4 * D_MODEL:6 * D_MODEL]).astype(BF)


def _proj(h, nw, w_in, cos, sin, cross, kdec, tm):
    rows = h.shape[0]
    ntab = cos.shape[0] // tm
    row_spec = lambda width: pl.BlockSpec((tm, width), lambda i: (i, 0))
    tab_spec = pl.BlockSpec((tm, RET_KDIM), lambda i: (i % ntab, 0))
    widths = (D_MODEL, D_MODEL, D_MODEL, D_MODEL, 2 * D_MODEL, 2 * D_MODEL)
    return pl.pallas_call(
        _proj_kernel,
        grid=(rows // tm,),
        in_specs=[row_spec(D_MODEL), _const_spec(nw.shape), _const_spec(w_in.shape),
                  tab_spec, tab_spec, _const_spec(cross.shape), _const_spec(kdec.shape)],
        out_specs=[row_spec(w) for w in widths],
        out_shape=[jax.ShapeDtypeStruct((rows, w), BF) for w in widths],
        compiler_params=_params(1),
        name="ret_proj",
    )(h, nw, w_in, cos, sin, cross, kdec)


def _ret_prompt_kernel(tot_ref, kdm_ref, vm_ref, dm_ref, q_ref, qx_ref, k_ref, kd_ref, v_ref,
                       gw_ref, gb_ref, o_ref, s_ref, state):
    t = pl.program_id(1)

    @pl.when(t == 0)
    def _():
        for h in range(RET_HEADS):
            state[h] = _kt_v(kdm_ref[:, _head_cols(h, RET_KDIM)], vm_ref[:, _head_cols(h, RET_VDIM)])

    for c in range(q_ref.shape[1] // SUB):
        rows = slice(c * SUB, (c + 1) * SUB)
        for h in range(RET_HEADS):
            kc = _head_cols(h, RET_KDIM)
            vc = _head_cols(h, RET_VDIM)
            q = q_ref[0, rows, kc]
            v = v_ref[0, rows, vc]
            s_prev = state[h]
            scores = lax.dot_general(q, k_ref[0, rows, kc], (((1,), (1,)), ((), ())),
                                     preferred_element_type=F32) * dm_ref[h]
            o = _dot(scores.astype(BF), v) + _dot(qx_ref[0, rows, kc], s_prev.astype(BF))
            state[h] = tot_ref[h] * s_prev + _kt_v(kd_ref[0, rows, kc], v)
            o_ref[0, rows, vc] = _group_norm(o, gw_ref[:, vc], gb_ref[:, vc]).astype(BF)

    @pl.when(t == pl.num_programs(1) - 1)
    def _():
        s_ref[0] = state[...]


def _ret_prompt(chunk_decay, kd_meta, v_meta, dmask, q, qx, k, kd, v, gn_w, gn_b, tt=2 * SUB):
    b, t, _ = q.shape
    qk_spec = pl.BlockSpec((1, tt, D_MODEL), lambda i, j, s: (i, j, 0))
    v_spec = pl.BlockSpec((1, tt, 2 * D_MODEL), lambda i, j, s: (i, j, 0))
    st_shape = (RET_HEADS, RET_KDIM, RET_VDIM)
    return pl.pallas_call(
        _ret_prompt_kernel,
        grid_spec=pltpu.PrefetchScalarGridSpec(
            num_scalar_prefetch=1,
            grid=(b, t // tt),
            in_specs=[_const_spec(kd_meta.shape), _const_spec(v_meta.shape), _const_spec(dmask.shape),
                      qk_spec, qk_spec, qk_spec, qk_spec, v_spec,
                      _const_spec(gn_w.shape), _const_spec(gn_b.shape)],
            out_specs=[v_spec,
                       pl.BlockSpec((None, 1) + st_shape, lambda i, j, s: (0, i, 0, 0, 0))],
            scratch_shapes=[pltpu.VMEM(st_shape, F32)]),
        out_shape=[jax.ShapeDtypeStruct((b, t, 2 * D_MODEL), BF),
                   jax.ShapeDtypeStruct((1, b) + st_shape, F32)],
        compiler_params=_params(2),
        name="ret_prompt",
    )(chunk_decay, kd_meta, v_meta, dmask, q, qx, k, kd, v, gn_w, gn_b)


def _out_ffn_kernel(h_ref, on_ref, g_ref, wo_ref, nw_ref, wg_ref, wu_ref, wd_ref, nf_ref, o_ref):
    gated = (_silu(g_ref[...].astype(F32)) * on_ref[...].astype(F32)).astype(BF)
    h = h_ref[...] + _dot(gated, wo_ref[...])
    hn = _rms(h, nw_ref[...]).astype(BF)
    for c0, c1 in FF_CHUNKS:
        act = (_silu(_dot(hn, wg_ref[:, c0:c1])) * _dot(hn, wu_ref[:, c0:c1])).astype(BF)
        h = h + _dot(act, wd_ref[c0:c1, :])
    o_ref[...] = _rms(h, nf_ref[...])


def _out_ffn(h, on, g, wo, nw, ffn_w, nf, tm):
    rows = h.shape[0]
    row_spec = lambda width: pl.BlockSpec((tm, width), lambda i: (i, 0))
    consts = (wo, nw) + tuple(ffn_w) + (nf,)
    return pl.pallas_call(
        _out_ffn_kernel,
        grid=(rows // tm,),
        in_specs=[row_spec(D_MODEL), row_spec(2 * D_MODEL), row_spec(2 * D_MODEL)]
                 + [_const_spec(c.shape) for c in consts],
        out_specs=row_spec(D_MODEL),
        out_shape=jax.ShapeDtypeStruct((rows, D_MODEL), F32),
        compiler_params=_params(1),
        name="out_ffn",
    )(h, on, g, *consts)


def _rotary_tables(pos):
    theta = 1.0 / (ROPE_BASE ** jnp.linspace(0.0, 1.0, RET_KDIM // 2, dtype=F32))
    ang = pos.astype(F32)[:, None] * theta[None, :]
    cos = jnp.repeat(jnp.cos(ang), 2, axis=1)
    sin = jnp.stack([-jnp.sin(ang), jnp.sin(ang)], axis=-1).reshape(ang.shape[0], RET_KDIM)
    return cos, sin


def _per_head_cols(table):
    return jnp.asarray(np.repeat(table, RET_KDIM, axis=1), F32)


def _decay_to_end(c):
    return np.exp((c - 1.0 - np.arange(c))[:, None] * LOG_GAMMA[None, :])


def _decay_from_start(c):
    return np.exp((np.arange(c) + 1.0)[:, None] * LOG_GAMMA[None, :])


def _decay_mask(c):
    rel = np.arange(c)[:, None] - np.arange(c)[None, :]
    return np.where(rel[None] >= 0, np.exp(np.maximum(rel, 0)[None] * LOG_GAMMA[:, None, None]), 0.0)


def kernel(x_prompt, x_sample, state_pool, state_ret, meta_tokens, norm_mix, norm_ffn, norm_final,
           w_pool, pool_scale, w_ret_in, ret_gn_w, ret_gn_b, w_ret_out, w_ffn_gate, w_ffn_up, w_ffn_down):
    b, t, _ = x_prompt.shape
    bs = x_sample.shape[0]
    log_g = jnp.asarray(LOG_GAMMA, F32)

    wp = w_pool[0].astype(BF)
    ffn_w32 = (w_ffn_gate, w_ffn_up, w_ffn_down)
    nm0, nm1 = norm_mix[0:1], norm_mix[1:2]
    nf0, nf1 = norm_ffn[0:1], norm_ffn[1:2]
    nfin = norm_final[None, :]
    scale = pool_scale[0:1]
    gn_w, gn_b = ret_gn_w[0:1], ret_gn_b[0:1]

    hs, new_pool_sample, h_meta = _pool_sample(
        x_sample.reshape(bs, D_MODEL), state_pool.transpose(0, 2, 1, 3), meta_tokens, nm0, wp, scale)
    new_pool_sample = new_pool_sample.transpose(0, 2, 1, 3)
    h_small, ffn_w0 = _small_ffn(jnp.concatenate([hs, h_meta], axis=0), nf0, ffn_w32, 0)

    pos_small = jnp.concatenate([jnp.full((bs,), PAST_LEN, jnp.int32), jnp.arange(N_META, dtype=jnp.int32)])
    cos_s, sin_s = _rotary_tables(pos_small)
    kdec_s = _per_head_cols(np.concatenate([np.ones((bs, RET_HEADS)), _decay_to_end(N_META)], axis=0))
    proj_s, kd_s, w_in = _small_proj(h_small, nm1, w_ret_in, cos_s, sin_s, kdec_s)
    q_s = proj_s[:bs, 0:D_MODEL]
    k_s = proj_s[:bs, D_MODEL:2 * D_MODEL]
    v_s = proj_s[:bs, 2 * D_MODEL:4 * D_MODEL]

    h, tail, on_s, new_ret_sample = _layer0_stream(
        x_prompt, meta_tokens, nm0, wp, scale, nf0, ffn_w0, TM, log_g, q_s, k_s, v_s, gn_w, gn_b, state_ret)

    y_sample, w_out, ffn_w1 = _small_out_ffn(h_small, on_s, proj_s, w_ret_out, nf1, ffn_w32, 1, nfin, bs)

    cos, sin = _rotary_tables(N_META + jnp.arange(t, dtype=jnp.int32))
    reps = (TM // SUB, 1)
    q, qx, k, kd, v, g = _proj(h, nm1, w_in, cos, sin,
                               _per_head_cols(np.tile(_decay_from_start(SUB), reps)),
                               _per_head_cols(np.tile(_decay_to_end(SUB), reps)), TM)
    seq = lambda a: a.reshape(b, t, -1)
    on, new_ret_prompt = _ret_prompt(jnp.asarray(np.exp(SUB * LOG_GAMMA), F32), kd_s[bs:],
                                     proj_s[bs:, 2 * D_MODEL:4 * D_MODEL],
                                     jnp.asarray(_decay_mask(SUB), F32),
                                     seq(q), seq(qx), seq(k), seq(kd), seq(v), gn_w, gn_b)
    y = _out_ffn(h, on.reshape(b * t, -1), g, w_out, nf1, ffn_w1, nfin, TM)

    return (y.reshape(b, t, D_MODEL), y_sample.reshape(bs, 1, D_MODEL), tail[:, 1:][None],
            new_pool_sample, new_ret_prompt, new_ret_sample)
```

```python
import functools

import numpy as np
import jax
import jax.numpy as jnp
from jax import lax
from jax.experimental import pallas as pl
from jax.experimental.pallas import tpu as pltpu

D_MODEL = 1024
N_META = 16
PAST_LEN = 16384
POOL_WINDOWS = (2, 4, 8, 16)
POOL_GROUPS = len(POOL_WINDOWS)
POOL_GROUP_DIM = D_MODEL // POOL_GROUPS
POOL_BUF = max(POOL_WINDOWS) - 1
RET_HEADS = 4
RET_KDIM = D_MODEL // RET_HEADS
RET_VDIM = 2 * D_MODEL // RET_HEADS
ROPE_BASE = 10000.0
D_FF = 2816
EPS = 1e-6

BF = jnp.bfloat16
F32 = jnp.float32

SUB = 256
TM = 512
FF_CHUNKS = tuple((c, min(c + 512, D_FF)) for c in range(0, D_FF, 512))
FF_STREAM = 256
PROJ_STREAM = 1024
assert D_FF % FF_STREAM == 0 and D_MODEL % PROJ_STREAM == 0
V7X_VMEM_BYTES = 64 * 1024 * 1024
VMEM_LIMIT = V7X_VMEM_BYTES * 7 // 8

LOG_GAMMA = np.log(1.0 - 2.0 ** (-5.0 - np.arange(RET_HEADS)))


def _dot(a, b):
    return jnp.dot(a, b, preferred_element_type=F32)


def _kt_v(k, v):
    return lax.dot_general(k, v, (((0,), (0,)), ((), ())), preferred_element_type=F32)


def _rms(x, g):
    ms = jnp.mean(x * x, axis=-1, keepdims=True)
    return (x * lax.rsqrt(ms + EPS)) * g


def _silu(x):
    return x * jax.nn.sigmoid(x)


def _group_norm(o, gn_w, gn_b):
    mu = jnp.mean(o, axis=-1, keepdims=True)
    cen = o - mu
    var = jnp.mean(cen * cen, axis=-1, keepdims=True)
    return (cen * lax.rsqrt(var + EPS)) * gn_w + gn_b


def _rotary(x, cos, sin):
    lane = lax.broadcasted_iota(jnp.int32, x.shape, 1)
    partner = jnp.where((lane & 1) == 0, pltpu.roll(x, x.shape[1] - 1, 1), pltpu.roll(x, 1, 1))
    return x * cos + partner * sin


def _head_cols(h, width):
    return slice(h * width, (h + 1) * width)


def _group_cols(g):
    return slice(g * POOL_GROUP_DIM, (g + 1) * POOL_GROUP_DIM)


def _const_spec(shape):
    nd = len(shape)
    return pl.BlockSpec(shape, lambda *_: (0,) * nd, pipeline_mode=pl.Buffered(1))


def _params(n_grid):
    return pltpu.CompilerParams(
        dimension_semantics=("arbitrary",) * n_grid, vmem_limit_bytes=VMEM_LIMIT)


def _band_matrices():
    t = np.arange(SUB)[:, None]
    s = np.arange(SUB)[None, :]
    sh = np.arange(N_META)[None, :] - N_META
    cur = np.stack([((t - s >= 0) & (t - s < w)) for w in POOL_WINDOWS]).astype(np.float32)
    halo = np.stack([(t - sh < w) for w in POOL_WINDOWS]).astype(np.float32)
    tm = np.arange(N_META)[:, None]
    sm = np.arange(N_META)[None, :]
    meta = np.stack([((tm - sm >= 0) & (tm - sm < w)) for w in POOL_WINDOWS]).astype(np.float32)
    inv_meta = np.concatenate(
        [np.repeat(1.0 / np.minimum(w, tm + 1.0), POOL_GROUP_DIM, axis=1) for w in POOL_WINDOWS], axis=1)
    inv_w = np.concatenate([np.full((1, POOL_GROUP_DIM), 1.0 / w) for w in POOL_WINDOWS], axis=1)
    return (jnp.asarray(cur, BF), jnp.asarray(halo, BF), jnp.asarray(meta, BF),
            jnp.asarray(inv_meta, F32), jnp.asarray(inv_w, F32))


def _pool_mix(pooled_sum, inv_cnt, xn, wp_ref, scale):
    diff = (pooled_sum * inv_cnt - xn).astype(BF)
    mixed = jnp.concatenate(
        [_dot(diff[:, _group_cols(g)], wp_ref[g]) for g in range(POOL_GROUPS)], axis=1)
    return mixed * scale


def _pool_prompt_tile(x_ref, halo_ref, nw_ref, wp_ref, sc_ref, bc_ref, bh_ref, iw_ref, tail_ref):
    nw = nw_ref[...]
    scale = sc_ref[...]
    inv_w = iw_ref[...]
    halo = halo_ref[...]
    out = []
    for j in range(x_ref.shape[0] // SUB):
        x = x_ref[SUB * j:SUB * (j + 1), :]
        xn = _rms(x, nw)
        xb = xn.astype(BF)
        pooled = jnp.concatenate(
            [_dot(bc_ref[g], xb[:, _group_cols(g)]) + _dot(bh_ref[g], halo[:, _group_cols(g)])
             for g in range(POOL_GROUPS)], axis=1)
        out.append(x + _pool_mix(pooled, inv_w, xn, wp_ref, scale))
        halo = xb[SUB - N_META:, :]
    halo_ref[...] = halo
    tail_ref[0] = xn[SUB - N_META:, :]
    return jnp.concatenate(out, axis=0)


def _pool_sample_kernel(x_ref, prev_ref, meta_ref, nw_ref, wp_ref, sc_ref, iw_ref, bm_ref, icm_ref,
                        h_ref, np_ref, hm_ref):
    @pl.when(pl.program_id(0) == 0)
    def _():
        xm = meta_ref[...]
        xnm = _rms(xm, nw_ref[...])
        xbm = xnm.astype(BF)
        pooled_m = jnp.concatenate(
            [_dot(bm_ref[g], xbm[:, _group_cols(g)]) for g in range(POOL_GROUPS)], axis=1)
        hm_ref[...] = xm + _pool_mix(pooled_m, icm_ref[...], xnm, wp_ref, sc_ref[...])

    x = x_ref[...]
    xn = _rms(x, nw_ref[...])
    sums = []
    for g, w in enumerate(POOL_WINDOWS):
        s = xn[:, _group_cols(g)]
        for j in range(1, w):
            s = s + prev_ref[POOL_BUF - j, :, _group_cols(g)]
        sums.append(s)
    pooled = jnp.concatenate(sums, axis=1)
    h_ref[...] = x + _pool_mix(pooled, iw_ref[...], xn, wp_ref, sc_ref[...])
    for r in range(POOL_BUF - 1):
        np_ref[r] = prev_ref[r + 1]
    np_ref[POOL_BUF - 1] = xn


def _pool_sample(x, prev, meta, nw, wp, scale, tb=32):
    b = x.shape[0]
    _, _, band_meta, inv_meta, inv_w = _band_matrices()
    consts = (meta, nw, wp, scale, inv_w, band_meta, inv_meta)
    st_spec = pl.BlockSpec((None, POOL_BUF, tb, D_MODEL), lambda i: (0, 0, i, 0))
    return pl.pallas_call(
        _pool_sample_kernel,
        grid=(b // tb,),
        in_specs=[pl.BlockSpec((tb, D_MODEL), lambda i: (i, 0)), st_spec]
                 + [_const_spec(c.shape) for c in consts],
        out_specs=[pl.BlockSpec((tb, D_MODEL), lambda i: (i, 0)), st_spec,
                   pl.BlockSpec(meta.shape, lambda i: (0, 0))],
        out_shape=[jax.ShapeDtypeStruct((b, D_MODEL), F32),
                   jax.ShapeDtypeStruct(prev.shape, F32),
                   jax.ShapeDtypeStruct(meta.shape, F32)],
        compiler_params=_params(1),
        name="pool_sample",
    )(x, prev, *consts)


def _small_ffn_kernel(h_ref, nw_ref, wg_ref, wu_ref, wd_ref, o_ref, wg_bf, wu_bf, wd_bf, hn):
    @pl.when(pl.program_id(0) == 0)
    def _():
        h = h_ref[...]
        hn[...] = _rms(h, nw_ref[...]).astype(BF)
        o_ref[...] = h

    wg = wg_ref[...].astype(BF)
    wu = wu_ref[...].astype(BF)
    wd = wd_ref[...].astype(BF)
    wg_bf[...] = wg
    wu_bf[...] = wu
    wd_bf[...] = wd
    act = (_silu(_dot(hn[...], wg)) * _dot(hn[...], wu)).astype(BF)
    o_ref[...] += _dot(act, wd)


def _small_ffn(h, nw, ffn_w32, layer):
    rows = h.shape[0]
    whole = pl.BlockSpec((rows, D_MODEL), lambda c: (0, 0))
    out = pl.pallas_call(
        _small_ffn_kernel,
        grid=(D_FF // FF_STREAM,),
        in_specs=[whole, _const_spec(nw.shape),
                  pl.BlockSpec((None, D_MODEL, FF_STREAM), lambda c: (layer, 0, c)),
                  pl.BlockSpec((None, D_MODEL, FF_STREAM), lambda c: (layer, 0, c)),
                  pl.BlockSpec((None, FF_STREAM, D_MODEL), lambda c: (layer, c, 0))],
        out_specs=[whole,
                   pl.BlockSpec((D_MODEL, FF_STREAM), lambda c: (0, c)),
                   pl.BlockSpec((D_MODEL, FF_STREAM), lambda c: (0, c)),
                   pl.BlockSpec((FF_STREAM, D_MODEL), lambda c: (c, 0))],
        out_shape=[jax.ShapeDtypeStruct((rows, D_MODEL), F32),
                   jax.ShapeDtypeStruct((D_MODEL, D_FF), BF), jax.ShapeDtypeStruct((D_MODEL, D_FF), BF),
                   jax.ShapeDtypeStruct((D_FF, D_MODEL), BF)],
        scratch_shapes=[pltpu.VMEM((rows, D_MODEL), BF)],
        compiler_params=_params(1),
        name="small_ffn",
    )(h, nw, *ffn_w32)
    return out[0], tuple(out[1:])


def _small_proj_kernel(h_ref, nw_ref, w_ref, cos_ref, sin_ref, kdec_ref, p_ref, kd_ref, w_bf, hn):
    c = pl.program_id(0)
    q_chunks = D_MODEL // PROJ_STREAM

    @pl.when(c == 0)
    def _():
        hn[...] = _rms(h_ref[...], nw_ref[...]).astype(BF)

    w = w_ref[...].astype(BF)
    w_bf[...] = w
    y = _dot(hn[...], w)

    def rotated():
        reps = PROJ_STREAM // RET_KDIM
        return _rotary(y, jnp.concatenate([cos_ref[...]] * reps, axis=1),
                       jnp.concatenate([sin_ref[...]] * reps, axis=1))

    @pl.when(c < q_chunks)
    def _():
        p_ref[...] = (rotated() * (RET_KDIM ** -0.5)).astype(BF)

    @pl.when((c >= q_chunks) & (c < 2 * q_chunks))
    def _():
        k = rotated()
        p_ref[...] = k.astype(BF)
        kd_ref[...] = (k * kdec_ref[...]).astype(BF)

    @pl.when(c >= 2 * q_chunks)
    def _():
        p_ref[...] = y.astype(BF)


def _small_proj(h, nw, w_in32, cos, sin, kdec):
    rows = h.shape[0]
    q_chunks = D_MODEL // PROJ_STREAM
    k_chunk = lambda c: jnp.clip(c - q_chunks, 0, q_chunks - 1)
    whole = pl.BlockSpec((rows, D_MODEL), lambda c: (0, 0))
    return pl.pallas_call(
        _small_proj_kernel,
        grid=(6 * D_MODEL // PROJ_STREAM,),
        in_specs=[whole, _const_spec(nw.shape),
                  pl.BlockSpec((None, D_MODEL, PROJ_STREAM), lambda c: (0, 0, c)),
                  _const_spec(cos.shape), _const_spec(sin.shape),
                  pl.BlockSpec((rows, PROJ_STREAM), lambda c: (0, k_chunk(c)))],
        out_specs=[pl.BlockSpec((rows, PROJ_STREAM), lambda c: (0, c)),
                   pl.BlockSpec((rows, PROJ_STREAM), lambda c: (0, k_chunk(c))),
                   pl.BlockSpec((D_MODEL, PROJ_STREAM), lambda c: (0, c))],
        out_shape=[jax.ShapeDtypeStruct((rows, 6 * D_MODEL), BF),
                   jax.ShapeDtypeStruct((rows, D_MODEL), BF),
                   jax.ShapeDtypeStruct((D_MODEL, 6 * D_MODEL), BF)],
        scratch_shapes=[pltpu.VMEM((rows, D_MODEL), BF)],
        compiler_params=_params(1),
        name="small_proj",
    )(h, nw, w_in32, cos, sin, kdec)


STREAM_TB = 4


def _head_state_update(s, h, lg_ref, qt, kt, v_ref, gw_ref, gb_ref, s_in, s_out, on_ref):
    gamma = jnp.exp(jnp.full((1, 1), 1.0, F32) * lg_ref[h])
    kc = _head_cols(h, RET_KDIM)
    vc = _head_cols(h, RET_VDIM)
    q = qt[kc, s:s + 1]
    k = kt[kc, s:s + 1]
    v = v_ref[s:s + 1, vc]
    s_prev = s_in[s, h]
    qs = jnp.sum(q * s_prev, axis=0, keepdims=True)
    score = jnp.sum(q * k, axis=0, keepdims=True)
    o = score * v + qs * gamma
    s_out[s, h] = gamma * s_prev + k * v
    on_ref[s:s + 1, vc] = _group_norm(o, gw_ref[:, vc], gb_ref[:, vc])


def _layer0_stream_kernel(tiles_per_seq, lg_ref, x_ref, meta_ref, nm_ref, wp_ref, sc_ref, bc_ref, bh_ref,
                          iw_ref, nw_ref, wg_ref, wu_ref, wd_ref, qt_ref, kt_ref, v_ref, gw_ref, gb_ref,
                          s_hbm, o_ref, tail_ref, on_ref, so_hbm, halo, s_in, s_out, in_sem, out_sem):
    i = pl.program_id(0)
    n = pl.num_programs(0)

    @pl.when(i % tiles_per_seq == 0)
    def _():
        halo[...] = _rms(meta_ref[...], nm_ref[...]).astype(BF)

    def in_copy(step, s):
        return pltpu.make_async_copy(s_hbm.at[0, step * STREAM_TB + s], s_in.at[s], in_sem.at[s])

    def out_copy(step, s):
        return pltpu.make_async_copy(s_out.at[s], so_hbm.at[0, step * STREAM_TB + s], out_sem.at[s])

    @pl.when(i == 0)
    def _():
        for s in range(STREAM_TB):
            in_copy(0, s).start()

    lanes = qt_ref.shape[1]
    shift = (lanes - i * STREAM_TB) % lanes
    qt = pltpu.roll(qt_ref[...], shift, 1)
    kt = pltpu.roll(kt_ref[...], shift, 1)

    h = _pool_prompt_tile(x_ref, halo, nm_ref, wp_ref, sc_ref, bc_ref, bh_ref, iw_ref, tail_ref)
    hn = _rms(h, nw_ref[...]).astype(BF)

    ffn = {"acc": h, "act": None}

    def up_unit(cols):
        c0, c1 = cols
        gt = _dot(hn, wg_ref[:, c0:c1])
        up = _dot(hn, wu_ref[:, c0:c1])
        ffn["act"] = (_silu(gt) * up).astype(BF)

    def down_unit(cols):
        c0, c1 = cols
        ffn["acc"] = ffn["acc"] + _dot(ffn["act"], wd_ref[c0:c1, :])

    units = [functools.partial(u, cols) for cols in FF_CHUNKS for u in (up_unit, down_unit)]
    per_sample = np.array_split(np.arange(len(units)), STREAM_TB)
    for s in range(STREAM_TB):
        in_copy(i, s).wait()

        @pl.when(i > 0)
        def _():
            out_copy(i - 1, s).wait()

        mine = list(per_sample[s])
        for hd in range(RET_HEADS):
            _head_state_update(s, hd, lg_ref, qt, kt, v_ref, gw_ref, gb_ref, s_in, s_out, on_ref)
            if hd == RET_HEADS - 1:
                out_copy(i, s).start()

                @pl.when(i + 1 < n)
                def _():
                    in_copy(i + 1, s).start()
            take = len(mine) - (len(mine) * (RET_HEADS - 1 - hd)) // RET_HEADS
            for u in mine[:take]:
                units[u]()
            mine = mine[take:]
    o_ref[...] = ffn["acc"]

    @pl.when(i == n - 1)
    def _():
        for s in range(STREAM_TB):
            out_copy(i, s).wait()


def _layer0_stream(x, meta, nm, wp, scale, nw, ffn_w, tm, log_g, q_s, k_s, v_s, gn_w, gn_b, s0):
    b, t, _ = x.shape
    rows = b * t
    n = rows // tm
    bs = q_s.shape[0]
    assert bs == n * STREAM_TB and t % tm == 0
    tiles_per_seq = t // tm
    band_cur, band_halo, _, _, inv_w = _band_matrices()
    qt = q_s.astype(F32).T
    kt = k_s.astype(F32).T
    v3 = v_s.astype(F32).reshape(n, STREAM_TB, -1)
    row_spec = pl.BlockSpec((tm, D_MODEL), lambda i, lg: (i, 0))
    tail_spec = pl.BlockSpec((1, N_META, D_MODEL), lambda i, lg: (i // tiles_per_seq, 0, 0))
    v_spec = pl.BlockSpec((None, STREAM_TB, 2 * D_MODEL), lambda i, lg: (i, 0, 0))
    hbm = pl.BlockSpec(memory_space=pl.ANY)
    blk = (STREAM_TB, RET_HEADS, RET_KDIM, RET_VDIM)
    consts = (meta, nm, wp, scale, band_cur, band_halo, inv_w, nw) + tuple(ffn_w) + (qt, kt)
    h_out, tail, on_s, s1 = pl.pallas_call(
        functools.partial(_layer0_stream_kernel, tiles_per_seq),
        grid_spec=pltpu.PrefetchScalarGridSpec(
            num_scalar_prefetch=1,
            grid=(n,),
            in_specs=[row_spec] + [_const_spec(c.shape) for c in consts]
                     + [v_spec, _const_spec(gn_w.shape), _const_spec(gn_b.shape), hbm],
            out_specs=[row_spec, tail_spec, v_spec, hbm],
            scratch_shapes=[pltpu.VMEM((N_META, D_MODEL), BF),
                            pltpu.VMEM(blk, F32), pltpu.VMEM(blk, F32),
                            pltpu.SemaphoreType.DMA((STREAM_TB,)), pltpu.SemaphoreType.DMA((STREAM_TB,))]),
        out_shape=[jax.ShapeDtypeStruct((rows, D_MODEL), F32),
                   jax.ShapeDtypeStruct((b, N_META, D_MODEL), F32),
                   jax.ShapeDtypeStruct(v3.shape, F32),
                   jax.ShapeDtypeStruct(s0.shape, F32)],
        compiler_params=_params(1),
        name="layer0_stream",
    )(log_g, x.reshape(rows, D_MODEL), *consts, v3, gn_w, gn_b, s0)
    return h_out, tail, on_s.reshape(bs, -1), s1


def _cast_row_blocks(rows, n):
    for hold in (1, 2, 4, 8):
        blocks = n // hold
        if n % hold == 0 and rows % blocks == 0 and (rows // blocks) % 16 == 0:
            return rows // blocks, hold
    raise ValueError((rows, n))


def _cast_specs(w32, layer, n):
    rows, cols = w32.shape[1:]
    blk, hold = _cast_row_blocks(rows, n)
    src = pl.BlockSpec((None, blk, cols), lambda i: (layer, i // hold, 0))
    dst = pl.BlockSpec((blk, cols), lambda i: (i // hold, 0))
    return src, dst, jax.ShapeDtypeStruct((rows, cols), BF)


def _proj_kernel(h_ref, nw_ref, w_ref, cos_ref, sin_ref, cross_ref, kdec_ref,
                 c0_ref, c1_ref, c2_ref, c3_ref,
                 q_ref, qx_ref, k_ref, kd_ref, v_ref, g_ref, d0_ref, d1_ref, d2_ref, d3_ref):
    for src, dst in ((c0_ref, d0_ref), (c1_ref, d1_ref), (c2_ref, d2_ref), (c3_ref, d3_ref)):
        dst[...] = src[...].astype(BF)

    hn = _rms(h_ref[...], nw_ref[...]).astype(BF)
    cos = jnp.concatenate([cos_ref[...]] * RET_HEADS, axis=1)
    sin = jnp.concatenate([sin_ref[...]] * RET_HEADS, axis=1)

    q = _rotary(_dot(hn, w_ref[:, 0:D_MODEL]), cos, sin) * (RET_KDIM ** -0.5)
    q_ref[...] = q.astype(BF)
    qx_ref[...] = (q * cross_ref[...]).astype(BF)
    k = _rotary(_dot(hn, w_ref[:, D_MODEL:2 * D_MODEL]), cos, sin)
    k_ref[...] = k.astype(BF)
    kd_ref[...] = (k * kdec_ref[...]).astype(BF)
    v_ref[...] = _dot(hn, w_ref[:, 2 * D_MODEL:4 * D_MODEL]).astype(BF)
    g_ref[...] = _dot(hn, w_ref[:, 4 * D_MODEL:6 * D_MODEL]).astype(BF)


def _proj(h, nw, w_in, cos, sin, cross, kdec, tm, cast):
    rows = h.shape[0]
    n = rows // tm
    ntab = cos.shape[0] // tm
    row_spec = lambda width: pl.BlockSpec((tm, width), lambda i: (i, 0))
    tab_spec = pl.BlockSpec((tm, RET_KDIM), lambda i: (i % ntab, 0))
    widths = (D_MODEL, D_MODEL, D_MODEL, D_MODEL, 2 * D_MODEL, 2 * D_MODEL)
    cast_src, cast_dst, cast_shape = zip(*[_cast_specs(w32, layer, n) for w32, layer in cast])
    out = pl.pallas_call(
        _proj_kernel,
        grid=(n,),
        in_specs=[row_spec(D_MODEL), _const_spec(nw.shape), _const_spec(w_in.shape),
                  tab_spec, tab_spec, _const_spec(cross.shape), _const_spec(kdec.shape)] + list(cast_src),
        out_specs=[row_spec(w) for w in widths] + list(cast_dst),
        out_shape=[jax.ShapeDtypeStruct((rows, w), BF) for w in widths] + list(cast_shape),
        compiler_params=_params(1),
        name="ret_proj",
    )(h, nw, w_in, cos, sin, cross, kdec, *[w32 for w32, _ in cast])
    return tuple(out[:len(widths)]), tuple(out[len(widths):])


def _ret_prompt_kernel(tot_ref, kdm_ref, vm_ref, dm_ref, q_ref, qx_ref, k_ref, kd_ref, v_ref,
                       gw_ref, gb_ref, o_ref, s_ref, state):
    t = pl.program_id(1)

    @pl.when(t == 0)
    def _():
        for h in range(RET_HEADS):
            state[h] = _kt_v(kdm_ref[:, _head_cols(h, RET_KDIM)], vm_ref[:, _head_cols(h, RET_VDIM)])

    for c in range(q_ref.shape[1] // SUB):
        rows = slice(c * SUB, (c + 1) * SUB)
        for h in range(RET_HEADS):
            kc = _head_cols(h, RET_KDIM)
            vc = _head_cols(h, RET_VDIM)
            q = q_ref[0, rows, kc]
            v = v_ref[0, rows, vc]
            s_prev = state[h]
            scores = lax.dot_general(q, k_ref[0, rows, kc], (((1,), (1,)), ((), ())),
                                     preferred_element_type=F32) * dm_ref[h]
            o = _dot(scores.astype(BF), v) + _dot(qx_ref[0, rows, kc], s_prev.astype(BF))
            state[h] = tot_ref[h] * s_prev + _kt_v(kd_ref[0, rows, kc], v)
            o_ref[0, rows, vc] = _group_norm(o, gw_ref[:, vc], gb_ref[:, vc]).astype(BF)

    @pl.when(t == pl.num_programs(1) - 1)
    def _():
        s_ref[0] = state[...]


def _ret_prompt(chunk_decay, kd_meta, v_meta, dmask, q, qx, k, kd, v, gn_w, gn_b, tt=4 * SUB):
    b, t, _ = q.shape
    qk_spec = pl.BlockSpec((1, tt, D_MODEL), lambda i, j, s: (i, j, 0))
    v_spec = pl.BlockSpec((1, tt, 2 * D_MODEL), lambda i, j, s: (i, j, 0))
    st_shape = (RET_HEADS, RET_KDIM, RET_VDIM)
    return pl.pallas_call(
        _ret_prompt_kernel,
        grid_spec=pltpu.PrefetchScalarGridSpec(
            num_scalar_prefetch=1,
            grid=(b, t // tt),
            in_specs=[_const_spec(kd_meta.shape), _const_spec(v_meta.shape), _const_spec(dmask.shape),
                      qk_spec, qk_spec, qk_spec, qk_spec, v_spec,
                      _const_spec(gn_w.shape), _const_spec(gn_b.shape)],
            out_specs=[v_spec,
                       pl.BlockSpec((None, 1) + st_shape, lambda i, j, s: (0, i, 0, 0, 0))],
            scratch_shapes=[pltpu.VMEM(st_shape, F32)]),
        out_shape=[jax.ShapeDtypeStruct((b, t, 2 * D_MODEL), BF),
                   jax.ShapeDtypeStruct((1, b) + st_shape, F32)],
        compiler_params=_params(2),
        name="ret_prompt",
    )(chunk_decay, kd_meta, v_meta, dmask, q, qx, k, kd, v, gn_w, gn_b)


def _out_ffn_rows(h, on, g, wo_ref, nw_ref, wg_ref, wu_ref, wd_ref, nf_ref):
    gated = (_silu(g.astype(F32)) * on.astype(F32)).astype(BF)
    h = h + _dot(gated, wo_ref[...])
    hn = _rms(h, nw_ref[...]).astype(BF)
    for c0, c1 in FF_CHUNKS:
        act = (_silu(_dot(hn, wg_ref[:, c0:c1])) * _dot(hn, wu_ref[:, c0:c1])).astype(BF)
        h = h + _dot(act, wd_ref[c0:c1, :])
    return _rms(h, nf_ref[...])


def _out_ffn_kernel(h_ref, on_ref, g_ref, hs_ref, ons_ref, gs_ref, wo_ref, nw_ref, wg_ref, wu_ref,
                    wd_ref, nf_ref, o_ref, os_ref):
    weights = (wo_ref, nw_ref, wg_ref, wu_ref, wd_ref, nf_ref)
    last = pl.program_id(0) == pl.num_programs(0) - 1

    @pl.when(jnp.logical_not(last))
    def _():
        o_ref[...] = _out_ffn_rows(h_ref[...], on_ref[...], g_ref[...], *weights)

    @pl.when(last)
    def _():
        os_ref[...] = _out_ffn_rows(hs_ref[...], ons_ref[...], gs_ref[...], *weights)


def _out_ffn(h, on, g, h_small, on_s, proj_s, wo, nw, ffn_w, nf, tm):
    rows = h.shape[0]
    n = rows // tm
    rows_s = on_s.shape[0]
    gate_block = (proj_s.shape[1] - 2 * D_MODEL) // (2 * D_MODEL)
    row_spec = lambda width: pl.BlockSpec((tm, width), lambda i: (jnp.minimum(i, n - 1), 0))
    small_spec = lambda width, blk=0: pl.BlockSpec((rows_s, width), lambda i: (0, blk))
    consts = (wo, nw) + tuple(ffn_w) + (nf,)
    return pl.pallas_call(
        _out_ffn_kernel,
        grid=(n + 1,),
        in_specs=[row_spec(D_MODEL), row_spec(2 * D_MODEL), row_spec(2 * D_MODEL),
                  small_spec(D_MODEL), small_spec(2 * D_MODEL), small_spec(2 * D_MODEL, gate_block)]
                 + [_const_spec(c.shape) for c in consts],
        out_specs=[row_spec(D_MODEL), small_spec(D_MODEL)],
        out_shape=[jax.ShapeDtypeStruct((rows, D_MODEL), F32),
                   jax.ShapeDtypeStruct((rows_s, D_MODEL), F32)],
        compiler_params=_params(1),
        name="out_ffn",
    )(h, on, g, h_small, on_s, proj_s, *consts)


def _rotary_tables(pos):
    theta = 1.0 / (ROPE_BASE ** jnp.linspace(0.0, 1.0, RET_KDIM // 2, dtype=F32))
    ang = pos.astype(F32)[:, None] * theta[None, :]
    cos = jnp.repeat(jnp.cos(ang), 2, axis=1)
    sin = jnp.stack([-jnp.sin(ang), jnp.sin(ang)], axis=-1).reshape(ang.shape[0], RET_KDIM)
    return cos, sin


def _per_head_cols(table):
    return jnp.asarray(np.repeat(table, RET_KDIM, axis=1), F32)


def _decay_to_end(c):
    return np.exp((c - 1.0 - np.arange(c))[:, None] * LOG_GAMMA[None, :])


def _decay_from_start(c):
    return np.exp((np.arange(c) + 1.0)[:, None] * LOG_GAMMA[None, :])


def _decay_mask(c):
    rel = np.arange(c)[:, None] - np.arange(c)[None, :]
    return np.where(rel[None] >= 0, np.exp(np.maximum(rel, 0)[None] * LOG_GAMMA[:, None, None]), 0.0)


def kernel(x_prompt, x_sample, state_pool, state_ret, meta_tokens, norm_mix, norm_ffn, norm_final,
           w_pool, pool_scale, w_ret_in, ret_gn_w, ret_gn_b, w_ret_out, w_ffn_gate, w_ffn_up, w_ffn_down):
    b, t, _ = x_prompt.shape
    bs = x_sample.shape[0]
    log_g = jnp.asarray(LOG_GAMMA, F32)

    wp = w_pool[0].astype(BF)
    ffn_w32 = (w_ffn_gate, w_ffn_up, w_ffn_down)
    nm0, nm1 = norm_mix[0:1], norm_mix[1:2]
    nf0, nf1 = norm_ffn[0:1], norm_ffn[1:2]
    nfin = norm_final[None, :]
    scale = pool_scale[0:1]
    gn_w, gn_b = ret_gn_w[0:1], ret_gn_b[0:1]

    hs, new_pool_sample, h_meta = _pool_sample(
        x_sample.reshape(bs, D_MODEL), state_pool.transpose(0, 2, 1, 3), meta_tokens, nm0, wp, scale)
    new_pool_sample = new_pool_sample.transpose(0, 2, 1, 3)
    h_small, ffn_w0 = _small_ffn(jnp.concatenate([hs, h_meta], axis=0), nf0, ffn_w32, 0)

    pos_small = jnp.concatenate([jnp.full((bs,), PAST_LEN, jnp.int32), jnp.arange(N_META, dtype=jnp.int32)])
    cos_s, sin_s = _rotary_tables(pos_small)
    kdec_s = _per_head_cols(np.concatenate([np.ones((bs, RET_HEADS)), _decay_to_end(N_META)], axis=0))
    proj_s, kd_s, w_in = _small_proj(h_small, nm1, w_ret_in, cos_s, sin_s, kdec_s)
    q_s = proj_s[:bs, 0:D_MODEL]
    k_s = proj_s[:bs, D_MODEL:2 * D_MODEL]
    v_s = proj_s[:bs, 2 * D_MODEL:4 * D_MODEL]

    h, tail, on_s, new_ret_sample = _layer0_stream(
        x_prompt, meta_tokens, nm0, wp, scale, nf0, ffn_w0, TM, log_g, q_s, k_s, v_s, gn_w, gn_b, state_ret)

    cos, sin = _rotary_tables(N_META + jnp.arange(t, dtype=jnp.int32))
    reps = (TM // SUB, 1)
    (q, qx, k, kd, v, g), (w_out, *ffn_w1) = _proj(
        h, nm1, w_in, cos, sin,
        _per_head_cols(np.tile(_decay_from_start(SUB), reps)),
        _per_head_cols(np.tile(_decay_to_end(SUB), reps)), TM,
        cast=((w_ret_out, 0), (w_ffn_gate, 1), (w_ffn_up, 1), (w_ffn_down, 1)))
    seq = lambda a: a.reshape(b, t, -1)
    on, new_ret_prompt = _ret_prompt(jnp.asarray(np.exp(SUB * LOG_GAMMA), F32), kd_s[bs:],
                                     proj_s[bs:, 2 * D_MODEL:4 * D_MODEL],
                                     jnp.asarray(_decay_mask(SUB), F32),
                                     seq(q), seq(qx), seq(k), seq(kd), seq(v), gn_w, gn_b)
    y, y_sample = _out_ffn(h, on.reshape(b * t, -1), g, h_small, on_s, proj_s, w_out, nf1, ffn_w1, nfin, TM)

    return (y.reshape(b, t, D_MODEL), y_sample.reshape(bs, 1, D_MODEL), tail[:, 1:][None],
            new_pool_sample, new_ret_prompt, new_ret_sample)
```

```python
import functools

import numpy as np
import jax
import jax.numpy as jnp
from jax import lax
from jax.experimental import pallas as pl
from jax.experimental.pallas import tpu as pltpu

D_MODEL = 1024
N_META = 16
PAST_LEN = 16384
POOL_WINDOWS = (2, 4, 8, 16)
POOL_GROUPS = len(POOL_WINDOWS)
POOL_GROUP_DIM = D_MODEL // POOL_GROUPS
POOL_BUF = max(POOL_WINDOWS) - 1
RET_HEADS = 4
RET_KDIM = D_MODEL // RET_HEADS
RET_VDIM = 2 * D_MODEL // RET_HEADS
ROPE_BASE = 10000.0
D_FF = 2816
EPS = 1e-6

BF = jnp.bfloat16
F32 = jnp.float32

SUB = 256
TM = 512
FF_CHUNKS = tuple((c, min(c + 512, D_FF)) for c in range(0, D_FF, 512))
FF_STREAM = 256
assert D_FF % FF_STREAM == 0
V7X_VMEM_BYTES = 64 * 1024 * 1024
VMEM_LIMIT = V7X_VMEM_BYTES * 7 // 8

LOG_GAMMA = np.log(1.0 - 2.0 ** (-5.0 - np.arange(RET_HEADS)))


def _dot(a, b):
    return jnp.dot(a, b, preferred_element_type=F32)


def _kt_v(k, v):
    return lax.dot_general(k, v, (((0,), (0,)), ((), ())), preferred_element_type=F32)


def _rms(x, g):
    ms = jnp.mean(x * x, axis=-1, keepdims=True)
    return (x * lax.rsqrt(ms + EPS)) * g


def _silu(x):
    return x * jax.nn.sigmoid(x)


def _group_norm(o, gn_w, gn_b):
    mu = jnp.mean(o, axis=-1, keepdims=True)
    cen = o - mu
    var = jnp.mean(cen * cen, axis=-1, keepdims=True)
    return (cen * lax.rsqrt(var + EPS)) * gn_w + gn_b


def _rotary(x, cos, sin):
    lane = lax.broadcasted_iota(jnp.int32, x.shape, 1)
    partner = jnp.where((lane & 1) == 0, pltpu.roll(x, x.shape[1] - 1, 1), pltpu.roll(x, 1, 1))
    return x * cos + partner * sin


def _head_cols(h, width):
    return slice(h * width, (h + 1) * width)


def _group_cols(g):
    return slice(g * POOL_GROUP_DIM, (g + 1) * POOL_GROUP_DIM)


def _const_spec(shape):
    nd = len(shape)
    return pl.BlockSpec(shape, lambda *_: (0,) * nd, pipeline_mode=pl.Buffered(1))


def _params(n_grid):
    return pltpu.CompilerParams(
        dimension_semantics=("arbitrary",) * n_grid, vmem_limit_bytes=VMEM_LIMIT)


def _band_matrices():
    t = np.arange(SUB)[:, None]
    s = np.arange(SUB)[None, :]
    sh = np.arange(N_META)[None, :] - N_META
    cur = np.stack([((t - s >= 0) & (t - s < w)) for w in POOL_WINDOWS]).astype(np.float32)
    halo = np.stack([(t - sh < w) for w in POOL_WINDOWS]).astype(np.float32)
    tm = np.arange(N_META)[:, None]
    sm = np.arange(N_META)[None, :]
    meta = np.stack([((tm - sm >= 0) & (tm - sm < w)) for w in POOL_WINDOWS]).astype(np.float32)
    inv_meta = np.concatenate(
        [np.repeat(1.0 / np.minimum(w, tm + 1.0), POOL_GROUP_DIM, axis=1) for w in POOL_WINDOWS], axis=1)
    inv_w = np.concatenate([np.full((1, POOL_GROUP_DIM), 1.0 / w) for w in POOL_WINDOWS], axis=1)
    return (jnp.asarray(cur, BF), jnp.asarray(halo, BF), jnp.asarray(meta, BF),
            jnp.asarray(inv_meta, F32), jnp.asarray(inv_w, F32))


def _pool_mix(pooled_sum, inv_cnt, xn, wp_ref, scale):
    diff = (pooled_sum * inv_cnt - xn).astype(BF)
    mixed = jnp.concatenate(
        [_dot(diff[:, _group_cols(g)], wp_ref[g]) for g in range(POOL_GROUPS)], axis=1)
    return mixed * scale


def _pool_prompt_tile(x_ref, halo_ref, nw_ref, wp_ref, sc_ref, bc_ref, bh_ref, iw_ref, tail_ref):
    nw = nw_ref[...]
    scale = sc_ref[...]
    inv_w = iw_ref[...]
    halo = halo_ref[...]
    out = []
    for j in range(x_ref.shape[0] // SUB):
        x = x_ref[SUB * j:SUB * (j + 1), :]
        xn = _rms(x, nw)
        xb = xn.astype(BF)
        pooled = jnp.concatenate(
            [_dot(bc_ref[g], xb[:, _group_cols(g)]) + _dot(bh_ref[g], halo[:, _group_cols(g)])
             for g in range(POOL_GROUPS)], axis=1)
        out.append(x + _pool_mix(pooled, inv_w, xn, wp_ref, scale))
        halo = xb[SUB - N_META:, :]
    halo_ref[...] = halo
    tail_ref[0] = xn[SUB - N_META:, :]
    return jnp.concatenate(out, axis=0)


def _pool_sample_kernel(x_ref, prev_ref, meta_ref, nw_ref, wp_ref, sc_ref, iw_ref, bm_ref, icm_ref,
                        h_ref, np_ref, hm_ref):
    @pl.when(pl.program_id(0) == 0)
    def _():
        xm = meta_ref[...]
        xnm = _rms(xm, nw_ref[...])
        xbm = xnm.astype(BF)
        pooled_m = jnp.concatenate(
            [_dot(bm_ref[g], xbm[:, _group_cols(g)]) for g in range(POOL_GROUPS)], axis=1)
        hm_ref[...] = xm + _pool_mix(pooled_m, icm_ref[...], xnm, wp_ref, sc_ref[...])

    x = x_ref[:, 0, :]
    xn = _rms(x, nw_ref[...])
    sums = []
    for g, w in enumerate(POOL_WINDOWS):
        s = xn[:, _group_cols(g)]
        for j in range(1, w):
            s = s + prev_ref[POOL_BUF - j, :, _group_cols(g)]
        sums.append(s)
    pooled = jnp.concatenate(sums, axis=1)
    h_ref[...] = x + _pool_mix(pooled, iw_ref[...], xn, wp_ref, sc_ref[...])
    for r in range(POOL_BUF - 1):
        np_ref[r] = prev_ref[r + 1]
    np_ref[POOL_BUF - 1] = xn


def _pool_sample(x, prev, meta, nw, wp, scale, tb=32):
    b = x.shape[0]
    _, _, band_meta, inv_meta, inv_w = _band_matrices()
    consts = (meta, nw, wp, scale, inv_w, band_meta, inv_meta)
    st_spec = pl.BlockSpec((None, POOL_BUF, tb, D_MODEL), lambda i: (0, 0, i, 0))
    return pl.pallas_call(
        _pool_sample_kernel,
        grid=(b // tb,),
        in_specs=[pl.BlockSpec((tb, 1, D_MODEL), lambda i: (i, 0, 0)), st_spec]
                 + [_const_spec(c.shape) for c in consts],
        out_specs=[pl.BlockSpec((tb, D_MODEL), lambda i: (i, 0)), st_spec,
                   pl.BlockSpec(meta.shape, lambda i: (0, 0))],
        out_shape=[jax.ShapeDtypeStruct((b, D_MODEL), F32),
                   jax.ShapeDtypeStruct(prev.shape, F32),
                   jax.ShapeDtypeStruct(meta.shape, F32)],
        compiler_params=_params(1),
        name="pool_sample",
    )(x, prev, *consts)


def _small_ffn_kernel(hs_ref, hm_ref, nw_ref, wg_ref, wu_ref, wd_ref, o_ref, wg_bf, wu_bf, wd_bf, hn):
    @pl.when(pl.program_id(0) == 0)
    def _():
        h = jnp.concatenate([hs_ref[...], hm_ref[...]], axis=0)
        hn[...] = _rms(h, nw_ref[...]).astype(BF)
        o_ref[...] = h

    wg = wg_ref[...].astype(BF)
    wu = wu_ref[...].astype(BF)
    wd = wd_ref[...].astype(BF)
    wg_bf[...] = wg
    wu_bf[...] = wu
    wd_bf[...] = wd
    act = (_silu(_dot(hn[...], wg)) * _dot(hn[...], wu)).astype(BF)
    o_ref[...] += _dot(act, wd)


def _small_ffn(hs, hm, nw, ffn_w32, layer):
    rows = hs.shape[0] + hm.shape[0]
    whole = pl.BlockSpec((rows, D_MODEL), lambda c: (0, 0))
    out = pl.pallas_call(
        _small_ffn_kernel,
        grid=(D_FF // FF_STREAM,),
        in_specs=[_const_spec(hs.shape), _const_spec(hm.shape), _const_spec(nw.shape),
                  pl.BlockSpec((None, D_MODEL, FF_STREAM), lambda c: (layer, 0, c)),
                  pl.BlockSpec((None, D_MODEL, FF_STREAM), lambda c: (layer, 0, c)),
                  pl.BlockSpec((None, FF_STREAM, D_MODEL), lambda c: (layer, c, 0))],
        out_specs=[whole,
                   pl.BlockSpec((D_MODEL, FF_STREAM), lambda c: (0, c)),
                   pl.BlockSpec((D_MODEL, FF_STREAM), lambda c: (0, c)),
                   pl.BlockSpec((FF_STREAM, D_MODEL), lambda c: (c, 0))],
        out_shape=[jax.ShapeDtypeStruct((rows, D_MODEL), F32),
                   jax.ShapeDtypeStruct((D_MODEL, D_FF), BF), jax.ShapeDtypeStruct((D_MODEL, D_FF), BF),
                   jax.ShapeDtypeStruct((D_FF, D_MODEL), BF)],
        scratch_shapes=[pltpu.VMEM((rows, D_MODEL), BF)],
        compiler_params=_params(1),
        name="small_ffn",
    )(hs, hm, nw, *ffn_w32)
    return out[0], tuple(out[1:])


def _small_proj_kernel(h_ref, nw_ref, w_ref, cos_ref, sin_ref, kdec_ref,
                       p_ref, kd_ref, qt_ref, kt_ref, v_ref, w_bf, hn):
    c = pl.program_id(0)
    bs = qt_ref.shape[1]

    @pl.when(c == 0)
    def _():
        hn[...] = _rms(h_ref[...], nw_ref[...]).astype(BF)

    w = w_ref[...].astype(BF)
    w_bf[...] = w
    y = _dot(hn[...], w)

    def rotated():
        return _rotary(y, jnp.concatenate([cos_ref[...]] * RET_HEADS, axis=1),
                       jnp.concatenate([sin_ref[...]] * RET_HEADS, axis=1))

    @pl.when(c == 0)
    def _():
        q = (rotated() * (RET_KDIM ** -0.5)).astype(BF)
        p_ref[...] = q
        qt_ref[...] = q[:bs].astype(F32).T

    @pl.when(c == 1)
    def _():
        k = rotated()
        p_ref[...] = k.astype(BF)
        kt_ref[...] = k[:bs].astype(BF).astype(F32).T
        kd_ref[...] = (k * kdec_ref[...]).astype(BF)

    @pl.when((c == 2) | (c == 3))
    def _():
        v = y.astype(BF)
        p_ref[...] = v
        v_ref[...] = v[:bs].astype(F32)

    @pl.when(c >= 4)
    def _():
        p_ref[...] = y.astype(BF)


def _small_proj(h, nw, w_in32, cos, sin, kdec, bs):
    rows = h.shape[0]
    whole = lambda r, w: pl.BlockSpec((r, w), lambda c: (0, 0))
    v_chunk = lambda c: jnp.clip(c - 2, 0, 1)
    return pl.pallas_call(
        _small_proj_kernel,
        grid=(6,),
        in_specs=[whole(rows, D_MODEL), _const_spec(nw.shape),
                  pl.BlockSpec((None, D_MODEL, D_MODEL), lambda c: (0, 0, c)),
                  _const_spec(cos.shape), _const_spec(sin.shape), _const_spec(kdec.shape)],
        out_specs=[pl.BlockSpec((rows, D_MODEL), lambda c: (0, c)),
                   whole(rows, D_MODEL), whole(D_MODEL, bs), whole(D_MODEL, bs),
                   pl.BlockSpec((bs, D_MODEL), lambda c: (0, v_chunk(c))),
                   pl.BlockSpec((D_MODEL, D_MODEL), lambda c: (0, c))],
        out_shape=[jax.ShapeDtypeStruct((rows, 6 * D_MODEL), BF),
                   jax.ShapeDtypeStruct((rows, D_MODEL), BF),
                   jax.ShapeDtypeStruct((D_MODEL, bs), F32),
                   jax.ShapeDtypeStruct((D_MODEL, bs), F32),
                   jax.ShapeDtypeStruct((bs, 2 * D_MODEL), F32),
                   jax.ShapeDtypeStruct((D_MODEL, 6 * D_MODEL), BF)],
        scratch_shapes=[pltpu.VMEM((rows, D_MODEL), BF)],
        compiler_params=_params(1),
        name="small_proj",
    )(h, nw, w_in32, cos, sin, kdec)


STREAM_TB = 4


def _head_state_update(row, s, h, lg_ref, qt, kt, v_ref, gw_ref, gb_ref, s_in, s_out, on_ref):
    gamma = jnp.exp(jnp.full((1, 1), 1.0, F32) * lg_ref[h])
    kc = _head_cols(h, RET_KDIM)
    vc = _head_cols(h, RET_VDIM)
    q = qt[kc, s:s + 1]
    k = kt[kc, s:s + 1]
    v = v_ref[pl.ds(row, 1), vc]
    s_prev = s_in[s, h]
    qs = jnp.sum(q * s_prev, axis=0, keepdims=True)
    score = jnp.sum(q * k, axis=0, keepdims=True)
    o = score * v + qs * gamma
    s_out[s, h] = gamma * s_prev + k * v
    on_ref[pl.ds(row, 1), vc] = _group_norm(o, gw_ref[:, vc], gb_ref[:, vc])


def _layer0_stream_kernel(tiles_per_seq, lg_ref, x_ref, meta_ref, nm_ref, wp_ref, sc_ref, bc_ref, bh_ref,
                          iw_ref, nw_ref, wg_ref, wu_ref, wd_ref, qt_ref, kt_ref, v_ref, gw_ref, gb_ref,
                          s_hbm, o_ref, tail_ref, on_ref, so_hbm, halo, s_in, s_out, in_sem, out_sem):
    i = pl.program_id(0)
    n = pl.num_programs(0)

    @pl.when(i % tiles_per_seq == 0)
    def _():
        halo[...] = _rms(meta_ref[...], nm_ref[...]).astype(BF)

    def in_copy(step, s):
        return pltpu.make_async_copy(s_hbm.at[0, step * STREAM_TB + s], s_in.at[s], in_sem.at[s])

    def out_copy(step, s):
        return pltpu.make_async_copy(s_out.at[s], so_hbm.at[0, step * STREAM_TB + s], out_sem.at[s])

    @pl.when(i == 0)
    def _():
        for s in range(STREAM_TB):
            in_copy(0, s).start()

    lanes = qt_ref.shape[1]
    shift = (lanes - i * STREAM_TB) % lanes
    qt = pltpu.roll(qt_ref[...], shift, 1)
    kt = pltpu.roll(kt_ref[...], shift, 1)

    h = _pool_prompt_tile(x_ref, halo, nm_ref, wp_ref, sc_ref, bc_ref, bh_ref, iw_ref, tail_ref)
    hn = _rms(h, nw_ref[...]).astype(BF)

    ffn = {"acc": h, "act": None}

    def up_unit(cols):
        c0, c1 = cols
        gt = _dot(hn, wg_ref[:, c0:c1])
        up = _dot(hn, wu_ref[:, c0:c1])
        ffn["act"] = (_silu(gt) * up).astype(BF)

    def down_unit(cols):
        c0, c1 = cols
        ffn["acc"] = ffn["acc"] + _dot(ffn["act"], wd_ref[c0:c1, :])

    units = [functools.partial(u, cols) for cols in FF_CHUNKS for u in (up_unit, down_unit)]
    per_sample = np.array_split(np.arange(len(units)), STREAM_TB)
    for s in range(STREAM_TB):
        in_copy(i, s).wait()

        @pl.when(i > 0)
        def _():
            out_copy(i - 1, s).wait()

        mine = list(per_sample[s])
        for hd in range(RET_HEADS):
            _head_state_update(i * STREAM_TB + s, s, hd, lg_ref, qt, kt, v_ref, gw_ref, gb_ref,
                               s_in, s_out, on_ref)
            if hd == RET_HEADS - 1:
                out_copy(i, s).start()

                @pl.when(i + 1 < n)
                def _():
                    in_copy(i + 1, s).start()
            take = len(mine) - (len(mine) * (RET_HEADS - 1 - hd)) // RET_HEADS
            for u in mine[:take]:
                units[u]()
            mine = mine[take:]
    o_ref[...] = ffn["acc"]

    @pl.when(i == n - 1)
    def _():
        for s in range(STREAM_TB):
            out_copy(i, s).wait()


def _layer0_stream(x, meta, nm, wp, scale, nw, ffn_w, tm, log_g, qt, kt, v_s, gn_w, gn_b, s0):
    b, t, _ = x.shape
    rows = b * t
    n = rows // tm
    assert v_s.shape[0] == n * STREAM_TB and t % tm == 0
    tiles_per_seq = t // tm
    band_cur, band_halo, _, _, inv_w = _band_matrices()
    row_spec = pl.BlockSpec((tm, D_MODEL), lambda i, lg: (i, 0))
    tail_spec = pl.BlockSpec((1, N_META, D_MODEL), lambda i, lg: (i // tiles_per_seq, 0, 0))
    on_spec = pl.BlockSpec(v_s.shape, lambda i, lg: (0, 0))
    hbm = pl.BlockSpec(memory_space=pl.ANY)
    blk = (STREAM_TB, RET_HEADS, RET_KDIM, RET_VDIM)
    consts = (meta, nm, wp, scale, band_cur, band_halo, inv_w, nw) + tuple(ffn_w) + (qt, kt, v_s, gn_w, gn_b)
    return pl.pallas_call(
        functools.partial(_layer0_stream_kernel, tiles_per_seq),
        grid_spec=pltpu.PrefetchScalarGridSpec(
            num_scalar_prefetch=1,
            grid=(n,),
            in_specs=[row_spec] + [_const_spec(c.shape) for c in consts] + [hbm],
            out_specs=[row_spec, tail_spec, on_spec, hbm],
            scratch_shapes=[pltpu.VMEM((N_META, D_MODEL), BF),
                            pltpu.VMEM(blk, F32), pltpu.VMEM(blk, F32),
                            pltpu.SemaphoreType.DMA((STREAM_TB,)), pltpu.SemaphoreType.DMA((STREAM_TB,))]),
        out_shape=[jax.ShapeDtypeStruct((rows, D_MODEL), F32),
                   jax.ShapeDtypeStruct((b, N_META, D_MODEL), F32),
                   jax.ShapeDtypeStruct(v_s.shape, F32),
                   jax.ShapeDtypeStruct(s0.shape, F32)],
        compiler_params=_params(1),
        name="layer0_stream",
    )(log_g, x.reshape(rows, D_MODEL), *consts, s0)


def _cast_row_blocks(rows, n):
    for hold in (1, 2, 4, 8):
        blocks = n // hold
        if n % hold == 0 and rows % blocks == 0 and (rows // blocks) % 16 == 0:
            return rows // blocks, hold
    raise ValueError((rows, n))


def _cast_specs(w32, layer, n):
    rows, cols = w32.shape[1:]
    blk, hold = _cast_row_blocks(rows, n)
    src = pl.BlockSpec((None, blk, cols), lambda i: (layer, i // hold, 0))
    dst = pl.BlockSpec((blk, cols), lambda i: (i // hold, 0))
    return src, dst, jax.ShapeDtypeStruct((rows, cols), BF)


def _proj_kernel(h_ref, nw_ref, w_ref, cos_ref, sin_ref, cross_ref, kdec_ref,
                 c0_ref, c1_ref, c2_ref, c3_ref,
                 q_ref, qx_ref, k_ref, kd_ref, v_ref, g_ref, d0_ref, d1_ref, d2_ref, d3_ref):
    for src, dst in ((c0_ref, d0_ref), (c1_ref, d1_ref), (c2_ref, d2_ref), (c3_ref, d3_ref)):
        dst[...] = src[...].astype(BF)

    hn = _rms(h_ref[...], nw_ref[...]).astype(BF)
    cos = jnp.concatenate([cos_ref[...]] * RET_HEADS, axis=1)
    sin = jnp.concatenate([sin_ref[...]] * RET_HEADS, axis=1)

    q = _rotary(_dot(hn, w_ref[:, 0:D_MODEL]), cos, sin) * (RET_KDIM ** -0.5)
    q_ref[...] = q.astype(BF)
    qx_ref[...] = (q * cross_ref[...]).astype(BF)
    k = _rotary(_dot(hn, w_ref[:, D_MODEL:2 * D_MODEL]), cos, sin)
    k_ref[...] = k.astype(BF)
    kd_ref[...] = (k * kdec_ref[...]).astype(BF)
    v_ref[...] = _dot(hn, w_ref[:, 2 * D_MODEL:4 * D_MODEL]).astype(BF)
    g_ref[...] = _dot(hn, w_ref[:, 4 * D_MODEL:6 * D_MODEL]).astype(BF)


def _proj(h, nw, w_in, cos, sin, cross, kdec, tm, cast):
    rows = h.shape[0]
    n = rows // tm
    ntab = cos.shape[0] // tm
    row_spec = lambda width: pl.BlockSpec((tm, width), lambda i: (i, 0))
    tab_spec = pl.BlockSpec((tm, RET_KDIM), lambda i: (i % ntab, 0))
    widths = (D_MODEL, D_MODEL, D_MODEL, D_MODEL, 2 * D_MODEL, 2 * D_MODEL)
    cast_src, cast_dst, cast_shape = zip(*[_cast_specs(w32, layer, n) for w32, layer in cast])
    out = pl.pallas_call(
        _proj_kernel,
        grid=(n,),
        in_specs=[row_spec(D_MODEL), _const_spec(nw.shape), _const_spec(w_in.shape),
                  tab_spec, tab_spec, _const_spec(cross.shape), _const_spec(kdec.shape)] + list(cast_src),
        out_specs=[row_spec(w) for w in widths] + list(cast_dst),
        out_shape=[jax.ShapeDtypeStruct((rows, w), BF) for w in widths] + list(cast_shape),
        compiler_params=_params(1),
        name="ret_proj",
    )(h, nw, w_in, cos, sin, cross, kdec, *[w32 for w32, _ in cast])
    return tuple(out[:len(widths)]), tuple(out[len(widths):])


def _ret_prompt_kernel(tot_ref, kdm_ref, vm_ref, dm_ref, q_ref, qx_ref, k_ref, kd_ref, v_ref,
                       gw_ref, gb_ref, o_ref, s_ref, state):
    t = pl.program_id(1)

    @pl.when(t == 0)
    def _():
        for h in range(RET_HEADS):
            state[h] = _kt_v(kdm_ref[:, _head_cols(h, RET_KDIM)], vm_ref[:, _head_cols(h, RET_VDIM)])

    for c in range(q_ref.shape[1] // SUB):
        rows = slice(c * SUB, (c + 1) * SUB)
        for h in range(RET_HEADS):
            kc = _head_cols(h, RET_KDIM)
            vc = _head_cols(h, RET_VDIM)
            q = q_ref[0, rows, kc]
            v = v_ref[0, rows, vc]
            s_prev = state[h]
            scores = lax.dot_general(q, k_ref[0, rows, kc], (((1,), (1,)), ((), ())),
                                     preferred_element_type=F32) * dm_ref[h]
            o = _dot(scores.astype(BF), v) + _dot(qx_ref[0, rows, kc], s_prev.astype(BF))
            state[h] = tot_ref[h] * s_prev + _kt_v(kd_ref[0, rows, kc], v)
            o_ref[0, rows, vc] = _group_norm(o, gw_ref[:, vc], gb_ref[:, vc]).astype(BF)

    @pl.when(t == pl.num_programs(1) - 1)
    def _():
        s_ref[0] = state[...]


def _ret_prompt(chunk_decay, kd_meta, v_meta, dmask, q, qx, k, kd, v, gn_w, gn_b, tt=4 * SUB):
    b, t, _ = q.shape
    qk_spec = pl.BlockSpec((1, tt, D_MODEL), lambda i, j, s: (i, j, 0))
    v_spec = pl.BlockSpec((1, tt, 2 * D_MODEL), lambda i, j, s: (i, j, 0))
    st_shape = (RET_HEADS, RET_KDIM, RET_VDIM)
    return pl.pallas_call(
        _ret_prompt_kernel,
        grid_spec=pltpu.PrefetchScalarGridSpec(
            num_scalar_prefetch=1,
            grid=(b, t // tt),
            in_specs=[_const_spec(kd_meta.shape), _const_spec(v_meta.shape), _const_spec(dmask.shape),
                      qk_spec, qk_spec, qk_spec, qk_spec, v_spec,
                      _const_spec(gn_w.shape), _const_spec(gn_b.shape)],
            out_specs=[v_spec,
                       pl.BlockSpec((None, 1) + st_shape, lambda i, j, s: (0, i, 0, 0, 0))],
            scratch_shapes=[pltpu.VMEM(st_shape, F32)]),
        out_shape=[jax.ShapeDtypeStruct((b, t, 2 * D_MODEL), BF),
                   jax.ShapeDtypeStruct((1, b) + st_shape, F32)],
        compiler_params=_params(2),
        name="ret_prompt",
    )(chunk_decay, kd_meta, v_meta, dmask, q, qx, k, kd, v, gn_w, gn_b)


def _out_ffn_rows(h, on, g, wo_ref, nw_ref, wg_ref, wu_ref, wd_ref, nf_ref):
    gated = (_silu(g.astype(F32)) * on.astype(F32)).astype(BF)
    h = h + _dot(gated, wo_ref[...])
    hn = _rms(h, nw_ref[...]).astype(BF)
    for c0, c1 in FF_CHUNKS:
        act = (_silu(_dot(hn, wg_ref[:, c0:c1])) * _dot(hn, wu_ref[:, c0:c1])).astype(BF)
        h = h + _dot(act, wd_ref[c0:c1, :])
    return _rms(h, nf_ref[...])


def _out_ffn_kernel(h_ref, on_ref, g_ref, hs_ref, ons_ref, gs_ref, wo_ref, nw_ref, wg_ref, wu_ref,
                    wd_ref, nf_ref, o_ref, os_ref):
    weights = (wo_ref, nw_ref, wg_ref, wu_ref, wd_ref, nf_ref)
    last = pl.program_id(0) == pl.num_programs(0) - 1

    @pl.when(jnp.logical_not(last))
    def _():
        o_ref[...] = _out_ffn_rows(h_ref[...], on_ref[...], g_ref[...], *weights)

    @pl.when(last)
    def _():
        os_ref[...] = _out_ffn_rows(hs_ref[...], ons_ref[...], gs_ref[...], *weights)


def _out_ffn(h, on, g, h_small, on_s, proj_s, wo, nw, ffn_w, nf, tm):
    rows = h.shape[0]
    n = rows // tm
    rows_s = on_s.shape[0]
    gate_block = (proj_s.shape[1] - 2 * D_MODEL) // (2 * D_MODEL)
    row_spec = lambda width: pl.BlockSpec((tm, width), lambda i: (jnp.minimum(i, n - 1), 0))
    small_spec = lambda width, blk=0: pl.BlockSpec((rows_s, width), lambda i: (0, blk))
    consts = (wo, nw) + tuple(ffn_w) + (nf,)
    return pl.pallas_call(
        _out_ffn_kernel,
        grid=(n + 1,),
        in_specs=[row_spec(D_MODEL), row_spec(2 * D_MODEL), row_spec(2 * D_MODEL),
                  small_spec(D_MODEL), small_spec(2 * D_MODEL), small_spec(2 * D_MODEL, gate_block)]
                 + [_const_spec(c.shape) for c in consts],
        out_specs=[row_spec(D_MODEL), small_spec(D_MODEL)],
        out_shape=[jax.ShapeDtypeStruct((rows, D_MODEL), F32),
                   jax.ShapeDtypeStruct((rows_s, D_MODEL), F32)],
        compiler_params=_params(1),
        name="out_ffn",
    )(h, on, g, h_small, on_s, proj_s, *consts)


def _rotary_tables(pos):
    theta = 1.0 / (ROPE_BASE ** jnp.linspace(0.0, 1.0, RET_KDIM // 2, dtype=F32))
    ang = pos.astype(F32)[:, None] * jnp.repeat(theta, 2)[None, :]
    sign = jnp.asarray(np.tile([-1.0, 1.0], RET_KDIM // 2), F32)
    return jnp.cos(ang), jnp.sin(ang) * sign[None, :]


def _per_head_cols(table):
    return jnp.asarray(np.repeat(table, RET_KDIM, axis=1), F32)


def _decay_to_end(c):
    return np.exp((c - 1.0 - np.arange(c))[:, None] * LOG_GAMMA[None, :])


def _decay_from_start(c):
    return np.exp((np.arange(c) + 1.0)[:, None] * LOG_GAMMA[None, :])


def _decay_mask(c):
    rel = np.arange(c)[:, None] - np.arange(c)[None, :]
    return np.where(rel[None] >= 0, np.exp(np.maximum(rel, 0)[None] * LOG_GAMMA[:, None, None]), 0.0)


def kernel(x_prompt, x_sample, state_pool, state_ret, meta_tokens, norm_mix, norm_ffn, norm_final,
           w_pool, pool_scale, w_ret_in, ret_gn_w, ret_gn_b, w_ret_out, w_ffn_gate, w_ffn_up, w_ffn_down):
    b, t, _ = x_prompt.shape
    bs = x_sample.shape[0]
    log_g = jnp.asarray(LOG_GAMMA, F32)

    wp = w_pool[0].astype(BF)
    ffn_w32 = (w_ffn_gate, w_ffn_up, w_ffn_down)
    nm0, nm1 = norm_mix[0:1], norm_mix[1:2]
    nf0, nf1 = norm_ffn[0:1], norm_ffn[1:2]
    nfin = norm_final[None, :]
    scale = pool_scale[0:1]
    gn_w, gn_b = ret_gn_w[0:1], ret_gn_b[0:1]

    hs, new_pool_sample, h_meta = _pool_sample(
        x_sample, state_pool.transpose(0, 2, 1, 3), meta_tokens, nm0, wp, scale)
    new_pool_sample = new_pool_sample.transpose(0, 2, 1, 3)
    h_small, ffn_w0 = _small_ffn(hs, h_meta, nf0, ffn_w32, 0)

    pos_small = np.concatenate([np.full((bs,), PAST_LEN), np.arange(N_META)]).astype(np.int32)
    cos_s, sin_s = _rotary_tables(jnp.asarray(pos_small))
    kdec_s = _per_head_cols(np.concatenate([np.ones((bs, RET_HEADS)), _decay_to_end(N_META)], axis=0))
    proj_s, kd_s, qt_s, kt_s, v_s, w_in = _small_proj(h_small, nm1, w_ret_in, cos_s, sin_s, kdec_s, bs)

    h, tail, on_s, new_ret_sample = _layer0_stream(
        x_prompt, meta_tokens, nm0, wp, scale, nf0, ffn_w0, TM, log_g, qt_s, kt_s, v_s, gn_w, gn_b, state_ret)

    cos, sin = _rotary_tables(N_META + jnp.arange(t, dtype=jnp.int32))
    reps = (TM // SUB, 1)
    (q, qx, k, kd, v, g), (w_out, *ffn_w1) = _proj(
        h, nm1, w_in, cos, sin,
        _per_head_cols(np.tile(_decay_from_start(SUB), reps)),
        _per_head_cols(np.tile(_decay_to_end(SUB), reps)), TM,
        cast=((w_ret_out, 0), (w_ffn_gate, 1), (w_ffn_up, 1), (w_ffn_down, 1)))
    seq = lambda a: a.reshape(b, t, -1)
    on, new_ret_prompt = _ret_prompt(jnp.asarray(np.exp(SUB * LOG_GAMMA), F32), kd_s[bs:],
                                     proj_s[bs:, 2 * D_MODEL:4 * D_MODEL],
                                     jnp.asarray(_decay_mask(SUB), F32),
                                     seq(q), seq(qx), seq(k), seq(kd), seq(v), gn_w, gn_b)
    y, y_sample = _out_ffn(h, on.reshape(b * t, -1), g, h_small, on_s, proj_s, w_out, nf1, ffn_w1, nfin, TM)

    return (y.reshape(b, t, D_MODEL), y_sample.reshape(bs, 1, D_MODEL), tail[:, 1:][None],
            new_pool_sample, new_ret_prompt, new_ret_sample)
```

```python
import functools

import numpy as np
import jax
import jax.numpy as jnp
from jax import lax
from jax.experimental import pallas as pl
from jax.experimental.pallas import tpu as pltpu

D_MODEL = 1024
N_META = 16
PAST_LEN = 16384
POOL_WINDOWS = (2, 4, 8, 16)
POOL_GROUPS = len(POOL_WINDOWS)
POOL_GROUP_DIM = D_MODEL // POOL_GROUPS
POOL_BUF = max(POOL_WINDOWS) - 1
RET_HEADS = 4
RET_KDIM = D_MODEL // RET_HEADS
RET_VDIM = 2 * D_MODEL // RET_HEADS
ROPE_BASE = 10000.0
D_FF = 2816
EPS = 1e-6

BF = jnp.bfloat16
F32 = jnp.float32

SUB = 256
TM = 512
FF_CHUNKS = tuple((c, min(c + 512, D_FF)) for c in range(0, D_FF, 512))
FF_STREAM = 256
assert D_FF % FF_STREAM == 0
V7X_VMEM_BYTES = 64 * 1024 * 1024
VMEM_LIMIT = V7X_VMEM_BYTES * 7 // 8

LOG_GAMMA = np.log(1.0 - 2.0 ** (-5.0 - np.arange(RET_HEADS)))


def _dot(a, b):
    return jnp.dot(a, b, preferred_element_type=F32)


def _kt_v(k, v):
    return lax.dot_general(k, v, (((0,), (0,)), ((), ())), preferred_element_type=F32)


def _rms(x, g):
    ms = jnp.mean(x * x, axis=-1, keepdims=True)
    return (x * lax.rsqrt(ms + EPS)) * g


def _silu(x):
    return x * jax.nn.sigmoid(x)


def _group_norm(o, gn_w, gn_b):
    mu = jnp.mean(o, axis=-1, keepdims=True)
    cen = o - mu
    var = jnp.mean(cen * cen, axis=-1, keepdims=True)
    return (cen * lax.rsqrt(var + EPS)) * gn_w + gn_b


def _rotary(x, cos, sin):
    lane = lax.broadcasted_iota(jnp.int32, x.shape, 1)
    partner = jnp.where((lane & 1) == 0, pltpu.roll(x, x.shape[1] - 1, 1), pltpu.roll(x, 1, 1))
    return x * cos + partner * sin


def _head_cols(h, width):
    return slice(h * width, (h + 1) * width)


def _group_cols(g):
    return slice(g * POOL_GROUP_DIM, (g + 1) * POOL_GROUP_DIM)


def _const_spec(shape):
    nd = len(shape)
    return pl.BlockSpec(shape, lambda *_: (0,) * nd, pipeline_mode=pl.Buffered(1))


def _params(n_grid):
    return pltpu.CompilerParams(
        dimension_semantics=("arbitrary",) * n_grid, vmem_limit_bytes=VMEM_LIMIT)


def _band_matrices():
    t = np.arange(SUB)[:, None]
    s = np.arange(SUB)[None, :]
    sh = np.arange(N_META)[None, :] - N_META
    cur = np.stack([((t - s >= 0) & (t - s < w)) for w in POOL_WINDOWS]).astype(np.float32)
    halo = np.stack([(t - sh < w) for w in POOL_WINDOWS]).astype(np.float32)
    tm = np.arange(N_META)[:, None]
    sm = np.arange(N_META)[None, :]
    meta = np.stack([((tm - sm >= 0) & (tm - sm < w)) for w in POOL_WINDOWS]).astype(np.float32)
    inv_meta = np.concatenate(
        [np.repeat(1.0 / np.minimum(w, tm + 1.0), POOL_GROUP_DIM, axis=1) for w in POOL_WINDOWS], axis=1)
    inv_w = np.concatenate([np.full((1, POOL_GROUP_DIM), 1.0 / w) for w in POOL_WINDOWS], axis=1)
    return (jnp.asarray(cur, BF), jnp.asarray(halo, BF), jnp.asarray(meta, BF),
            jnp.asarray(inv_meta, F32), jnp.asarray(inv_w, F32))


def _pool_mix(pooled_sum, inv_cnt, xn, wp_ref, scale):
    diff = (pooled_sum * inv_cnt - xn).astype(BF)
    mixed = jnp.concatenate(
        [_dot(diff[:, _group_cols(g)], wp_ref[g]) for g in range(POOL_GROUPS)], axis=1)
    return mixed * scale


def _pool_prompt_tile(x_ref, halo_ref, nw_ref, wp_ref, sc_ref, bc_ref, bh_ref, iw_ref, tail_ref):
    nw = nw_ref[...]
    scale = sc_ref[...]
    inv_w = iw_ref[...]
    halo = halo_ref[...]
    out = []
    for j in range(x_ref.shape[0] // SUB):
        x = x_ref[SUB * j:SUB * (j + 1), :]
        xn = _rms(x, nw)
        xb = xn.astype(BF)
        pooled = jnp.concatenate(
            [_dot(bc_ref[g], xb[:, _group_cols(g)]) + _dot(bh_ref[g], halo[:, _group_cols(g)])
             for g in range(POOL_GROUPS)], axis=1)
        out.append(x + _pool_mix(pooled, inv_w, xn, wp_ref, scale))
        halo = xb[SUB - N_META:, :]
    halo_ref[...] = halo
    tail_ref[0] = xn[SUB - N_META:, :]
    return jnp.concatenate(out, axis=0)


def _pool_sample_kernel(x_ref, prev_ref, meta_ref, nw_ref, wp_ref, sc_ref, iw_ref, bm_ref, icm_ref,
                        h_ref, np_ref, hm_ref):
    @pl.when(pl.program_id(0) == 0)
    def _():
        xm = meta_ref[...]
        xnm = _rms(xm, nw_ref[...])
        xbm = xnm.astype(BF)
        pooled_m = jnp.concatenate(
            [_dot(bm_ref[g], xbm[:, _group_cols(g)]) for g in range(POOL_GROUPS)], axis=1)
        hm_ref[...] = xm + _pool_mix(pooled_m, icm_ref[...], xnm, wp_ref, sc_ref[...])

    x = x_ref[:, 0, :]
    xn = _rms(x, nw_ref[...])
    sums = []
    for g, w in enumerate(POOL_WINDOWS):
        s = xn[:, _group_cols(g)]
        for j in range(1, w):
            s = s + prev_ref[POOL_BUF - j, :, _group_cols(g)]
        sums.append(s)
    pooled = jnp.concatenate(sums, axis=1)
    h_ref[...] = x + _pool_mix(pooled, iw_ref[...], xn, wp_ref, sc_ref[...])
    for r in range(POOL_BUF - 1):
        np_ref[r] = prev_ref[r + 1]
    np_ref[POOL_BUF - 1] = xn


def _pool_sample(x, prev, meta, nw, wp, scale, tb=32):
    b = x.shape[0]
    _, _, band_meta, inv_meta, inv_w = _band_matrices()
    consts = (meta, nw, wp, scale, inv_w, band_meta, inv_meta)
    st_spec = pl.BlockSpec((None, POOL_BUF, tb, D_MODEL), lambda i: (0, 0, i, 0))
    return pl.pallas_call(
        _pool_sample_kernel,
        grid=(b // tb,),
        in_specs=[pl.BlockSpec((tb, 1, D_MODEL), lambda i: (i, 0, 0)), st_spec]
                 + [_const_spec(c.shape) for c in consts],
        out_specs=[pl.BlockSpec((tb, D_MODEL), lambda i: (i, 0)), st_spec,
                   pl.BlockSpec(meta.shape, lambda i: (0, 0))],
        out_shape=[jax.ShapeDtypeStruct((b, D_MODEL), F32),
                   jax.ShapeDtypeStruct(prev.shape, F32),
                   jax.ShapeDtypeStruct(meta.shape, F32)],
        compiler_params=_params(1),
        name="pool_sample",
    )(x, prev, *consts)


def _small_ffn_kernel(hs_ref, hm_ref, nw_ref, wg_ref, wu_ref, wd_ref, o_ref, wg_bf, wu_bf, wd_bf, hn):
    @pl.when(pl.program_id(0) == 0)
    def _():
        h = jnp.concatenate([hs_ref[...], hm_ref[...]], axis=0)
        hn[...] = _rms(h, nw_ref[...]).astype(BF)
        o_ref[...] = h

    wg = wg_ref[...].astype(BF)
    wu = wu_ref[...].astype(BF)
    wd = wd_ref[...].astype(BF)
    wg_bf[...] = wg
    wu_bf[...] = wu
    wd_bf[...] = wd
    act = (_silu(_dot(hn[...], wg)) * _dot(hn[...], wu)).astype(BF)
    o_ref[...] += _dot(act, wd)


def _small_ffn(hs, hm, nw, ffn_w32, layer):
    rows = hs.shape[0] + hm.shape[0]
    whole = pl.BlockSpec((rows, D_MODEL), lambda c: (0, 0))
    out = pl.pallas_call(
        _small_ffn_kernel,
        grid=(D_FF // FF_STREAM,),
        in_specs=[_const_spec(hs.shape), _const_spec(hm.shape), _const_spec(nw.shape),
                  pl.BlockSpec((None, D_MODEL, FF_STREAM), lambda c: (layer, 0, c)),
                  pl.BlockSpec((None, D_MODEL, FF_STREAM), lambda c: (layer, 0, c)),
                  pl.BlockSpec((None, FF_STREAM, D_MODEL), lambda c: (layer, c, 0))],
        out_specs=[whole,
                   pl.BlockSpec((D_MODEL, FF_STREAM), lambda c: (0, c)),
                   pl.BlockSpec((D_MODEL, FF_STREAM), lambda c: (0, c)),
                   pl.BlockSpec((FF_STREAM, D_MODEL), lambda c: (c, 0))],
        out_shape=[jax.ShapeDtypeStruct((rows, D_MODEL), F32),
                   jax.ShapeDtypeStruct((D_MODEL, D_FF), BF), jax.ShapeDtypeStruct((D_MODEL, D_FF), BF),
                   jax.ShapeDtypeStruct((D_FF, D_MODEL), BF)],
        scratch_shapes=[pltpu.VMEM((rows, D_MODEL), BF)],
        compiler_params=_params(1),
        name="small_ffn",
    )(hs, hm, nw, *ffn_w32)
    return out[0], tuple(out[1:])


def _small_proj_kernel(h_ref, nw_ref, w_ref, cos_ref, sin_ref, kdec_ref,
                       p_ref, kd_ref, qt_ref, kt_ref, v_ref, w_bf, hn):
    c = pl.program_id(0)
    bs = qt_ref.shape[1]

    @pl.when(c == 0)
    def _():
        hn[...] = _rms(h_ref[...], nw_ref[...]).astype(BF)

    w = w_ref[...].astype(BF)
    w_bf[...] = w
    y = _dot(hn[...], w)

    def rotated():
        return _rotary(y, jnp.concatenate([cos_ref[...]] * RET_HEADS, axis=1),
                       jnp.concatenate([sin_ref[...]] * RET_HEADS, axis=1))

    @pl.when(c == 0)
    def _():
        q = (rotated() * (RET_KDIM ** -0.5)).astype(BF)
        p_ref[...] = q
        qt_ref[...] = q[:bs].astype(F32).T

    @pl.when(c == 1)
    def _():
        k = rotated()
        p_ref[...] = k.astype(BF)
        kt_ref[...] = k[:bs].astype(BF).astype(F32).T
        kd_ref[...] = (k * kdec_ref[...]).astype(BF)

    @pl.when((c == 2) | (c == 3))
    def _():
        v = y.astype(BF)
        p_ref[...] = v
        v_ref[...] = v[:bs].astype(F32)

    @pl.when(c >= 4)
    def _():
        p_ref[...] = y.astype(BF)


def _small_proj(h, nw, w_in32, cos, sin, kdec, bs):
    rows = h.shape[0]
    whole = lambda r, w: pl.BlockSpec((r, w), lambda c: (0, 0))
    v_chunk = lambda c: jnp.clip(c - 2, 0, 1)
    return pl.pallas_call(
        _small_proj_kernel,
        grid=(6,),
        in_specs=[whole(rows, D_MODEL), _const_spec(nw.shape),
                  pl.BlockSpec((None, D_MODEL, D_MODEL), lambda c: (0, 0, c)),
                  _const_spec(cos.shape), _const_spec(sin.shape), _const_spec(kdec.shape)],
        out_specs=[pl.BlockSpec((rows, D_MODEL), lambda c: (0, c)),
                   whole(rows, D_MODEL), whole(D_MODEL, bs), whole(D_MODEL, bs),
                   pl.BlockSpec((bs, D_MODEL), lambda c: (0, v_chunk(c))),
                   pl.BlockSpec((D_MODEL, D_MODEL), lambda c: (0, c))],
        out_shape=[jax.ShapeDtypeStruct((rows, 6 * D_MODEL), BF),
                   jax.ShapeDtypeStruct((rows, D_MODEL), BF),
                   jax.ShapeDtypeStruct((D_MODEL, bs), F32),
                   jax.ShapeDtypeStruct((D_MODEL, bs), F32),
                   jax.ShapeDtypeStruct((bs, 2 * D_MODEL), F32),
                   jax.ShapeDtypeStruct((D_MODEL, 6 * D_MODEL), BF)],
        scratch_shapes=[pltpu.VMEM((rows, D_MODEL), BF)],
        compiler_params=_params(1),
        name="small_proj",
    )(h, nw, w_in32, cos, sin, kdec)


STREAM_TB = 4


def _head_state_update(row, s, h, lg_ref, qt, kt, v_ref, gw_ref, gb_ref, slots, on_ref):
    gamma = jnp.exp(jnp.full((1, 1), 1.0, F32) * lg_ref[h])
    kc = _head_cols(h, RET_KDIM)
    vc = _head_cols(h, RET_VDIM)
    q = qt[kc, s:s + 1]
    k = kt[kc, s:s + 1]
    v = v_ref[pl.ds(row, 1), vc]
    s_prev = slots[s, h]
    qs = jnp.sum(q * s_prev, axis=0, keepdims=True)
    score = jnp.sum(q * k, axis=0, keepdims=True)
    o = score * v + qs * gamma
    slots[s, h] = gamma * s_prev + k * v
    on_ref[pl.ds(row, 1), vc] = _group_norm(o, gw_ref[:, vc], gb_ref[:, vc])


def _layer0_stream_kernel(tiles_per_seq, lg_ref, x_ref, meta_ref, nm_ref, wp_ref, sc_ref, bc_ref, bh_ref,
                          iw_ref, nw_ref, wg_ref, wu_ref, wd_ref, qt_ref, kt_ref, v_ref, gw_ref, gb_ref,
                          s_hbm, o_ref, tail_ref, on_ref, so_hbm, halo, slots, in_sem, out_sem):
    i = pl.program_id(0)
    n = pl.num_programs(0)
    last_slot = STREAM_TB - 1

    @pl.when(i % tiles_per_seq == 0)
    def _():
        halo[...] = _rms(meta_ref[...], nm_ref[...]).astype(BF)

    def in_copy(step, s):
        return pltpu.make_async_copy(s_hbm.at[0, step * STREAM_TB + s], slots.at[s], in_sem.at[s])

    def out_copy(step, s):
        return pltpu.make_async_copy(slots.at[s], so_hbm.at[0, step * STREAM_TB + s], out_sem.at[s])

    @pl.when(i == 0)
    def _():
        for s in range(STREAM_TB):
            in_copy(0, s).start()

    @pl.when(i > 0)
    def _():
        out_copy(i - 1, last_slot).wait()
        in_copy(i, last_slot).start()

    lanes = qt_ref.shape[1]
    shift = (lanes - i * STREAM_TB) % lanes
    qt = pltpu.roll(qt_ref[...], shift, 1)
    kt = pltpu.roll(kt_ref[...], shift, 1)

    h = _pool_prompt_tile(x_ref, halo, nm_ref, wp_ref, sc_ref, bc_ref, bh_ref, iw_ref, tail_ref)
    hn = _rms(h, nw_ref[...]).astype(BF)

    ffn = {"acc": h, "act": None}

    def up_unit(cols):
        c0, c1 = cols
        gt = _dot(hn, wg_ref[:, c0:c1])
        up = _dot(hn, wu_ref[:, c0:c1])
        ffn["act"] = (_silu(gt) * up).astype(BF)

    def down_unit(cols):
        c0, c1 = cols
        ffn["acc"] = ffn["acc"] + _dot(ffn["act"], wd_ref[c0:c1, :])

    units = [functools.partial(u, cols) for cols in FF_CHUNKS for u in (up_unit, down_unit)]
    per_sample = np.array_split(np.arange(len(units)), STREAM_TB)
    for s in range(STREAM_TB):
        in_copy(i, s).wait()
        mine = list(per_sample[s])
        for hd in range(RET_HEADS):
            _head_state_update(i * STREAM_TB + s, s, hd, lg_ref, qt, kt, v_ref, gw_ref, gb_ref,
                               slots, on_ref)
            if hd == RET_HEADS - 1:
                out_copy(i, s).start()
                if s > 0:
                    out_copy(i, s - 1).wait()

                    @pl.when(i + 1 < n)
                    def _():
                        in_copy(i + 1, s - 1).start()
            take = len(mine) - (len(mine) * (RET_HEADS - 1 - hd)) // RET_HEADS
            for u in mine[:take]:
                units[u]()
            mine = mine[take:]
    o_ref[...] = ffn["acc"]

    @pl.when(i == n - 1)
    def _():
        out_copy(i, last_slot).wait()


def _layer0_stream(x, meta, nm, wp, scale, nw, ffn_w, tm, log_g, qt, kt, v_s, gn_w, gn_b, s0):
    b, t, _ = x.shape
    rows = b * t
    n = rows // tm
    assert v_s.shape[0] == n * STREAM_TB and t % tm == 0
    tiles_per_seq = t // tm
    band_cur, band_halo, _, _, inv_w = _band_matrices()
    row_spec = pl.BlockSpec((tm, D_MODEL), lambda i, lg: (i, 0))
    tail_spec = pl.BlockSpec((1, N_META, D_MODEL), lambda i, lg: (i // tiles_per_seq, 0, 0))
    on_spec = pl.BlockSpec(v_s.shape, lambda i, lg: (0, 0))
    hbm = pl.BlockSpec(memory_space=pl.ANY)
    blk = (STREAM_TB, RET_HEADS, RET_KDIM, RET_VDIM)
    consts = (meta, nm, wp, scale, band_cur, band_halo, inv_w, nw) + tuple(ffn_w) + (qt, kt, v_s, gn_w, gn_b)
    return pl.pallas_call(
        functools.partial(_layer0_stream_kernel, tiles_per_seq),
        grid_spec=pltpu.PrefetchScalarGridSpec(
            num_scalar_prefetch=1,
            grid=(n,),
            in_specs=[row_spec] + [_const_spec(c.shape) for c in consts] + [hbm],
            out_specs=[row_spec, tail_spec, on_spec, hbm],
            scratch_shapes=[pltpu.VMEM((N_META, D_MODEL), BF), pltpu.VMEM(blk, F32),
                            pltpu.SemaphoreType.DMA((STREAM_TB,)), pltpu.SemaphoreType.DMA((STREAM_TB,))]),
        out_shape=[jax.ShapeDtypeStruct((rows, D_MODEL), F32),
                   jax.ShapeDtypeStruct((b, N_META, D_MODEL), F32),
                   jax.ShapeDtypeStruct(v_s.shape, F32),
                   jax.ShapeDtypeStruct(s0.shape, F32)],
        compiler_params=_params(1),
        name="layer0_stream",
    )(log_g, x.reshape(rows, D_MODEL), *consts, s0)


def _cast_row_blocks(rows, n):
    for hold in (1, 2, 4, 8):
        blocks = n // hold
        if n % hold == 0 and rows % blocks == 0 and (rows // blocks) % 16 == 0:
            return rows // blocks, hold
    raise ValueError((rows, n))


def _cast_specs(w32, layer, n):
    rows, cols = w32.shape[1:]
    blk, hold = _cast_row_blocks(rows, n)
    src = pl.BlockSpec((None, blk, cols), lambda i: (layer, i // hold, 0))
    dst = pl.BlockSpec((blk, cols), lambda i: (i // hold, 0))
    return src, dst, jax.ShapeDtypeStruct((rows, cols), BF)


def _proj_kernel(h_ref, nw_ref, w_ref, cos_ref, sin_ref, cross_ref, kdec_ref,
                 c0_ref, c1_ref, c2_ref, c3_ref,
                 q_ref, qx_ref, k_ref, kd_ref, v_ref, g_ref, d0_ref, d1_ref, d2_ref, d3_ref):
    for src, dst in ((c0_ref, d0_ref), (c1_ref, d1_ref), (c2_ref, d2_ref), (c3_ref, d3_ref)):
        dst[...] = src[...].astype(BF)

    hn = _rms(h_ref[...], nw_ref[...]).astype(BF)
    cos = jnp.concatenate([cos_ref[...]] * RET_HEADS, axis=1)
    sin = jnp.concatenate([sin_ref[...]] * RET_HEADS, axis=1)

    q = _rotary(_dot(hn, w_ref[:, 0:D_MODEL]), cos, sin) * (RET_KDIM ** -0.5)
    q_ref[...] = q.astype(BF)
    qx_ref[...] = (q * cross_ref[...]).astype(BF)
    k = _rotary(_dot(hn, w_ref[:, D_MODEL:2 * D_MODEL]), cos, sin)
    k_ref[...] = k.astype(BF)
    kd_ref[...] = (k * kdec_ref[...]).astype(BF)
    v_ref[...] = _dot(hn, w_ref[:, 2 * D_MODEL:4 * D_MODEL]).astype(BF)
    g_ref[...] = _dot(hn, w_ref[:, 4 * D_MODEL:6 * D_MODEL]).astype(BF)


def _proj(h, nw, w_in, cos, sin, cross, kdec, tm, cast):
    rows = h.shape[0]
    n = rows // tm
    ntab = cos.shape[0] // tm
    row_spec = lambda width: pl.BlockSpec((tm, width), lambda i: (i, 0))
    tab_spec = pl.BlockSpec((tm, RET_KDIM), lambda i: (i % ntab, 0))
    widths = (D_MODEL, D_MODEL, D_MODEL, D_MODEL, 2 * D_MODEL, 2 * D_MODEL)
    cast_src, cast_dst, cast_shape = zip(*[_cast_specs(w32, layer, n) for w32, layer in cast])
    out = pl.pallas_call(
        _proj_kernel,
        grid=(n,),
        in_specs=[row_spec(D_MODEL), _const_spec(nw.shape), _const_spec(w_in.shape),
                  tab_spec, tab_spec, _const_spec(cross.shape), _const_spec(kdec.shape)] + list(cast_src),
        out_specs=[row_spec(w) for w in widths] + list(cast_dst),
        out_shape=[jax.ShapeDtypeStruct((rows, w), BF) for w in widths] + list(cast_shape),
        compiler_params=_params(1),
        name="ret_proj",
    )(h, nw, w_in, cos, sin, cross, kdec, *[w32 for w32, _ in cast])
    return tuple(out[:len(widths)]), tuple(out[len(widths):])


def _ret_prompt_kernel(tot_ref, kdm_ref, vm_ref, dm_ref, q_ref, qx_ref, k_ref, kd_ref, v_ref,
                       gw_ref, gb_ref, o_ref, s_ref, state):
    t = pl.program_id(1)

    @pl.when(t == 0)
    def _():
        for h in range(RET_HEADS):
            state[h] = _kt_v(kdm_ref[:, _head_cols(h, RET_KDIM)], vm_ref[:, _head_cols(h, RET_VDIM)])

    for c in range(q_ref.shape[1] // SUB):
        rows = slice(c * SUB, (c + 1) * SUB)
        for h in range(RET_HEADS):
            kc = _head_cols(h, RET_KDIM)
            vc = _head_cols(h, RET_VDIM)
            q = q_ref[0, rows, kc]
            v = v_ref[0, rows, vc]
            s_prev = state[h]
            scores = lax.dot_general(q, k_ref[0, rows, kc], (((1,), (1,)), ((), ())),
                                     preferred_element_type=F32) * dm_ref[h]
            o = _dot(scores.astype(BF), v) + _dot(qx_ref[0, rows, kc], s_prev.astype(BF))
            state[h] = tot_ref[h] * s_prev + _kt_v(kd_ref[0, rows, kc], v)
            o_ref[0, rows, vc] = _group_norm(o, gw_ref[:, vc], gb_ref[:, vc]).astype(BF)

    @pl.when(t == pl.num_programs(1) - 1)
    def _():
        s_ref[0] = state[...]


def _ret_prompt(chunk_decay, kd_meta, v_meta, dmask, q, qx, k, kd, v, gn_w, gn_b, tt=4 * SUB):
    b, t, _ = q.shape
    qk_spec = pl.BlockSpec((1, tt, D_MODEL), lambda i, j, s: (i, j, 0))
    v_spec = pl.BlockSpec((1, tt, 2 * D_MODEL), lambda i, j, s: (i, j, 0))
    st_shape = (RET_HEADS, RET_KDIM, RET_VDIM)
    return pl.pallas_call(
        _ret_prompt_kernel,
        grid_spec=pltpu.PrefetchScalarGridSpec(
            num_scalar_prefetch=1,
            grid=(b, t // tt),
            in_specs=[_const_spec(kd_meta.shape), _const_spec(v_meta.shape), _const_spec(dmask.shape),
                      qk_spec, qk_spec, qk_spec, qk_spec, v_spec,
                      _const_spec(gn_w.shape), _const_spec(gn_b.shape)],
            out_specs=[v_spec,
                       pl.BlockSpec((None, 1) + st_shape, lambda i, j, s: (0, i, 0, 0, 0))],
            scratch_shapes=[pltpu.VMEM(st_shape, F32)]),
        out_shape=[jax.ShapeDtypeStruct((b, t, 2 * D_MODEL), BF),
                   jax.ShapeDtypeStruct((1, b) + st_shape, F32)],
        compiler_params=_params(2),
        name="ret_prompt",
    )(chunk_decay, kd_meta, v_meta, dmask, q, qx, k, kd, v, gn_w, gn_b)


def _out_ffn_rows(h, on, g, wo_ref, nw_ref, wg_ref, wu_ref, wd_ref, nf_ref):
    gated = (_silu(g.astype(F32)) * on.astype(F32)).astype(BF)
    h = h + _dot(gated, wo_ref[...])
    hn = _rms(h, nw_ref[...]).astype(BF)
    for c0, c1 in FF_CHUNKS:
        act = (_silu(_dot(hn, wg_ref[:, c0:c1])) * _dot(hn, wu_ref[:, c0:c1])).astype(BF)
        h = h + _dot(act, wd_ref[c0:c1, :])
    return _rms(h, nf_ref[...])


def _out_ffn_kernel(h_ref, on_ref, g_ref, hs_ref, ons_ref, gs_ref, wo_ref, nw_ref, wg_ref, wu_ref,
                    wd_ref, nf_ref, o_ref, os_ref):
    weights = (wo_ref, nw_ref, wg_ref, wu_ref, wd_ref, nf_ref)
    last = pl.program_id(0) == pl.num_programs(0) - 1

    @pl.when(jnp.logical_not(last))
    def _():
        o_ref[...] = _out_ffn_rows(h_ref[...], on_ref[...], g_ref[...], *weights)

    @pl.when(last)
    def _():
        os_ref[...] = _out_ffn_rows(hs_ref[...], ons_ref[...], gs_ref[...], *weights)


def _out_ffn(h, on, g, h_small, on_s, proj_s, wo, nw, ffn_w, nf, tm):
    rows = h.shape[0]
    n = rows // tm
    rows_s = on_s.shape[0]
    gate_block = (proj_s.shape[1] - 2 * D_MODEL) // (2 * D_MODEL)
    row_spec = lambda width: pl.BlockSpec((tm, width), lambda i: (jnp.minimum(i, n - 1), 0))
    small_spec = lambda width, blk=0: pl.BlockSpec((rows_s, width), lambda i: (0, blk))
    consts = (wo, nw) + tuple(ffn_w) + (nf,)
    return pl.pallas_call(
        _out_ffn_kernel,
        grid=(n + 1,),
        in_specs=[row_spec(D_MODEL), row_spec(2 * D_MODEL), row_spec(2 * D_MODEL),
                  small_spec(D_MODEL), small_spec(2 * D_MODEL), small_spec(2 * D_MODEL, gate_block)]
                 + [_const_spec(c.shape) for c in consts],
        out_specs=[row_spec(D_MODEL), small_spec(D_MODEL)],
        out_shape=[jax.ShapeDtypeStruct((rows, D_MODEL), F32),
                   jax.ShapeDtypeStruct((rows_s, D_MODEL), F32)],
        compiler_params=_params(1),
        name="out_ffn",
    )(h, on, g, h_small, on_s, proj_s, *consts)


def _rotary_tables(pos):
    theta = 1.0 / (ROPE_BASE ** jnp.linspace(0.0, 1.0, RET_KDIM // 2, dtype=F32))
    ang = pos.astype(F32)[:, None] * jnp.repeat(theta, 2)[None, :]
    sign = jnp.asarray(np.tile([-1.0, 1.0], RET_KDIM // 2), F32)
    return jnp.cos(ang), jnp.sin(ang) * sign[None, :]


def _per_head_cols(table):
    return jnp.asarray(np.repeat(table, RET_KDIM, axis=1), F32)


def _decay_to_end(c):
    return np.exp((c - 1.0 - np.arange(c))[:, None] * LOG_GAMMA[None, :])


def _decay_from_start(c):
    return np.exp((np.arange(c) + 1.0)[:, None] * LOG_GAMMA[None, :])


def _decay_mask(c):
    rel = np.arange(c)[:, None] - np.arange(c)[None, :]
    return np.where(rel[None] >= 0, np.exp(np.maximum(rel, 0)[None] * LOG_GAMMA[:, None, None]), 0.0)


def kernel(x_prompt, x_sample, state_pool, state_ret, meta_tokens, norm_mix, norm_ffn, norm_final,
           w_pool, pool_scale, w_ret_in, ret_gn_w, ret_gn_b, w_ret_out, w_ffn_gate, w_ffn_up, w_ffn_down):
    b, t, _ = x_prompt.shape
    bs = x_sample.shape[0]
    log_g = jnp.asarray(LOG_GAMMA, F32)

    wp = w_pool[0].astype(BF)
    ffn_w32 = (w_ffn_gate, w_ffn_up, w_ffn_down)
    nm0, nm1 = norm_mix[0:1], norm_mix[1:2]
    nf0, nf1 = norm_ffn[0:1], norm_ffn[1:2]
    nfin = norm_final[None, :]
    scale = pool_scale[0:1]
    gn_w, gn_b = ret_gn_w[0:1], ret_gn_b[0:1]

    hs, new_pool_sample, h_meta = _pool_sample(
        x_sample, state_pool.transpose(0, 2, 1, 3), meta_tokens, nm0, wp, scale)
    new_pool_sample = new_pool_sample.transpose(0, 2, 1, 3)
    h_small, ffn_w0 = _small_ffn(hs, h_meta, nf0, ffn_w32, 0)

    pos_small = np.concatenate([np.full((bs,), PAST_LEN), np.arange(N_META)]).astype(np.int32)
    cos_s, sin_s = _rotary_tables(jnp.asarray(pos_small))
    kdec_s = _per_head_cols(np.concatenate([np.ones((bs, RET_HEADS)), _decay_to_end(N_META)], axis=0))
    proj_s, kd_s, qt_s, kt_s, v_s, w_in = _small_proj(h_small, nm1, w_ret_in, cos_s, sin_s, kdec_s, bs)

    h, tail, on_s, new_ret_sample = _layer0_stream(
        x_prompt, meta_tokens, nm0, wp, scale, nf0, ffn_w0, TM, log_g, qt_s, kt_s, v_s, gn_w, gn_b, state_ret)

    cos, sin = _rotary_tables(N_META + jnp.arange(t, dtype=jnp.int32))
    reps = (TM // SUB, 1)
    (q, qx, k, kd, v, g), (w_out, *ffn_w1) = _proj(
        h, nm1, w_in, cos, sin,
        _per_head_cols(np.tile(_decay_from_start(SUB), reps)),
        _per_head_cols(np.tile(_decay_to_end(SUB), reps)), TM,
        cast=((w_ret_out, 0), (w_ffn_gate, 1), (w_ffn_up, 1), (w_ffn_down, 1)))
    seq = lambda a: a.reshape(b, t, -1)
    on, new_ret_prompt = _ret_prompt(jnp.asarray(np.exp(SUB * LOG_GAMMA), F32), kd_s[bs:],
                                     proj_s[bs:, 2 * D_MODEL:4 * D_MODEL],
                                     jnp.asarray(_decay_mask(SUB), F32),
                                     seq(q), seq(qx), seq(k), seq(kd), seq(v), gn_w, gn_b)
    y, y_sample = _out_ffn(h, on.reshape(b * t, -1), g, h_small, on_s, proj_s, w_out, nf1, ffn_w1, nfin, TM)

    return (y.reshape(b, t, D_MODEL), y_sample.reshape(bs, 1, D_MODEL), tail[:, 1:][None],
            new_pool_sample, new_ret_prompt, new_ret_sample)
```

```python
import functools

import numpy as np
import jax
import jax.numpy as jnp
from jax import lax
from jax.experimental import pallas as pl
from jax.experimental.pallas import tpu as pltpu

D_MODEL = 1024
N_META = 16
PAST_LEN = 16384
POOL_WINDOWS = (2, 4, 8, 16)
POOL_GROUPS = len(POOL_WINDOWS)
POOL_GROUP_DIM = D_MODEL // POOL_GROUPS
POOL_BUF = max(POOL_WINDOWS) - 1
RET_HEADS = 4
RET_KDIM = D_MODEL // RET_HEADS
RET_VDIM = 2 * D_MODEL // RET_HEADS
ROPE_BASE = 10000.0
D_FF = 2816
EPS = 1e-6

BF = jnp.bfloat16
F32 = jnp.float32

SUB = 256
TM = 512
FF_CHUNKS = tuple((c, min(c + 512, D_FF)) for c in range(0, D_FF, 512))
FF_STREAM = 256
assert D_FF % FF_STREAM == 0
POOL_SAMPLE_TB = 32
BF16_TILE_ROWS = 16
V7X_VMEM_BYTES = 64 * 1024 * 1024
VMEM_LIMIT = V7X_VMEM_BYTES * 7 // 8

LOG_GAMMA = np.log(1.0 - 2.0 ** (-5.0 - np.arange(RET_HEADS)))


def _dot(a, b):
    return jnp.dot(a, b, preferred_element_type=F32)


def _kt_v(k, v):
    return lax.dot_general(k, v, (((0,), (0,)), ((), ())), preferred_element_type=F32)


def _rms(x, g):
    ms = jnp.mean(x * x, axis=-1, keepdims=True)
    return (x * lax.rsqrt(ms + EPS)) * g


def _silu(x):
    return x * jax.nn.sigmoid(x)


def _group_norm(o, gn_w, gn_b):
    mu = jnp.mean(o, axis=-1, keepdims=True)
    cen = o - mu
    var = jnp.mean(cen * cen, axis=-1, keepdims=True)
    return (cen * lax.rsqrt(var + EPS)) * gn_w + gn_b


def _rotary(x, cos, sin):
    lane = lax.broadcasted_iota(jnp.int32, x.shape, 1)
    partner = jnp.where((lane & 1) == 0, pltpu.roll(x, x.shape[1] - 1, 1), pltpu.roll(x, 1, 1))
    return x * cos + partner * sin


def _head_cols(h, width):
    return slice(h * width, (h + 1) * width)


def _group_cols(g):
    return slice(g * POOL_GROUP_DIM, (g + 1) * POOL_GROUP_DIM)


def _const_spec(shape):
    nd = len(shape)
    return pl.BlockSpec(shape, lambda *_: (0,) * nd, pipeline_mode=pl.Buffered(1))


def _params(n_grid):
    return pltpu.CompilerParams(
        dimension_semantics=("arbitrary",) * n_grid, vmem_limit_bytes=VMEM_LIMIT)


def _band_matrices():
    t = np.arange(SUB)[:, None]
    s = np.arange(SUB)[None, :]
    sh = np.arange(N_META)[None, :] - N_META
    cur = np.stack([((t - s >= 0) & (t - s < w)) for w in POOL_WINDOWS]).astype(np.float32)
    halo = np.stack([(t - sh < w) for w in POOL_WINDOWS]).astype(np.float32)
    tm = np.arange(N_META)[:, None]
    sm = np.arange(N_META)[None, :]
    meta = np.stack([((tm - sm >= 0) & (tm - sm < w)) for w in POOL_WINDOWS]).astype(np.float32)
    inv_meta = np.concatenate(
        [np.repeat(1.0 / np.minimum(w, tm + 1.0), POOL_GROUP_DIM, axis=1) for w in POOL_WINDOWS], axis=1)
    inv_w = np.concatenate([np.full((1, POOL_GROUP_DIM), 1.0 / w) for w in POOL_WINDOWS], axis=1)
    return (jnp.asarray(cur, BF), jnp.asarray(halo, BF), jnp.asarray(meta, BF),
            jnp.asarray(inv_meta, F32), jnp.asarray(inv_w, F32))


def _pool_mix(pooled_sum, inv_cnt, xn, wp_ref, scale):
    diff = (pooled_sum * inv_cnt - xn).astype(BF)
    mixed = jnp.concatenate(
        [_dot(diff[:, _group_cols(g)], wp_ref[g]) for g in range(POOL_GROUPS)], axis=1)
    return mixed * scale


def _pool_prompt_tile(x_ref, halo_ref, nw_ref, wp_ref, sc_ref, bc_ref, bh_ref, iw_ref, tail_ref):
    nw = nw_ref[...]
    scale = sc_ref[...]
    inv_w = iw_ref[...]
    halo = halo_ref[...]
    out = []
    for j in range(x_ref.shape[0] // SUB):
        x = x_ref[SUB * j:SUB * (j + 1), :]
        xn = _rms(x, nw)
        xb = xn.astype(BF)
        pooled = jnp.concatenate(
            [_dot(bc_ref[g], xb[:, _group_cols(g)]) + _dot(bh_ref[g], halo[:, _group_cols(g)])
             for g in range(POOL_GROUPS)], axis=1)
        out.append(x + _pool_mix(pooled, inv_w, xn, wp_ref, scale))
        halo = xb[SUB - N_META:, :]
    halo_ref[...] = halo
    tail_ref[0] = xn[SUB - N_META:, :]
    return jnp.concatenate(out, axis=0)


def _pool_sample_kernel(x_ref, prev_ref, meta_ref, nw_ref, wp_ref, sc_ref, iw_ref, bm_ref, icm_ref,
                        h_ref, np_ref, hm_ref):
    @pl.when(pl.program_id(0) == 0)
    def _():
        xm = meta_ref[...]
        xnm = _rms(xm, nw_ref[...])
        xbm = xnm.astype(BF)
        pooled_m = jnp.concatenate(
            [_dot(bm_ref[g], xbm[:, _group_cols(g)]) for g in range(POOL_GROUPS)], axis=1)
        hm_ref[...] = xm + _pool_mix(pooled_m, icm_ref[...], xnm, wp_ref, sc_ref[...])

    x = x_ref[:, 0, :]
    xn = _rms(x, nw_ref[...])
    sums = []
    for g, w in enumerate(POOL_WINDOWS):
        s = xn[:, _group_cols(g)]
        for j in range(1, w):
            s = s + prev_ref[POOL_BUF - j, :, _group_cols(g)]
        sums.append(s)
    pooled = jnp.concatenate(sums, axis=1)
    h_ref[...] = x + _pool_mix(pooled, iw_ref[...], xn, wp_ref, sc_ref[...])
    for r in range(POOL_BUF - 1):
        np_ref[r] = prev_ref[r + 1]
    np_ref[POOL_BUF - 1] = xn


def _pool_sample(x, prev, meta, nw, wp, scale, tb=POOL_SAMPLE_TB):
    b = x.shape[0]
    _, _, band_meta, inv_meta, inv_w = _band_matrices()
    consts = (meta, nw, wp, scale, inv_w, band_meta, inv_meta)
    st_spec = pl.BlockSpec((None, POOL_BUF, tb, D_MODEL), lambda i: (0, 0, i, 0))
    return pl.pallas_call(
        _pool_sample_kernel,
        grid=(b // tb,),
        in_specs=[pl.BlockSpec((tb, 1, D_MODEL), lambda i: (i, 0, 0)), st_spec]
                 + [_const_spec(c.shape) for c in consts],
        out_specs=[pl.BlockSpec((tb, D_MODEL), lambda i: (i, 0)), st_spec,
                   pl.BlockSpec(meta.shape, lambda i: (0, 0))],
        out_shape=[jax.ShapeDtypeStruct((b, D_MODEL), F32),
                   jax.ShapeDtypeStruct(prev.shape, F32),
                   jax.ShapeDtypeStruct(meta.shape, F32)],
        compiler_params=_params(1),
        name="pool_sample",
    )(x, prev, *consts)


def _small_ffn_kernel(hs_ref, hm_ref, nw_ref, wg_ref, wu_ref, wd_ref, o_ref, wg_bf, wu_bf, wd_bf, hn):
    @pl.when(pl.program_id(0) == 0)
    def _():
        h = jnp.concatenate([hs_ref[...], hm_ref[...]], axis=0)
        hn[...] = _rms(h, nw_ref[...]).astype(BF)
        o_ref[...] = h

    wg = wg_ref[...].astype(BF)
    wu = wu_ref[...].astype(BF)
    wd = wd_ref[...].astype(BF)
    wg_bf[...] = wg
    wu_bf[...] = wu
    wd_bf[...] = wd
    act = (_silu(_dot(hn[...], wg)) * _dot(hn[...], wu)).astype(BF)
    o_ref[...] += _dot(act, wd)


def _small_ffn(hs, hm, nw, ffn_w32, layer):
    rows = hs.shape[0] + hm.shape[0]
    whole = pl.BlockSpec((rows, D_MODEL), lambda c: (0, 0))
    out = pl.pallas_call(
        _small_ffn_kernel,
        grid=(D_FF // FF_STREAM,),
        in_specs=[_const_spec(hs.shape), _const_spec(hm.shape), _const_spec(nw.shape),
                  pl.BlockSpec((None, D_MODEL, FF_STREAM), lambda c: (layer, 0, c)),
                  pl.BlockSpec((None, D_MODEL, FF_STREAM), lambda c: (layer, 0, c)),
                  pl.BlockSpec((None, FF_STREAM, D_MODEL), lambda c: (layer, c, 0))],
        out_specs=[whole,
                   pl.BlockSpec((D_MODEL, FF_STREAM), lambda c: (0, c)),
                   pl.BlockSpec((D_MODEL, FF_STREAM), lambda c: (0, c)),
                   pl.BlockSpec((FF_STREAM, D_MODEL), lambda c: (c, 0))],
        out_shape=[jax.ShapeDtypeStruct((rows, D_MODEL), F32),
                   jax.ShapeDtypeStruct((D_MODEL, D_FF), BF), jax.ShapeDtypeStruct((D_MODEL, D_FF), BF),
                   jax.ShapeDtypeStruct((D_FF, D_MODEL), BF)],
        scratch_shapes=[pltpu.VMEM((rows, D_MODEL), BF)],
        compiler_params=_params(1),
        name="small_ffn",
    )(hs, hm, nw, *ffn_w32)
    return out[0], tuple(out[1:])


def _small_proj_kernel(h_ref, nw_ref, w_ref, cos_ref, sin_ref, kdec_ref,
                       p_ref, kd_ref, qt_ref, kt_ref, v_ref, w_bf, hn):
    c = pl.program_id(0)
    bs = qt_ref.shape[1]

    @pl.when(c == 0)
    def _():
        hn[...] = _rms(h_ref[...], nw_ref[...]).astype(BF)

    w = w_ref[...].astype(BF)
    w_bf[...] = w
    y = _dot(hn[...], w)

    def rotated():
        return _rotary(y, jnp.concatenate([cos_ref[...]] * RET_HEADS, axis=1),
                       jnp.concatenate([sin_ref[...]] * RET_HEADS, axis=1))

    @pl.when(c == 0)
    def _():
        q = (rotated() * (RET_KDIM ** -0.5)).astype(BF)
        p_ref[...] = q
        qt_ref[...] = q[:bs].astype(F32).T

    @pl.when(c == 1)
    def _():
        k = rotated()
        p_ref[...] = k.astype(BF)
        kt_ref[...] = k[:bs].astype(BF).astype(F32).T
        kd_ref[...] = (k * kdec_ref[...]).astype(BF)

    @pl.when((c == 2) | (c == 3))
    def _():
        v = y.astype(BF)
        p_ref[...] = v
        v_ref[...] = v[:bs].astype(F32)

    @pl.when(c >= 4)
    def _():
        p_ref[...] = y.astype(BF)


def _small_proj(h, nw, w_in32, cos, sin, kdec, bs):
    rows = h.shape[0]
    whole = lambda r, w: pl.BlockSpec((r, w), lambda c: (0, 0))
    v_chunk = lambda c: jnp.clip(c - 2, 0, 1)
    return pl.pallas_call(
        _small_proj_kernel,
        grid=(6,),
        in_specs=[whole(rows, D_MODEL), _const_spec(nw.shape),
                  pl.BlockSpec((None, D_MODEL, D_MODEL), lambda c: (0, 0, c)),
                  _const_spec(cos.shape), _const_spec(sin.shape), _const_spec(kdec.shape)],
        out_specs=[pl.BlockSpec((rows, D_MODEL), lambda c: (0, c)),
                   whole(rows, D_MODEL), whole(D_MODEL, bs), whole(D_MODEL, bs),
                   pl.BlockSpec((bs, D_MODEL), lambda c: (0, v_chunk(c))),
                   pl.BlockSpec((D_MODEL, D_MODEL), lambda c: (0, c))],
        out_shape=[jax.ShapeDtypeStruct((rows, 6 * D_MODEL), BF),
                   jax.ShapeDtypeStruct((rows, D_MODEL), BF),
                   jax.ShapeDtypeStruct((D_MODEL, bs), F32),
                   jax.ShapeDtypeStruct((D_MODEL, bs), F32),
                   jax.ShapeDtypeStruct((bs, 2 * D_MODEL), F32),
                   jax.ShapeDtypeStruct((D_MODEL, 6 * D_MODEL), BF)],
        scratch_shapes=[pltpu.VMEM((rows, D_MODEL), BF)],
        compiler_params=_params(1),
        name="small_proj",
    )(h, nw, w_in32, cos, sin, kdec)


STREAM_TB = 4


def _head_state_update(row, s, h, lg_ref, qt, kt, v_ref, gw_ref, gb_ref, s_in, s_out, on_ref):
    gamma = jnp.exp(jnp.full((1, 1), 1.0, F32) * lg_ref[h])
    kc = _head_cols(h, RET_KDIM)
    vc = _head_cols(h, RET_VDIM)
    q = qt[kc, s:s + 1]
    k = kt[kc, s:s + 1]
    v = v_ref[pl.ds(row, 1), vc]
    s_prev = s_in[s, h]
    qs = jnp.sum(q * s_prev, axis=0, keepdims=True)
    score = jnp.sum(q * k, axis=0, keepdims=True)
    o = score * v + qs * gamma
    s_out[s, h] = gamma * s_prev + k * v
    on_ref[pl.ds(row, 1), vc] = _group_norm(o, gw_ref[:, vc], gb_ref[:, vc])


def _layer0_stream_kernel(tiles_per_seq, lg_ref, x_ref, meta_ref, nm_ref, wp_ref, sc_ref, bc_ref, bh_ref,
                          iw_ref, nw_ref, wg_ref, wu_ref, wd_ref, qt_ref, kt_ref, v_ref, gw_ref, gb_ref,
                          s_hbm, o_ref, tail_ref, on_ref, so_hbm, halo, s_in, s_out, in_sem, out_sem):
    i = pl.program_id(0)
    n = pl.num_programs(0)

    @pl.when(i % tiles_per_seq == 0)
    def _():
        halo[...] = _rms(meta_ref[...], nm_ref[...]).astype(BF)

    def in_copy(step, s):
        return pltpu.make_async_copy(s_hbm.at[0, step * STREAM_TB + s], s_in.at[s], in_sem.at[s])

    def out_copy(step, s):
        return pltpu.make_async_copy(s_out.at[s], so_hbm.at[0, step * STREAM_TB + s], out_sem.at[s])

    @pl.when(i == 0)
    def _():
        for s in range(STREAM_TB):
            in_copy(0, s).start()

    lanes = qt_ref.shape[1]
    shift = (lanes - i * STREAM_TB) % lanes
    qt = pltpu.roll(qt_ref[...], shift, 1)
    kt = pltpu.roll(kt_ref[...], shift, 1)

    h = _pool_prompt_tile(x_ref, halo, nm_ref, wp_ref, sc_ref, bc_ref, bh_ref, iw_ref, tail_ref)
    hn = _rms(h, nw_ref[...]).astype(BF)

    ffn = {"acc": h, "act": None}

    def up_unit(cols):
        c0, c1 = cols
        gt = _dot(hn, wg_ref[:, c0:c1])
        up = _dot(hn, wu_ref[:, c0:c1])
        ffn["act"] = (_silu(gt) * up).astype(BF)

    def down_unit(cols):
        c0, c1 = cols
        ffn["acc"] = ffn["acc"] + _dot(ffn["act"], wd_ref[c0:c1, :])

    units = [functools.partial(u, cols) for cols in FF_CHUNKS for u in (up_unit, down_unit)]
    per_sample = np.array_split(np.arange(len(units)), STREAM_TB)
    for s in range(STREAM_TB):
        in_copy(i, s).wait()

        @pl.when(i > 0)
        def _():
            out_copy(i - 1, s).wait()

        mine = list(per_sample[s])
        for hd in range(RET_HEADS):
            _head_state_update(i * STREAM_TB + s, s, hd, lg_ref, qt, kt, v_ref, gw_ref, gb_ref,
                               s_in, s_out, on_ref)
            if hd == RET_HEADS - 1:
                out_copy(i, s).start()

                @pl.when(i + 1 < n)
                def _():
                    in_copy(i + 1, s).start()
            take = len(mine) - (len(mine) * (RET_HEADS - 1 - hd)) // RET_HEADS
            for u in mine[:take]:
                units[u]()
            mine = mine[take:]
    o_ref[...] = ffn["acc"]

    @pl.when(i == n - 1)
    def _():
        for s in range(STREAM_TB):
            out_copy(i, s).wait()


def _layer0_stream(x, meta, nm, wp, scale, nw, ffn_w, tm, log_g, qt, kt, v_s, gn_w, gn_b, s0):
    b, t, _ = x.shape
    rows = b * t
    n = rows // tm
    assert v_s.shape[0] == n * STREAM_TB and t % tm == 0
    tiles_per_seq = t // tm
    band_cur, band_halo, _, _, inv_w = _band_matrices()
    row_spec = pl.BlockSpec((tm, D_MODEL), lambda i, lg: (i, 0))
    tail_spec = pl.BlockSpec((1, N_META, D_MODEL), lambda i, lg: (i // tiles_per_seq, 0, 0))
    on_spec = pl.BlockSpec(v_s.shape, lambda i, lg: (0, 0))
    hbm = pl.BlockSpec(memory_space=pl.ANY)
    blk = (STREAM_TB, RET_HEADS, RET_KDIM, RET_VDIM)
    consts = (meta, nm, wp, scale, band_cur, band_halo, inv_w, nw) + tuple(ffn_w) + (qt, kt, v_s, gn_w, gn_b)
    return pl.pallas_call(
        functools.partial(_layer0_stream_kernel, tiles_per_seq),
        grid_spec=pltpu.PrefetchScalarGridSpec(
            num_scalar_prefetch=1,
            grid=(n,),
            in_specs=[row_spec] + [_const_spec(c.shape) for c in consts] + [hbm],
            out_specs=[row_spec, tail_spec, on_spec, hbm],
            scratch_shapes=[pltpu.VMEM((N_META, D_MODEL), BF),
                            pltpu.VMEM(blk, F32), pltpu.VMEM(blk, F32),
                            pltpu.SemaphoreType.DMA((STREAM_TB,)), pltpu.SemaphoreType.DMA((STREAM_TB,))]),
        out_shape=[jax.ShapeDtypeStruct((rows, D_MODEL), F32),
                   jax.ShapeDtypeStruct((b, N_META, D_MODEL), F32),
                   jax.ShapeDtypeStruct(v_s.shape, F32),
                   jax.ShapeDtypeStruct(s0.shape, F32)],
        compiler_params=_params(1),
        name="layer0_stream",
    )(log_g, x.reshape(rows, D_MODEL), *consts, s0)


def _cast_row_blocks(rows, n):
    for hold in range(1, n + 1):
        blocks = n // hold
        if n % hold == 0 and rows % blocks == 0 and (rows // blocks) % BF16_TILE_ROWS == 0:
            return rows // blocks, hold
    raise ValueError((rows, n))


def _cast_specs(w32, layer, n):
    rows, cols = w32.shape[1:]
    blk, hold = _cast_row_blocks(rows, n)
    src = pl.BlockSpec((None, blk, cols), lambda i: (layer, i // hold, 0))
    dst = pl.BlockSpec((blk, cols), lambda i: (i // hold, 0))
    return src, dst, jax.ShapeDtypeStruct((rows, cols), BF)


def _proj_kernel(h_ref, nw_ref, w_ref, cos_ref, sin_ref, cross_ref, kdec_ref,
                 c0_ref, c1_ref, c2_ref, c3_ref,
                 q_ref, qx_ref, k_ref, kd_ref, v_ref, g_ref, d0_ref, d1_ref, d2_ref, d3_ref):
    for src, dst in ((c0_ref, d0_ref), (c1_ref, d1_ref), (c2_ref, d2_ref), (c3_ref, d3_ref)):
        dst[...] = src[...].astype(BF)

    hn = _rms(h_ref[...], nw_ref[...]).astype(BF)
    cos = jnp.concatenate([cos_ref[...]] * RET_HEADS, axis=1)
    sin = jnp.concatenate([sin_ref[...]] * RET_HEADS, axis=1)

    q = _rotary(_dot(hn, w_ref[:, 0:D_MODEL]), cos, sin) * (RET_KDIM ** -0.5)
    q_ref[...] = q.astype(BF)
    qx_ref[...] = (q * cross_ref[...]).astype(BF)
    k = _rotary(_dot(hn, w_ref[:, D_MODEL:2 * D_MODEL]), cos, sin)
    k_ref[...] = k.astype(BF)
    kd_ref[...] = (k * kdec_ref[...]).astype(BF)
    v_ref[...] = _dot(hn, w_ref[:, 2 * D_MODEL:4 * D_MODEL]).astype(BF)
    g_ref[...] = _dot(hn, w_ref[:, 4 * D_MODEL:6 * D_MODEL]).astype(BF)


def _proj(h, nw, w_in, cos, sin, cross, kdec, tm, cast):
    rows = h.shape[0]
    n = rows // tm
    ntab = cos.shape[0] // tm
    row_spec = lambda width: pl.BlockSpec((tm, width), lambda i: (i, 0))
    tab_spec = pl.BlockSpec((tm, RET_KDIM), lambda i: (i % ntab, 0))
    widths = (D_MODEL, D_MODEL, D_MODEL, D_MODEL, 2 * D_MODEL, 2 * D_MODEL)
    cast_src, cast_dst, cast_shape = zip(*[_cast_specs(w32, layer, n) for w32, layer in cast])
    out = pl.pallas_call(
        _proj_kernel,
        grid=(n,),
        in_specs=[row_spec(D_MODEL), _const_spec(nw.shape), _const_spec(w_in.shape),
                  tab_spec, tab_spec, _const_spec(cross.shape), _const_spec(kdec.shape)] + list(cast_src),
        out_specs=[row_spec(w) for w in widths] + list(cast_dst),
        out_shape=[jax.ShapeDtypeStruct((rows, w), BF) for w in widths] + list(cast_shape),
        compiler_params=_params(1),
        name="ret_proj",
    )(h, nw, w_in, cos, sin, cross, kdec, *[w32 for w32, _ in cast])
    return tuple(out[:len(widths)]), tuple(out[len(widths):])


def _ret_prompt_kernel(tot_ref, kdm_ref, vm_ref, dm_ref, q_ref, qx_ref, k_ref, kd_ref, v_ref,
                       gw_ref, gb_ref, o_ref, s_ref, state):
    t = pl.program_id(1)

    @pl.when(t == 0)
    def _():
        for h in range(RET_HEADS):
            state[h] = _kt_v(kdm_ref[:, _head_cols(h, RET_KDIM)], vm_ref[:, _head_cols(h, RET_VDIM)])

    for c in range(q_ref.shape[1] // SUB):
        rows = slice(c * SUB, (c + 1) * SUB)
        for h in range(RET_HEADS):
            kc = _head_cols(h, RET_KDIM)
            vc = _head_cols(h, RET_VDIM)
            q = q_ref[0, rows, kc]
            v = v_ref[0, rows, vc]
            s_prev = state[h]
            scores = lax.dot_general(q, k_ref[0, rows, kc], (((1,), (1,)), ((), ())),
                                     preferred_element_type=F32) * dm_ref[h]
            o = _dot(scores.astype(BF), v) + _dot(qx_ref[0, rows, kc], s_prev.astype(BF))
            state[h] = tot_ref[h] * s_prev + _kt_v(kd_ref[0, rows, kc], v)
            o_ref[0, rows, vc] = _group_norm(o, gw_ref[:, vc], gb_ref[:, vc]).astype(BF)

    @pl.when(t == pl.num_programs(1) - 1)
    def _():
        s_ref[0] = state[...]


def _ret_prompt(chunk_decay, kd_meta, v_meta, dmask, q, qx, k, kd, v, gn_w, gn_b, tt=4 * SUB):
    b, t, _ = q.shape
    qk_spec = pl.BlockSpec((1, tt, D_MODEL), lambda i, j, s: (i, j, 0))
    v_spec = pl.BlockSpec((1, tt, 2 * D_MODEL), lambda i, j, s: (i, j, 0))
    st_shape = (RET_HEADS, RET_KDIM, RET_VDIM)
    return pl.pallas_call(
        _ret_prompt_kernel,
        grid_spec=pltpu.PrefetchScalarGridSpec(
            num_scalar_prefetch=1,
            grid=(b, t // tt),
            in_specs=[_const_spec(kd_meta.shape), _const_spec(v_meta.shape), _const_spec(dmask.shape),
                      qk_spec, qk_spec, qk_spec, qk_spec, v_spec,
                      _const_spec(gn_w.shape), _const_spec(gn_b.shape)],
            out_specs=[v_spec,
                       pl.BlockSpec((None, 1) + st_shape, lambda i, j, s: (0, i, 0, 0, 0))],
            scratch_shapes=[pltpu.VMEM(st_shape, F32)]),
        out_shape=[jax.ShapeDtypeStruct((b, t, 2 * D_MODEL), BF),
                   jax.ShapeDtypeStruct((1, b) + st_shape, F32)],
        compiler_params=_params(2),
        name="ret_prompt",
    )(chunk_decay, kd_meta, v_meta, dmask, q, qx, k, kd, v, gn_w, gn_b)


def _out_ffn_rows(h, on, g, wo_ref, nw_ref, wg_ref, wu_ref, wd_ref, nf_ref):
    gated = (_silu(g.astype(F32)) * on.astype(F32)).astype(BF)
    h = h + _dot(gated, wo_ref[...])
    hn = _rms(h, nw_ref[...]).astype(BF)
    for c0, c1 in FF_CHUNKS:
        act = (_silu(_dot(hn, wg_ref[:, c0:c1])) * _dot(hn, wu_ref[:, c0:c1])).astype(BF)
        h = h + _dot(act, wd_ref[c0:c1, :])
    return _rms(h, nf_ref[...])


def _out_ffn_kernel(h_ref, on_ref, g_ref, hs_ref, ons_ref, gs_ref, wo_ref, nw_ref, wg_ref, wu_ref,
                    wd_ref, nf_ref, o_ref, os_ref):
    weights = (wo_ref, nw_ref, wg_ref, wu_ref, wd_ref, nf_ref)
    last = pl.program_id(0) == pl.num_programs(0) - 1

    @pl.when(jnp.logical_not(last))
    def _():
        o_ref[...] = _out_ffn_rows(h_ref[...], on_ref[...], g_ref[...], *weights)

    @pl.when(last)
    def _():
        os_ref[...] = _out_ffn_rows(hs_ref[...], ons_ref[...], gs_ref[...], *weights)


def _out_ffn(h, on, g, h_small, on_s, proj_s, wo, nw, ffn_w, nf, tm):
    rows = h.shape[0]
    n = rows // tm
    rows_s = on_s.shape[0]
    gate_block = (proj_s.shape[1] - 2 * D_MODEL) // (2 * D_MODEL)
    row_spec = lambda width: pl.BlockSpec((tm, width), lambda i: (jnp.minimum(i, n - 1), 0))
    small_spec = lambda width, blk=0: pl.BlockSpec((rows_s, width), lambda i: (0, blk))
    consts = (wo, nw) + tuple(ffn_w) + (nf,)
    return pl.pallas_call(
        _out_ffn_kernel,
        grid=(n + 1,),
        in_specs=[row_spec(D_MODEL), row_spec(2 * D_MODEL), row_spec(2 * D_MODEL),
                  small_spec(D_MODEL), small_spec(2 * D_MODEL), small_spec(2 * D_MODEL, gate_block)]
                 + [_const_spec(c.shape) for c in consts],
        out_specs=[row_spec(D_MODEL), small_spec(D_MODEL)],
        out_shape=[jax.ShapeDtypeStruct((rows, D_MODEL), F32),
                   jax.ShapeDtypeStruct((rows_s, D_MODEL), F32)],
        compiler_params=_params(1),
        name="out_ffn",
    )(h, on, g, h_small, on_s, proj_s, *consts)


def _rotary_tables(pos):
    theta = 1.0 / (ROPE_BASE ** jnp.linspace(0.0, 1.0, RET_KDIM // 2, dtype=F32))
    ang = pos.astype(F32)[:, None] * jnp.repeat(theta, 2)[None, :]
    sign = jnp.asarray(np.tile([-1.0, 1.0], RET_KDIM // 2), F32)
    return jnp.cos(ang), jnp.sin(ang) * sign[None, :]


def _per_head_cols(table):
    return jnp.asarray(np.repeat(table, RET_KDIM, axis=1), F32)


def _decay_to_end(c):
    return np.exp((c - 1.0 - np.arange(c))[:, None] * LOG_GAMMA[None, :])


def _decay_from_start(c):
    return np.exp((np.arange(c) + 1.0)[:, None] * LOG_GAMMA[None, :])


def _decay_mask(c):
    rel = np.arange(c)[:, None] - np.arange(c)[None, :]
    return np.where(rel[None] >= 0, np.exp(np.maximum(rel, 0)[None] * LOG_GAMMA[:, None, None]), 0.0)


def kernel(x_prompt, x_sample, state_pool, state_ret, meta_tokens, norm_mix, norm_ffn, norm_final,
           w_pool, pool_scale, w_ret_in, ret_gn_w, ret_gn_b, w_ret_out, w_ffn_gate, w_ffn_up, w_ffn_down):
    b, t, _ = x_prompt.shape
    bs = x_sample.shape[0]
    log_g = jnp.asarray(LOG_GAMMA, F32)

    wp = w_pool[0].astype(BF)
    ffn_w32 = (w_ffn_gate, w_ffn_up, w_ffn_down)
    nm0, nm1 = norm_mix[0:1], norm_mix[1:2]
    nf0, nf1 = norm_ffn[0:1], norm_ffn[1:2]
    nfin = norm_final[None, :]
    scale = pool_scale[0:1]
    gn_w, gn_b = ret_gn_w[0:1], ret_gn_b[0:1]

    hs, new_pool_sample, h_meta = _pool_sample(
        x_sample, state_pool.transpose(0, 2, 1, 3), meta_tokens, nm0, wp, scale)
    new_pool_sample = new_pool_sample.transpose(0, 2, 1, 3)
    h_small, ffn_w0 = _small_ffn(hs, h_meta, nf0, ffn_w32, 0)

    pos_small = np.concatenate([np.full((bs,), PAST_LEN), np.arange(N_META)]).astype(np.int32)
    cos_s, sin_s = _rotary_tables(jnp.asarray(pos_small))
    kdec_s = _per_head_cols(np.concatenate([np.ones((bs, RET_HEADS)), _decay_to_end(N_META)], axis=0))
    proj_s, kd_s, qt_s, kt_s, v_s, w_in = _small_proj(h_small, nm1, w_ret_in, cos_s, sin_s, kdec_s, bs)

    h, tail, on_s, new_ret_sample = _layer0_stream(
        x_prompt, meta_tokens, nm0, wp, scale, nf0, ffn_w0, TM, log_g, qt_s, kt_s, v_s, gn_w, gn_b, state_ret)

    cos, sin = _rotary_tables(N_META + jnp.arange(t, dtype=jnp.int32))
    reps = (TM // SUB, 1)
    (q, qx, k, kd, v, g), (w_out, *ffn_w1) = _proj(
        h, nm1, w_in, cos, sin,
        _per_head_cols(np.tile(_decay_from_start(SUB), reps)),
        _per_head_cols(np.tile(_decay_to_end(SUB), reps)), TM,
        cast=((w_ret_out, 0), (w_ffn_gate, 1), (w_ffn_up, 1), (w_ffn_down, 1)))
    seq = lambda a: a.reshape(b, t, -1)
    on, new_ret_prompt = _ret_prompt(jnp.asarray(np.exp(SUB * LOG_GAMMA), F32), kd_s[bs:],
                                     proj_s[bs:, 2 * D_MODEL:4 * D_MODEL],
                                     jnp.asarray(_decay_mask(SUB), F32),
                                     seq(q), seq(qx), seq(k), seq(kd), seq(v), gn_w, gn_b)
    y, y_sample = _out_ffn(h, on.reshape(b * t, -1), g, h_small, on_s, proj_s, w_out, nf1, ffn_w1, nfin, TM)

    return (y.reshape(b, t, D_MODEL), y_sample.reshape(bs, 1, D_MODEL), tail[:, 1:][None],
            new_pool_sample, new_ret_prompt, new_ret_sample)
```

```python
import functools

import numpy as np
import jax
import jax.numpy as jnp
from jax import lax
from jax.experimental import pallas as pl
from jax.experimental.pallas import tpu as pltpu

D_MODEL = 1024
N_META = 16
PAST_LEN = 16384
POOL_WINDOWS = (2, 4, 8, 16)
POOL_GROUPS = len(POOL_WINDOWS)
POOL_GROUP_DIM = D_MODEL // POOL_GROUPS
POOL_BUF = max(POOL_WINDOWS) - 1
RET_HEADS = 4
RET_KDIM = D_MODEL // RET_HEADS
RET_VDIM = 2 * D_MODEL // RET_HEADS
ROPE_BASE = 10000.0
D_FF = 2816
EPS = 1e-6

BF = jnp.bfloat16
F32 = jnp.float32

SUB = 256
TM = 512
FF_CHUNKS = tuple((c, min(c + 512, D_FF)) for c in range(0, D_FF, 512))
FF_STREAM = 256
assert D_FF % FF_STREAM == 0
POOL_SAMPLE_TB = 32
BF16_TILE_ROWS = 16
V7X_VMEM_BYTES = 64 * 1024 * 1024
VMEM_LIMIT = V7X_VMEM_BYTES * 7 // 8

LOG_GAMMA = np.log(1.0 - 2.0 ** (-5.0 - np.arange(RET_HEADS)))


def _dot(a, b):
    return jnp.dot(a, b, preferred_element_type=F32)


def _kt_v(k, v):
    return lax.dot_general(k, v, (((0,), (0,)), ((), ())), preferred_element_type=F32)


def _rms(x, g):
    ms = jnp.mean(x * x, axis=-1, keepdims=True)
    return (x * lax.rsqrt(ms + EPS)) * g


def _silu(x):
    return x * jax.nn.sigmoid(x)


def _group_norm(o, gn_w, gn_b):
    mu = jnp.mean(o, axis=-1, keepdims=True)
    cen = o - mu
    var = jnp.mean(cen * cen, axis=-1, keepdims=True)
    return (cen * lax.rsqrt(var + EPS)) * gn_w + gn_b


def _rotary(x, cos, sin):
    lane = lax.broadcasted_iota(jnp.int32, x.shape, 1)
    partner = jnp.where((lane & 1) == 0, pltpu.roll(x, x.shape[1] - 1, 1), pltpu.roll(x, 1, 1))
    return x * cos + partner * sin


def _head_cols(h, width):
    return slice(h * width, (h + 1) * width)


def _group_cols(g):
    return slice(g * POOL_GROUP_DIM, (g + 1) * POOL_GROUP_DIM)


def _const_spec(shape):
    nd = len(shape)
    return pl.BlockSpec(shape, lambda *_: (0,) * nd, pipeline_mode=pl.Buffered(1))


def _params(n_grid):
    return pltpu.CompilerParams(
        dimension_semantics=("arbitrary",) * n_grid, vmem_limit_bytes=VMEM_LIMIT)


def _band_matrices():
    t = np.arange(SUB)[:, None]
    s = np.arange(SUB)[None, :]
    sh = np.arange(N_META)[None, :] - N_META
    cur = np.stack([((t - s >= 0) & (t - s < w)) for w in POOL_WINDOWS]).astype(np.float32)
    halo = np.stack([(t - sh < w) for w in POOL_WINDOWS]).astype(np.float32)
    tm = np.arange(N_META)[:, None]
    sm = np.arange(N_META)[None, :]
    meta = np.stack([((tm - sm >= 0) & (tm - sm < w)) for w in POOL_WINDOWS]).astype(np.float32)
    inv_meta = np.concatenate(
        [np.repeat(1.0 / np.minimum(w, tm + 1.0), POOL_GROUP_DIM, axis=1) for w in POOL_WINDOWS], axis=1)
    inv_w = np.concatenate([np.full((1, POOL_GROUP_DIM), 1.0 / w) for w in POOL_WINDOWS], axis=1)
    return (jnp.asarray(cur, BF), jnp.asarray(halo, BF), jnp.asarray(meta, BF),
            jnp.asarray(inv_meta, F32), jnp.asarray(inv_w, F32))


def _pool_mix(pooled_sum, inv_cnt, xn, wp_ref, scale):
    diff = (pooled_sum * inv_cnt - xn).astype(BF)
    mixed = jnp.concatenate(
        [_dot(diff[:, _group_cols(g)], wp_ref[g]) for g in range(POOL_GROUPS)], axis=1)
    return mixed * scale


def _pool_prompt_tile(x_ref, halo_ref, nw_ref, wp_ref, sc_ref, bc_ref, bh_ref, iw_ref, tail_ref):
    nw = nw_ref[...]
    scale = sc_ref[...]
    inv_w = iw_ref[...]
    halo = halo_ref[...]
    out = []
    for j in range(x_ref.shape[0] // SUB):
        x = x_ref[SUB * j:SUB * (j + 1), :]
        xn = _rms(x, nw)
        xb = xn.astype(BF)
        pooled = jnp.concatenate(
            [_dot(bc_ref[g], xb[:, _group_cols(g)]) + _dot(bh_ref[g], halo[:, _group_cols(g)])
             for g in range(POOL_GROUPS)], axis=1)
        out.append(x + _pool_mix(pooled, inv_w, xn, wp_ref, scale))
        halo = xb[SUB - N_META:, :]
    halo_ref[...] = halo
    tail_ref[0] = xn[SUB - N_META:, :]
    return jnp.concatenate(out, axis=0)


def _pool_sample_kernel(x_ref, prev_ref, meta_ref, nw_ref, wp_ref, sc_ref, iw_ref, bm_ref, icm_ref,
                        h_ref, np_ref, hm_ref):
    @pl.when(pl.program_id(0) == 0)
    def _():
        xm = meta_ref[...]
        xnm = _rms(xm, nw_ref[...])
        xbm = xnm.astype(BF)
        pooled_m = jnp.concatenate(
            [_dot(bm_ref[g], xbm[:, _group_cols(g)]) for g in range(POOL_GROUPS)], axis=1)
        hm_ref[...] = xm + _pool_mix(pooled_m, icm_ref[...], xnm, wp_ref, sc_ref[...])

    x = x_ref[:, 0, :]
    xn = _rms(x, nw_ref[...])
    sums = []
    for g, w in enumerate(POOL_WINDOWS):
        s = xn[:, _group_cols(g)]
        for j in range(1, w):
            s = s + prev_ref[POOL_BUF - j, :, _group_cols(g)]
        sums.append(s)
    pooled = jnp.concatenate(sums, axis=1)
    h_ref[...] = x + _pool_mix(pooled, iw_ref[...], xn, wp_ref, sc_ref[...])
    for r in range(POOL_BUF - 1):
        np_ref[r] = prev_ref[r + 1]
    np_ref[POOL_BUF - 1] = xn


def _pool_sample(x, prev, meta, nw, wp, scale, tb=POOL_SAMPLE_TB):
    b = x.shape[0]
    _, _, band_meta, inv_meta, inv_w = _band_matrices()
    consts = (meta, nw, wp, scale, inv_w, band_meta, inv_meta)
    st_spec = pl.BlockSpec((None, POOL_BUF, tb, D_MODEL), lambda i: (0, 0, i, 0))
    return pl.pallas_call(
        _pool_sample_kernel,
        grid=(b // tb,),
        in_specs=[pl.BlockSpec((tb, 1, D_MODEL), lambda i: (i, 0, 0)), st_spec]
                 + [_const_spec(c.shape) for c in consts],
        out_specs=[pl.BlockSpec((tb, D_MODEL), lambda i: (i, 0)), st_spec,
                   pl.BlockSpec(meta.shape, lambda i: (0, 0))],
        out_shape=[jax.ShapeDtypeStruct((b, D_MODEL), F32),
                   jax.ShapeDtypeStruct(prev.shape, F32),
                   jax.ShapeDtypeStruct(meta.shape, F32)],
        compiler_params=_params(1),
        name="pool_sample",
    )(x, prev, *consts)


def _small_ffn_kernel(hs_ref, hm_ref, nw_ref, wg_ref, wu_ref, wd_ref, o_ref, wg_bf, wu_bf, wd_bf, hn):
    @pl.when(pl.program_id(0) == 0)
    def _():
        h = jnp.concatenate([hs_ref[...], hm_ref[...]], axis=0)
        hn[...] = _rms(h, nw_ref[...]).astype(BF)
        o_ref[...] = h

    wg = wg_ref[...].astype(BF)
    wu = wu_ref[...].astype(BF)
    wd = wd_ref[...].astype(BF)
    wg_bf[...] = wg
    wu_bf[...] = wu
    wd_bf[...] = wd
    act = (_silu(_dot(hn[...], wg)) * _dot(hn[...], wu)).astype(BF)
    o_ref[...] += _dot(act, wd)


def _small_ffn(hs, hm, nw, ffn_w32, layer):
    rows = hs.shape[0] + hm.shape[0]
    whole = pl.BlockSpec((rows, D_MODEL), lambda c: (0, 0))
    out = pl.pallas_call(
        _small_ffn_kernel,
        grid=(D_FF // FF_STREAM,),
        in_specs=[_const_spec(hs.shape), _const_spec(hm.shape), _const_spec(nw.shape),
                  pl.BlockSpec((None, D_MODEL, FF_STREAM), lambda c: (layer, 0, c)),
                  pl.BlockSpec((None, D_MODEL, FF_STREAM), lambda c: (layer, 0, c)),
                  pl.BlockSpec((None, FF_STREAM, D_MODEL), lambda c: (layer, c, 0))],
        out_specs=[whole,
                   pl.BlockSpec((D_MODEL, FF_STREAM), lambda c: (0, c)),
                   pl.BlockSpec((D_MODEL, FF_STREAM), lambda c: (0, c)),
                   pl.BlockSpec((FF_STREAM, D_MODEL), lambda c: (c, 0))],
        out_shape=[jax.ShapeDtypeStruct((rows, D_MODEL), F32),
                   jax.ShapeDtypeStruct((D_MODEL, D_FF), BF), jax.ShapeDtypeStruct((D_MODEL, D_FF), BF),
                   jax.ShapeDtypeStruct((D_FF, D_MODEL), BF)],
        scratch_shapes=[pltpu.VMEM((rows, D_MODEL), BF)],
        compiler_params=_params(1),
        name="small_ffn",
    )(hs, hm, nw, *ffn_w32)
    return out[0], tuple(out[1:])


def _small_proj_kernel(h_ref, nw_ref, w_ref, cos_ref, sin_ref, kdec_ref,
                       p_ref, kd_ref, qt_ref, kt_ref, v_ref, w_bf, hn):
    c = pl.program_id(0)
    bs = qt_ref.shape[1]

    @pl.when(c == 0)
    def _():
        hn[...] = _rms(h_ref[...], nw_ref[...]).astype(BF)

    w = w_ref[...].astype(BF)
    w_bf[...] = w
    y = _dot(hn[...], w)

    def rotated():
        return _rotary(y, jnp.concatenate([cos_ref[...]] * RET_HEADS, axis=1),
                       jnp.concatenate([sin_ref[...]] * RET_HEADS, axis=1))

    @pl.when(c == 0)
    def _():
        q = (rotated() * (RET_KDIM ** -0.5)).astype(BF)
        p_ref[...] = q
        qt_ref[...] = q[:bs].astype(F32).T

    @pl.when(c == 1)
    def _():
        k = rotated()
        p_ref[...] = k.astype(BF)
        kt_ref[...] = k[:bs].astype(BF).astype(F32).T
        kd_ref[...] = (k * kdec_ref[...]).astype(BF)

    @pl.when((c == 2) | (c == 3))
    def _():
        v = y.astype(BF)
        p_ref[...] = v
        v_ref[...] = v[:bs].astype(F32)

    @pl.when(c >= 4)
    def _():
        p_ref[...] = y.astype(BF)


def _small_proj(h, nw, w_in32, cos, sin, kdec, bs):
    rows = h.shape[0]
    whole = lambda r, w: pl.BlockSpec((r, w), lambda c: (0, 0))
    v_chunk = lambda c: jnp.clip(c - 2, 0, 1)
    return pl.pallas_call(
        _small_proj_kernel,
        grid=(6,),
        in_specs=[whole(rows, D_MODEL), _const_spec(nw.shape),
                  pl.BlockSpec((None, D_MODEL, D_MODEL), lambda c: (0, 0, c)),
                  _const_spec(cos.shape), _const_spec(sin.shape), _const_spec(kdec.shape)],
        out_specs=[pl.BlockSpec((rows, D_MODEL), lambda c: (0, c)),
                   whole(rows, D_MODEL), whole(D_MODEL, bs), whole(D_MODEL, bs),
                   pl.BlockSpec((bs, D_MODEL), lambda c: (0, v_chunk(c))),
                   pl.BlockSpec((D_MODEL, D_MODEL), lambda c: (0, c))],
        out_shape=[jax.ShapeDtypeStruct((rows, 6 * D_MODEL), BF),
                   jax.ShapeDtypeStruct((rows, D_MODEL), BF),
                   jax.ShapeDtypeStruct((D_MODEL, bs), F32),
                   jax.ShapeDtypeStruct((D_MODEL, bs), F32),
                   jax.ShapeDtypeStruct((bs, 2 * D_MODEL), F32),
                   jax.ShapeDtypeStruct((D_MODEL, 6 * D_MODEL), BF)],
        scratch_shapes=[pltpu.VMEM((rows, D_MODEL), BF)],
        compiler_params=_params(1),
        name="small_proj",
    )(h, nw, w_in32, cos, sin, kdec)


STREAM_TB = 4


def _head_state_update(row, s, h, lg_ref, qt, kt, v_ref, gw_ref, gb_ref, s_in, s_out, on_ref):
    gamma = jnp.exp(jnp.full((1, 1), 1.0, F32) * lg_ref[h])
    kc = _head_cols(h, RET_KDIM)
    vc = _head_cols(h, RET_VDIM)
    q = qt[kc, s:s + 1]
    k = kt[kc, s:s + 1]
    v = v_ref[pl.ds(row, 1), vc]
    s_prev = s_in[s, h]
    qs = jnp.sum(q * s_prev, axis=0, keepdims=True)
    score = jnp.sum(q * k, axis=0, keepdims=True)
    o = score * v + qs * gamma
    s_out[s, h] = gamma * s_prev + k * v
    on_ref[pl.ds(row, 1), vc] = _group_norm(o, gw_ref[:, vc], gb_ref[:, vc])


def _layer0_stream_kernel(tiles_per_seq, lg_ref, x_ref, meta_ref, nm_ref, wp_ref, sc_ref, bc_ref, bh_ref,
                          iw_ref, nw_ref, wg_ref, wu_ref, wd_ref, qt_ref, kt_ref, v_ref, gw_ref, gb_ref,
                          s_hbm, o_ref, tail_ref, on_ref, so_hbm, halo, s_in, s_out, in_sem, out_sem):
    i = pl.program_id(0)
    n = pl.num_programs(0)

    @pl.when(i % tiles_per_seq == 0)
    def _():
        halo[...] = _rms(meta_ref[...], nm_ref[...]).astype(BF)

    def in_copy(step, s):
        return pltpu.make_async_copy(s_hbm.at[0, step * STREAM_TB + s], s_in.at[s], in_sem.at[s])

    def out_copy(step, s):
        return pltpu.make_async_copy(s_out.at[s], so_hbm.at[0, step * STREAM_TB + s], out_sem.at[s])

    @pl.when(i == 0)
    def _():
        for s in range(STREAM_TB):
            in_copy(0, s).start()

    lanes = qt_ref.shape[1]
    shift = (lanes - i * STREAM_TB) % lanes
    qt = pltpu.roll(qt_ref[...], shift, 1)
    kt = pltpu.roll(kt_ref[...], shift, 1)

    h = _pool_prompt_tile(x_ref, halo, nm_ref, wp_ref, sc_ref, bc_ref, bh_ref, iw_ref, tail_ref)
    hn = _rms(h, nw_ref[...]).astype(BF)

    ffn = {"acc": h, "act": None}

    def up_unit(cols):
        c0, c1 = cols
        gt = _dot(hn, wg_ref[:, c0:c1])
        up = _dot(hn, wu_ref[:, c0:c1])
        ffn["act"] = (_silu(gt) * up).astype(BF)

    def down_unit(cols):
        c0, c1 = cols
        ffn["acc"] = ffn["acc"] + _dot(ffn["act"], wd_ref[c0:c1, :])

    units = [functools.partial(u, cols) for cols in FF_CHUNKS for u in (up_unit, down_unit)]
    first_half = len(units) // 2
    for u in units[:first_half]:
        u()
    for s in range(STREAM_TB):
        in_copy(i, s).wait()

    @pl.when(i > 0)
    def _():
        for s in range(STREAM_TB):
            out_copy(i - 1, s).wait()

    rest = units[first_half:]
    updates = [(s, hd) for s in range(STREAM_TB) for hd in range(RET_HEADS)]
    done = 0
    for j, (s, hd) in enumerate(updates):
        _head_state_update(i * STREAM_TB + s, s, hd, lg_ref, qt, kt, v_ref, gw_ref, gb_ref,
                           s_in, s_out, on_ref)
        target = ((j + 1) * len(rest)) // len(updates)
        for u in rest[done:target]:
            u()
        done = target
    o_ref[...] = ffn["acc"]

    for s in range(STREAM_TB):
        out_copy(i, s).start()

    @pl.when(i + 1 < n)
    def _():
        for s in range(STREAM_TB):
            in_copy(i + 1, s).start()

    @pl.when(i == n - 1)
    def _():
        for s in range(STREAM_TB):
            out_copy(i, s).wait()


def _layer0_stream(x, meta, nm, wp, scale, nw, ffn_w, tm, log_g, qt, kt, v_s, gn_w, gn_b, s0):
    b, t, _ = x.shape
    rows = b * t
    n = rows // tm
    assert v_s.shape[0] == n * STREAM_TB and t % tm == 0
    tiles_per_seq = t // tm
    band_cur, band_halo, _, _, inv_w = _band_matrices()
    row_spec = pl.BlockSpec((tm, D_MODEL), lambda i, lg: (i, 0))
    tail_spec = pl.BlockSpec((1, N_META, D_MODEL), lambda i, lg: (i // tiles_per_seq, 0, 0))
    on_spec = pl.BlockSpec(v_s.shape, lambda i, lg: (0, 0))
    hbm = pl.BlockSpec(memory_space=pl.ANY)
    blk = (STREAM_TB, RET_HEADS, RET_KDIM, RET_VDIM)
    consts = (meta, nm, wp, scale, band_cur, band_halo, inv_w, nw) + tuple(ffn_w) + (qt, kt, v_s, gn_w, gn_b)
    return pl.pallas_call(
        functools.partial(_layer0_stream_kernel, tiles_per_seq),
        grid_spec=pltpu.PrefetchScalarGridSpec(
            num_scalar_prefetch=1,
            grid=(n,),
            in_specs=[row_spec] + [_const_spec(c.shape) for c in consts] + [hbm],
            out_specs=[row_spec, tail_spec, on_spec, hbm],
            scratch_shapes=[pltpu.VMEM((N_META, D_MODEL), BF),
                            pltpu.VMEM(blk, F32), pltpu.VMEM(blk, F32),
                            pltpu.SemaphoreType.DMA((STREAM_TB,)), pltpu.SemaphoreType.DMA((STREAM_TB,))]),
        out_shape=[jax.ShapeDtypeStruct((rows, D_MODEL), F32),
                   jax.ShapeDtypeStruct((b, N_META, D_MODEL), F32),
                   jax.ShapeDtypeStruct(v_s.shape, F32),
                   jax.ShapeDtypeStruct(s0.shape, F32)],
        compiler_params=_params(1),
        name="layer0_stream",
    )(log_g, x.reshape(rows, D_MODEL), *consts, s0)


def _cast_row_blocks(rows, n):
    for hold in range(1, n + 1):
        blocks = n // hold
        if n % hold == 0 and rows % blocks == 0 and (rows // blocks) % BF16_TILE_ROWS == 0:
            return rows // blocks, hold
    raise ValueError((rows, n))


def _cast_specs(w32, layer, n):
    rows, cols = w32.shape[1:]
    blk, hold = _cast_row_blocks(rows, n)
    src = pl.BlockSpec((None, blk, cols), lambda i: (layer, i // hold, 0))
    dst = pl.BlockSpec((blk, cols), lambda i: (i // hold, 0))
    return src, dst, jax.ShapeDtypeStruct((rows, cols), BF)


def _proj_kernel(h_ref, nw_ref, w_ref, cos_ref, sin_ref, cross_ref, kdec_ref,
                 c0_ref, c1_ref, c2_ref, c3_ref,
                 q_ref, qx_ref, k_ref, kd_ref, v_ref, g_ref, d0_ref, d1_ref, d2_ref, d3_ref):
    for src, dst in ((c0_ref, d0_ref), (c1_ref, d1_ref), (c2_ref, d2_ref), (c3_ref, d3_ref)):
        dst[...] = src[...].astype(BF)

    hn = _rms(h_ref[...], nw_ref[...]).astype(BF)
    cos = jnp.concatenate([cos_ref[...]] * RET_HEADS, axis=1)
    sin = jnp.concatenate([sin_ref[...]] * RET_HEADS, axis=1)

    q = _rotary(_dot(hn, w_ref[:, 0:D_MODEL]), cos, sin) * (RET_KDIM ** -0.5)
    q_ref[...] = q.astype(BF)
    qx_ref[...] = (q * cross_ref[...]).astype(BF)
    k = _rotary(_dot(hn, w_ref[:, D_MODEL:2 * D_MODEL]), cos, sin)
    k_ref[...] = k.astype(BF)
    kd_ref[...] = (k * kdec_ref[...]).astype(BF)
    v_ref[...] = _dot(hn, w_ref[:, 2 * D_MODEL:4 * D_MODEL]).astype(BF)
    g_ref[...] = _dot(hn, w_ref[:, 4 * D_MODEL:6 * D_MODEL]).astype(BF)


def _proj(h, nw, w_in, cos, sin, cross, kdec, tm, cast):
    rows = h.shape[0]
    n = rows // tm
    ntab = cos.shape[0] // tm
    row_spec = lambda width: pl.BlockSpec((tm, width), lambda i: (i, 0))
    tab_spec = pl.BlockSpec((tm, RET_KDIM), lambda i: (i % ntab, 0))
    widths = (D_MODEL, D_MODEL, D_MODEL, D_MODEL, 2 * D_MODEL, 2 * D_MODEL)
    cast_src, cast_dst, cast_shape = zip(*[_cast_specs(w32, layer, n) for w32, layer in cast])
    out = pl.pallas_call(
        _proj_kernel,
        grid=(n,),
        in_specs=[row_spec(D_MODEL), _const_spec(nw.shape), _const_spec(w_in.shape),
                  tab_spec, tab_spec, _const_spec(cross.shape), _const_spec(kdec.shape)] + list(cast_src),
        out_specs=[row_spec(w) for w in widths] + list(cast_dst),
        out_shape=[jax.ShapeDtypeStruct((rows, w), BF) for w in widths] + list(cast_shape),
        compiler_params=_params(1),
        name="ret_proj",
    )(h, nw, w_in, cos, sin, cross, kdec, *[w32 for w32, _ in cast])
    return tuple(out[:len(widths)]), tuple(out[len(widths):])


def _ret_prompt_kernel(tot_ref, kdm_ref, vm_ref, dm_ref, q_ref, qx_ref, k_ref, kd_ref, v_ref,
                       gw_ref, gb_ref, o_ref, s_ref, state):
    t = pl.program_id(1)

    @pl.when(t == 0)
    def _():
        for h in range(RET_HEADS):
            state[h] = _kt_v(kdm_ref[:, _head_cols(h, RET_KDIM)], vm_ref[:, _head_cols(h, RET_VDIM)])

    for c in range(q_ref.shape[1] // SUB):
        rows = slice(c * SUB, (c + 1) * SUB)
        for h in range(RET_HEADS):
            kc = _head_cols(h, RET_KDIM)
            vc = _head_cols(h, RET_VDIM)
            q = q_ref[0, rows, kc]
            v = v_ref[0, rows, vc]
            s_prev = state[h]
            scores = lax.dot_general(q, k_ref[0, rows, kc], (((1,), (1,)), ((), ())),
                                     preferred_element_type=F32) * dm_ref[h]
            o = _dot(scores.astype(BF), v) + _dot(qx_ref[0, rows, kc], s_prev.astype(BF))
            state[h] = tot_ref[h] * s_prev + _kt_v(kd_ref[0, rows, kc], v)
            o_ref[0, rows, vc] = _group_norm(o, gw_ref[:, vc], gb_ref[:, vc]).astype(BF)

    @pl.when(t == pl.num_programs(1) - 1)
    def _():
        s_ref[0] = state[...]


def _ret_prompt(chunk_decay, kd_meta, v_meta, dmask, q, qx, k, kd, v, gn_w, gn_b, tt=4 * SUB):
    b, t, _ = q.shape
    qk_spec = pl.BlockSpec((1, tt, D_MODEL), lambda i, j, s: (i, j, 0))
    v_spec = pl.BlockSpec((1, tt, 2 * D_MODEL), lambda i, j, s: (i, j, 0))
    st_shape = (RET_HEADS, RET_KDIM, RET_VDIM)
    return pl.pallas_call(
        _ret_prompt_kernel,
        grid_spec=pltpu.PrefetchScalarGridSpec(
            num_scalar_prefetch=1,
            grid=(b, t // tt),
            in_specs=[_const_spec(kd_meta.shape), _const_spec(v_meta.shape), _const_spec(dmask.shape),
                      qk_spec, qk_spec, qk_spec, qk_spec, v_spec,
                      _const_spec(gn_w.shape), _const_spec(gn_b.shape)],
            out_specs=[v_spec,
                       pl.BlockSpec((None, 1) + st_shape, lambda i, j, s: (0, i, 0, 0, 0))],
            scratch_shapes=[pltpu.VMEM(st_shape, F32)]),
        out_shape=[jax.ShapeDtypeStruct((b, t, 2 * D_MODEL), BF),
                   jax.ShapeDtypeStruct((1, b) + st_shape, F32)],
        compiler_params=_params(2),
        name="ret_prompt",
    )(chunk_decay, kd_meta, v_meta, dmask, q, qx, k, kd, v, gn_w, gn_b)


def _out_ffn_rows(h, on, g, wo_ref, nw_ref, wg_ref, wu_ref, wd_ref, nf_ref):
    gated = (_silu(g.astype(F32)) * on.astype(F32)).astype(BF)
    h = h + _dot(gated, wo_ref[...])
    hn = _rms(h, nw_ref[...]).astype(BF)
    for c0, c1 in FF_CHUNKS:
        act = (_silu(_dot(hn, wg_ref[:, c0:c1])) * _dot(hn, wu_ref[:, c0:c1])).astype(BF)
        h = h + _dot(act, wd_ref[c0:c1, :])
    return _rms(h, nf_ref[...])


def _out_ffn_kernel(h_ref, on_ref, g_ref, hs_ref, ons_ref, gs_ref, wo_ref, nw_ref, wg_ref, wu_ref,
                    wd_ref, nf_ref, o_ref, os_ref):
    weights = (wo_ref, nw_ref, wg_ref, wu_ref, wd_ref, nf_ref)
    last = pl.program_id(0) == pl.num_programs(0) - 1

    @pl.when(jnp.logical_not(last))
    def _():
        o_ref[...] = _out_ffn_rows(h_ref[...], on_ref[...], g_ref[...], *weights)

    @pl.when(last)
    def _():
        os_ref[...] = _out_ffn_rows(hs_ref[...], ons_ref[...], gs_ref[...], *weights)


def _out_ffn(h, on, g, h_small, on_s, proj_s, wo, nw, ffn_w, nf, tm):
    rows = h.shape[0]
    n = rows // tm
    rows_s = on_s.shape[0]
    gate_block = (proj_s.shape[1] - 2 * D_MODEL) // (2 * D_MODEL)
    row_spec = lambda width: pl.BlockSpec((tm, width), lambda i: (jnp.minimum(i, n - 1), 0))
    small_spec = lambda width, blk=0: pl.BlockSpec((rows_s, width), lambda i: (0, blk))
    consts = (wo, nw) + tuple(ffn_w) + (nf,)
    return pl.pallas_call(
        _out_ffn_kernel,
        grid=(n + 1,),
        in_specs=[row_spec(D_MODEL), row_spec(2 * D_MODEL), row_spec(2 * D_MODEL),
                  small_spec(D_MODEL), small_spec(2 * D_MODEL), small_spec(2 * D_MODEL, gate_block)]
                 + [_const_spec(c.shape) for c in consts],
        out_specs=[row_spec(D_MODEL), small_spec(D_MODEL)],
        out_shape=[jax.ShapeDtypeStruct((rows, D_MODEL), F32),
                   jax.ShapeDtypeStruct((rows_s, D_MODEL), F32)],
        compiler_params=_params(1),
        name="out_ffn",
    )(h, on, g, h_small, on_s, proj_s, *consts)


def _rotary_tables(pos):
    theta = 1.0 / (ROPE_BASE ** jnp.linspace(0.0, 1.0, RET_KDIM // 2, dtype=F32))
    ang = pos.astype(F32)[:, None] * jnp.repeat(theta, 2)[None, :]
    sign = jnp.asarray(np.tile([-1.0, 1.0], RET_KDIM // 2), F32)
    return jnp.cos(ang), jnp.sin(ang) * sign[None, :]


def _per_head_cols(table):
    return jnp.asarray(np.repeat(table, RET_KDIM, axis=1), F32)


def _decay_to_end(c):
    return np.exp((c - 1.0 - np.arange(c))[:, None] * LOG_GAMMA[None, :])


def _decay_from_start(c):
    return np.exp((np.arange(c) + 1.0)[:, None] * LOG_GAMMA[None, :])


def _decay_mask(c):
    rel = np.arange(c)[:, None] - np.arange(c)[None, :]
    return np.where(rel[None] >= 0, np.exp(np.maximum(rel, 0)[None] * LOG_GAMMA[:, None, None]), 0.0)


def kernel(x_prompt, x_sample, state_pool, state_ret, meta_tokens, norm_mix, norm_ffn, norm_final,
           w_pool, pool_scale, w_ret_in, ret_gn_w, ret_gn_b, w_ret_out, w_ffn_gate, w_ffn_up, w_ffn_down):
    b, t, _ = x_prompt.shape
    bs = x_sample.shape[0]
    log_g = jnp.asarray(LOG_GAMMA, F32)

    wp = w_pool[0].astype(BF)
    ffn_w32 = (w_ffn_gate, w_ffn_up, w_ffn_down)
    nm0, nm1 = norm_mix[0:1], norm_mix[1:2]
    nf0, nf1 = norm_ffn[0:1], norm_ffn[1:2]
    nfin = norm_final[None, :]
    scale = pool_scale[0:1]
    gn_w, gn_b = ret_gn_w[0:1], ret_gn_b[0:1]

    hs, new_pool_sample, h_meta = _pool_sample(
        x_sample, state_pool.transpose(0, 2, 1, 3), meta_tokens, nm0, wp, scale)
    new_pool_sample = new_pool_sample.transpose(0, 2, 1, 3)
    h_small, ffn_w0 = _small_ffn(hs, h_meta, nf0, ffn_w32, 0)

    pos_small = np.concatenate([np.full((bs,), PAST_LEN), np.arange(N_META)]).astype(np.int32)
    cos_s, sin_s = _rotary_tables(jnp.asarray(pos_small))
    kdec_s = _per_head_cols(np.concatenate([np.ones((bs, RET_HEADS)), _decay_to_end(N_META)], axis=0))
    proj_s, kd_s, qt_s, kt_s, v_s, w_in = _small_proj(h_small, nm1, w_ret_in, cos_s, sin_s, kdec_s, bs)

    h, tail, on_s, new_ret_sample = _layer0_stream(
        x_prompt, meta_tokens, nm0, wp, scale, nf0, ffn_w0, TM, log_g, qt_s, kt_s, v_s, gn_w, gn_b, state_ret)

    cos, sin = _rotary_tables(N_META + jnp.arange(t, dtype=jnp.int32))
    reps = (TM // SUB, 1)
    (q, qx, k, kd, v, g), (w_out, *ffn_w1) = _proj(
        h, nm1, w_in, cos, sin,
        _per_head_cols(np.tile(_decay_from_start(SUB), reps)),
        _per_head_cols(np.tile(_decay_to_end(SUB), reps)), TM,
        cast=((w_ret_out, 0), (w_ffn_gate, 1), (w_ffn_up, 1), (w_ffn_down, 1)))
    seq = lambda a: a.reshape(b, t, -1)
    on, new_ret_prompt = _ret_prompt(jnp.asarray(np.exp(SUB * LOG_GAMMA), F32), kd_s[bs:],
                                     proj_s[bs:, 2 * D_MODEL:4 * D_MODEL],
                                     jnp.asarray(_decay_mask(SUB), F32),
                                     seq(q), seq(qx), seq(k), seq(kd), seq(v), gn_w, gn_b)
    y, y_sample = _out_ffn(h, on.reshape(b * t, -1), g, h_small, on_s, proj_s, w_out, nf1, ffn_w1, nfin, TM)

    return (y.reshape(b, t, D_MODEL), y_sample.reshape(bs, 1, D_MODEL), tail[:, 1:][None],
            new_pool_sample, new_ret_prompt, new_ret_sample)
```

```python
import functools

import numpy as np
import jax
import jax.numpy as jnp
from jax import lax
from jax.experimental import pallas as pl
from jax.experimental.pallas import tpu as pltpu

D_MODEL = 1024
N_META = 16
PAST_LEN = 16384
POOL_WINDOWS = (2, 4, 8, 16)
POOL_GROUPS = len(POOL_WINDOWS)
POOL_GROUP_DIM = D_MODEL // POOL_GROUPS
POOL_BUF = max(POOL_WINDOWS) - 1
RET_HEADS = 4
RET_KDIM = D_MODEL // RET_HEADS
RET_VDIM = 2 * D_MODEL // RET_HEADS
ROPE_BASE = 10000.0
D_FF = 2816
EPS = 1e-6

BF = jnp.bfloat16
F32 = jnp.float32

SUB = 256
TM = 512
FF_CHUNKS = tuple((c, min(c + 256, D_FF)) for c in range(0, D_FF, 256))
FF_STREAM = 256
assert D_FF % FF_STREAM == 0
POOL_SAMPLE_TB = 32
BF16_TILE_ROWS = 16
V7X_VMEM_BYTES = 64 * 1024 * 1024
VMEM_LIMIT = V7X_VMEM_BYTES * 7 // 8

LOG_GAMMA = np.log(1.0 - 2.0 ** (-5.0 - np.arange(RET_HEADS)))


def _dot(a, b):
    return jnp.dot(a, b, preferred_element_type=F32)


def _kt_v(k, v):
    return lax.dot_general(k, v, (((0,), (0,)), ((), ())), preferred_element_type=F32)


def _rms(x, g):
    ms = jnp.mean(x * x, axis=-1, keepdims=True)
    return (x * lax.rsqrt(ms + EPS)) * g


def _silu(x):
    return x * jax.nn.sigmoid(x)


def _group_norm(o, gn_w, gn_b):
    mu = jnp.mean(o, axis=-1, keepdims=True)
    cen = o - mu
    var = jnp.mean(cen * cen, axis=-1, keepdims=True)
    return (cen * lax.rsqrt(var + EPS)) * gn_w + gn_b


def _rotary(x, cos, sin):
    lane = lax.broadcasted_iota(jnp.int32, x.shape, 1)
    partner = jnp.where((lane & 1) == 0, pltpu.roll(x, x.shape[1] - 1, 1), pltpu.roll(x, 1, 1))
    return x * cos + partner * sin


def _head_cols(h, width):
    return slice(h * width, (h + 1) * width)


def _group_cols(g):
    return slice(g * POOL_GROUP_DIM, (g + 1) * POOL_GROUP_DIM)


def _const_spec(shape):
    nd = len(shape)
    return pl.BlockSpec(shape, lambda *_: (0,) * nd, pipeline_mode=pl.Buffered(1))


def _params(n_grid):
    return pltpu.CompilerParams(
        dimension_semantics=("arbitrary",) * n_grid, vmem_limit_bytes=VMEM_LIMIT)


def _band_matrices():
    t = np.arange(SUB)[:, None]
    s = np.arange(SUB)[None, :]
    sh = np.arange(N_META)[None, :] - N_META
    cur = np.stack([((t - s >= 0) & (t - s < w)) for w in POOL_WINDOWS]).astype(np.float32)
    halo = np.stack([(t - sh < w) for w in POOL_WINDOWS]).astype(np.float32)
    tm = np.arange(N_META)[:, None]
    sm = np.arange(N_META)[None, :]
    meta = np.stack([((tm - sm >= 0) & (tm - sm < w)) for w in POOL_WINDOWS]).astype(np.float32)
    inv_meta = np.concatenate(
        [np.repeat(1.0 / np.minimum(w, tm + 1.0), POOL_GROUP_DIM, axis=1) for w in POOL_WINDOWS], axis=1)
    inv_w = np.concatenate([np.full((1, POOL_GROUP_DIM), 1.0 / w) for w in POOL_WINDOWS], axis=1)
    return (jnp.asarray(cur, BF), jnp.asarray(halo, BF), jnp.asarray(meta, BF),
            jnp.asarray(inv_meta, F32), jnp.asarray(inv_w, F32))


def _pool_mix(pooled_sum, inv_cnt, xn, wp_ref, scale):
    diff = (pooled_sum * inv_cnt - xn).astype(BF)
    mixed = jnp.concatenate(
        [_dot(diff[:, _group_cols(g)], wp_ref[g]) for g in range(POOL_GROUPS)], axis=1)
    return mixed * scale


def _pool_prompt_tile(x_ref, halo_ref, nw_ref, wp_ref, sc_ref, bc_ref, bh_ref, iw_ref, tail_ref):
    nw = nw_ref[...]
    scale = sc_ref[...]
    inv_w = iw_ref[...]
    halo = halo_ref[...]
    out = []
    for j in range(x_ref.shape[0] // SUB):
        x = x_ref[SUB * j:SUB * (j + 1), :]
        xn = _rms(x, nw)
        xb = xn.astype(BF)
        pooled = jnp.concatenate(
            [_dot(bc_ref[g], xb[:, _group_cols(g)]) + _dot(bh_ref[g], halo[:, _group_cols(g)])
             for g in range(POOL_GROUPS)], axis=1)
        out.append(x + _pool_mix(pooled, inv_w, xn, wp_ref, scale))
        halo = xb[SUB - N_META:, :]
    halo_ref[...] = halo
    tail_ref[0] = xn[SUB - N_META:, :]
    return jnp.concatenate(out, axis=0)


def _pool_sample_kernel(x_ref, prev_ref, meta_ref, nw_ref, wp_ref, sc_ref, iw_ref, bm_ref, icm_ref,
                        h_ref, np_ref, hm_ref):
    @pl.when(pl.program_id(0) == 0)
    def _():
        xm = meta_ref[...]
        xnm = _rms(xm, nw_ref[...])
        xbm = xnm.astype(BF)
        pooled_m = jnp.concatenate(
            [_dot(bm_ref[g], xbm[:, _group_cols(g)]) for g in range(POOL_GROUPS)], axis=1)
        hm_ref[...] = xm + _pool_mix(pooled_m, icm_ref[...], xnm, wp_ref, sc_ref[...])

    x = x_ref[:, 0, :]
    xn = _rms(x, nw_ref[...])
    sums = []
    for g, w in enumerate(POOL_WINDOWS):
        s = xn[:, _group_cols(g)]
        for j in range(1, w):
            s = s + prev_ref[POOL_BUF - j, :, _group_cols(g)]
        sums.append(s)
    pooled = jnp.concatenate(sums, axis=1)
    h_ref[...] = x + _pool_mix(pooled, iw_ref[...], xn, wp_ref, sc_ref[...])
    for r in range(POOL_BUF - 1):
        np_ref[r] = prev_ref[r + 1]
    np_ref[POOL_BUF - 1] = xn


def _pool_sample(x, prev, meta, nw, wp, scale, tb=POOL_SAMPLE_TB):
    b = x.shape[0]
    _, _, band_meta, inv_meta, inv_w = _band_matrices()
    consts = (meta, nw, wp, scale, inv_w, band_meta, inv_meta)
    st_spec = pl.BlockSpec((None, POOL_BUF, tb, D_MODEL), lambda i: (0, 0, i, 0))
    return pl.pallas_call(
        _pool_sample_kernel,
        grid=(b // tb,),
        in_specs=[pl.BlockSpec((tb, 1, D_MODEL), lambda i: (i, 0, 0)), st_spec]
                 + [_const_spec(c.shape) for c in consts],
        out_specs=[pl.BlockSpec((tb, D_MODEL), lambda i: (i, 0)), st_spec,
                   pl.BlockSpec(meta.shape, lambda i: (0, 0))],
        out_shape=[jax.ShapeDtypeStruct((b, D_MODEL), F32),
                   jax.ShapeDtypeStruct(prev.shape, F32),
                   jax.ShapeDtypeStruct(meta.shape, F32)],
        compiler_params=_params(1),
        name="pool_sample",
    )(x, prev, *consts)


STREAM_BUFFERS = 3


def _small_ffn_kernel(layer, hs_ref, hm_ref, nw_ref, wg_hbm, wu_hbm, wd_hbm,
                      o_ref, wg_bf_hbm, wu_bf_hbm, wd_bf_hbm, hn):
    h = jnp.concatenate([hs_ref[...], hm_ref[...]], axis=0)
    hn[...] = _rms(h, nw_ref[...]).astype(BF)
    o_ref[...] = h

    def step(wg_ref, wu_ref, wd_ref, wg_bf, wu_bf, wd_bf):
        wg = wg_ref[...].astype(BF)
        wu = wu_ref[...].astype(BF)
        wd = wd_ref[...].astype(BF)
        wg_bf[...] = wg
        wu_bf[...] = wu
        wd_bf[...] = wd
        act = (_silu(_dot(hn[...], wg)) * _dot(hn[...], wu)).astype(BF)
        o_ref[...] += _dot(act, wd)

    deep = pl.Buffered(STREAM_BUFFERS)
    cols = lambda c: (0, c)
    rows = lambda c: (c, 0)
    pltpu.emit_pipeline(
        step,
        grid=(D_FF // FF_STREAM,),
        in_specs=[pl.BlockSpec((D_MODEL, FF_STREAM), cols, pipeline_mode=deep),
                  pl.BlockSpec((D_MODEL, FF_STREAM), cols, pipeline_mode=deep),
                  pl.BlockSpec((FF_STREAM, D_MODEL), rows, pipeline_mode=deep)],
        out_specs=[pl.BlockSpec((D_MODEL, FF_STREAM), cols),
                   pl.BlockSpec((D_MODEL, FF_STREAM), cols),
                   pl.BlockSpec((FF_STREAM, D_MODEL), rows)],
    )(wg_hbm.at[layer], wu_hbm.at[layer], wd_hbm.at[layer], wg_bf_hbm, wu_bf_hbm, wd_bf_hbm)


def _small_ffn(hs, hm, nw, ffn_w32, layer):
    rows = hs.shape[0] + hm.shape[0]
    vmem = pl.BlockSpec(memory_space=pltpu.VMEM)
    hbm = pl.BlockSpec(memory_space=pl.ANY)
    out = pl.pallas_call(
        functools.partial(_small_ffn_kernel, layer),
        in_specs=[vmem, vmem, vmem, hbm, hbm, hbm],
        out_specs=[vmem, hbm, hbm, hbm],
        out_shape=[jax.ShapeDtypeStruct((rows, D_MODEL), F32),
                   jax.ShapeDtypeStruct((D_MODEL, D_FF), BF), jax.ShapeDtypeStruct((D_MODEL, D_FF), BF),
                   jax.ShapeDtypeStruct((D_FF, D_MODEL), BF)],
        scratch_shapes=[pltpu.VMEM((rows, D_MODEL), BF)],
        compiler_params=pltpu.CompilerParams(vmem_limit_bytes=VMEM_LIMIT),
        name="small_ffn",
    )(hs, hm, nw, *ffn_w32)
    return out[0], tuple(out[1:])


def _small_proj_kernel(h_ref, nw_ref, w_ref, cos_ref, sin_ref, kdec_ref,
                       p_ref, kd_ref, qt_ref, kt_ref, v_ref, w_bf, hn):
    c = pl.program_id(0)
    bs = qt_ref.shape[1]

    @pl.when(c == 0)
    def _():
        hn[...] = _rms(h_ref[...], nw_ref[...]).astype(BF)

    w = w_ref[...].astype(BF)
    w_bf[...] = w
    y = _dot(hn[...], w)

    def rotated():
        return _rotary(y, jnp.concatenate([cos_ref[...]] * RET_HEADS, axis=1),
                       jnp.concatenate([sin_ref[...]] * RET_HEADS, axis=1))

    @pl.when(c == 0)
    def _():
        q = (rotated() * (RET_KDIM ** -0.5)).astype(BF)
        p_ref[...] = q
        qt_ref[...] = q[:bs].astype(F32).T

    @pl.when(c == 1)
    def _():
        k = rotated()
        p_ref[...] = k.astype(BF)
        kt_ref[...] = k[:bs].astype(BF).astype(F32).T
        kd_ref[...] = (k * kdec_ref[...]).astype(BF)

    @pl.when((c == 2) | (c == 3))
    def _():
        v = y.astype(BF)
        p_ref[...] = v
        v_ref[...] = v[:bs].astype(F32)

    @pl.when(c >= 4)
    def _():
        p_ref[...] = y.astype(BF)


def _small_proj(h, nw, w_in32, cos, sin, kdec, bs):
    rows = h.shape[0]
    whole = lambda r, w: pl.BlockSpec((r, w), lambda c: (0, 0))
    v_chunk = lambda c: jnp.clip(c - 2, 0, 1)
    return pl.pallas_call(
        _small_proj_kernel,
        grid=(6,),
        in_specs=[whole(rows, D_MODEL), _const_spec(nw.shape),
                  pl.BlockSpec((None, D_MODEL, D_MODEL), lambda c: (0, 0, c)),
                  _const_spec(cos.shape), _const_spec(sin.shape), _const_spec(kdec.shape)],
        out_specs=[pl.BlockSpec((rows, D_MODEL), lambda c: (0, c)),
                   whole(rows, D_MODEL), whole(D_MODEL, bs), whole(D_MODEL, bs),
                   pl.BlockSpec((bs, D_MODEL), lambda c: (0, v_chunk(c))),
                   pl.BlockSpec((D_MODEL, D_MODEL), lambda c: (0, c))],
        out_shape=[jax.ShapeDtypeStruct((rows, 6 * D_MODEL), BF),
                   jax.ShapeDtypeStruct((rows, D_MODEL), BF),
                   jax.ShapeDtypeStruct((D_MODEL, bs), F32),
                   jax.ShapeDtypeStruct((D_MODEL, bs), F32),
                   jax.ShapeDtypeStruct((bs, 2 * D_MODEL), F32),
                   jax.ShapeDtypeStruct((D_MODEL, 6 * D_MODEL), BF)],
        scratch_shapes=[pltpu.VMEM((rows, D_MODEL), BF)],
        compiler_params=_params(1),
        name="small_proj",
    )(h, nw, w_in32, cos, sin, kdec)


STREAM_TB = 4


def _head_state_update(row, s, h, lg_ref, qt, kt, v_ref, gw_ref, gb_ref, s_in, s_out, on_ref):
    gamma = jnp.exp(jnp.full((1, 1), 1.0, F32) * lg_ref[h])
    kc = _head_cols(h, RET_KDIM)
    vc = _head_cols(h, RET_VDIM)
    q = qt[kc, s:s + 1]
    k = kt[kc, s:s + 1]
    v = v_ref[pl.ds(row, 1), vc]
    s_prev = s_in[s, h]
    qs = jnp.sum(q * s_prev, axis=0, keepdims=True)
    score = jnp.sum(q * k, axis=0, keepdims=True)
    o = score * v + qs * gamma
    s_out[s, h] = gamma * s_prev + k * v
    on_ref[pl.ds(row, 1), vc] = _group_norm(o, gw_ref[:, vc], gb_ref[:, vc])


def _layer0_stream_kernel(tiles_per_seq, lg_ref, x_ref, meta_ref, nm_ref, wp_ref, sc_ref, bc_ref, bh_ref,
                          iw_ref, nw_ref, wg_ref, wu_ref, wd_ref, qt_ref, kt_ref, v_ref, gw_ref, gb_ref,
                          s_hbm, o_ref, tail_ref, on_ref, so_hbm, halo, s_in, s_out, in_sem, out_sem):
    i = pl.program_id(0)
    n = pl.num_programs(0)

    @pl.when(i % tiles_per_seq == 0)
    def _():
        halo[...] = _rms(meta_ref[...], nm_ref[...]).astype(BF)

    def in_copy(step, s):
        return pltpu.make_async_copy(s_hbm.at[0, step * STREAM_TB + s], s_in.at[s], in_sem.at[s])

    def out_copy(step, s):
        return pltpu.make_async_copy(s_out.at[s], so_hbm.at[0, step * STREAM_TB + s], out_sem.at[s])

    @pl.when(i == 0)
    def _():
        for s in range(STREAM_TB):
            in_copy(0, s).start()

    lanes = qt_ref.shape[1]
    shift = (lanes - i * STREAM_TB) % lanes
    qt = pltpu.roll(qt_ref[...], shift, 1)
    kt = pltpu.roll(kt_ref[...], shift, 1)

    h = _pool_prompt_tile(x_ref, halo, nm_ref, wp_ref, sc_ref, bc_ref, bh_ref, iw_ref, tail_ref)
    hn = _rms(h, nw_ref[...]).astype(BF)

    ffn = {"acc": h, "act": None}

    def up_unit(cols):
        c0, c1 = cols
        gt = _dot(hn, wg_ref[:, c0:c1])
        up = _dot(hn, wu_ref[:, c0:c1])
        ffn["act"] = (_silu(gt) * up).astype(BF)

    def down_unit(cols):
        c0, c1 = cols
        ffn["acc"] = ffn["acc"] + _dot(ffn["act"], wd_ref[c0:c1, :])

    units = [functools.partial(u, cols) for cols in FF_CHUNKS for u in (up_unit, down_unit)]
    first_half = len(units) // 2
    for u in units[:first_half]:
        u()
    for s in range(STREAM_TB):
        in_copy(i, s).wait()

    @pl.when(i > 0)
    def _():
        for s in range(STREAM_TB):
            out_copy(i - 1, s).wait()

    rest = units[first_half:]
    updates = [(s, hd) for s in range(STREAM_TB) for hd in range(RET_HEADS)]
    done = 0
    for j, (s, hd) in enumerate(updates):
        _head_state_update(i * STREAM_TB + s, s, hd, lg_ref, qt, kt, v_ref, gw_ref, gb_ref,
                           s_in, s_out, on_ref)
        target = ((j + 1) * len(rest)) // len(updates)
        for u in rest[done:target]:
            u()
        done = target
    o_ref[...] = ffn["acc"]

    for s in range(STREAM_TB):
        out_copy(i, s).start()

    @pl.when(i + 1 < n)
    def _():
        for s in range(STREAM_TB):
            in_copy(i + 1, s).start()

    @pl.when(i == n - 1)
    def _():
        for s in range(STREAM_TB):
            out_copy(i, s).wait()


def _layer0_stream(x, meta, nm, wp, scale, nw, ffn_w, tm, log_g, qt, kt, v_s, gn_w, gn_b, s0):
    b, t, _ = x.shape
    rows = b * t
    n = rows // tm
    assert v_s.shape[0] == n * STREAM_TB and t % tm == 0
    tiles_per_seq = t // tm
    band_cur, band_halo, _, _, inv_w = _band_matrices()
    row_spec = pl.BlockSpec((tm, D_MODEL), lambda i, lg: (i, 0))
    tail_spec = pl.BlockSpec((1, N_META, D_MODEL), lambda i, lg: (i // tiles_per_seq, 0, 0))
    on_spec = pl.BlockSpec(v_s.shape, lambda i, lg: (0, 0))
    hbm = pl.BlockSpec(memory_space=pl.ANY)
    blk = (STREAM_TB, RET_HEADS, RET_KDIM, RET_VDIM)
    consts = (meta, nm, wp, scale, band_cur, band_halo, inv_w, nw) + tuple(ffn_w) + (qt, kt, v_s, gn_w, gn_b)
    return pl.pallas_call(
        functools.partial(_layer0_stream_kernel, tiles_per_seq),
        grid_spec=pltpu.PrefetchScalarGridSpec(
            num_scalar_prefetch=1,
            grid=(n,),
            in_specs=[row_spec] + [_const_spec(c.shape) for c in consts] + [hbm],
            out_specs=[row_spec, tail_spec, on_spec, hbm],
            scratch_shapes=[pltpu.VMEM((N_META, D_MODEL), BF),
                            pltpu.VMEM(blk, F32), pltpu.VMEM(blk, F32),
                            pltpu.SemaphoreType.DMA((STREAM_TB,)), pltpu.SemaphoreType.DMA((STREAM_TB,))]),
        out_shape=[jax.ShapeDtypeStruct((rows, D_MODEL), F32),
                   jax.ShapeDtypeStruct((b, N_META, D_MODEL), F32),
                   jax.ShapeDtypeStruct(v_s.shape, F32),
                   jax.ShapeDtypeStruct(s0.shape, F32)],
        compiler_params=_params(1),
        name="layer0_stream",
    )(log_g, x.reshape(rows, D_MODEL), *consts, s0)


def _cast_row_blocks(rows, n):
    for hold in range(1, n + 1):
        blocks = n // hold
        if n % hold == 0 and rows % blocks == 0 and (rows // blocks) % BF16_TILE_ROWS == 0:
            return rows // blocks, hold
    raise ValueError((rows, n))


def _cast_specs(w32, layer, n):
    rows, cols = w32.shape[1:]
    blk, hold = _cast_row_blocks(rows, n)
    src = pl.BlockSpec((None, blk, cols), lambda i: (layer, i // hold, 0))
    dst = pl.BlockSpec((blk, cols), lambda i: (i // hold, 0))
    return src, dst, jax.ShapeDtypeStruct((rows, cols), BF)


def _proj_kernel(h_ref, nw_ref, w_ref, cos_ref, sin_ref, cross_ref, kdec_ref,
                 c0_ref, c1_ref, c2_ref, c3_ref,
                 q_ref, qx_ref, k_ref, kd_ref, v_ref, g_ref, d0_ref, d1_ref, d2_ref, d3_ref):
    for src, dst in ((c0_ref, d0_ref), (c1_ref, d1_ref), (c2_ref, d2_ref), (c3_ref, d3_ref)):
        dst[...] = src[...].astype(BF)

    hn = _rms(h_ref[...], nw_ref[...]).astype(BF)
    cos = jnp.concatenate([cos_ref[...]] * RET_HEADS, axis=1)
    sin = jnp.concatenate([sin_ref[...]] * RET_HEADS, axis=1)

    q = _rotary(_dot(hn, w_ref[:, 0:D_MODEL]), cos, sin) * (RET_KDIM ** -0.5)
    q_ref[...] = q.astype(BF)
    qx_ref[...] = (q * cross_ref[...]).astype(BF)
    k = _rotary(_dot(hn, w_ref[:, D_MODEL:2 * D_MODEL]), cos, sin)
    k_ref[...] = k.astype(BF)
    kd_ref[...] = (k * kdec_ref[...]).astype(BF)
    v_ref[...] = _dot(hn, w_ref[:, 2 * D_MODEL:4 * D_MODEL]).astype(BF)
    g_ref[...] = _dot(hn, w_ref[:, 4 * D_MODEL:6 * D_MODEL]).astype(BF)


def _proj(h, nw, w_in, cos, sin, cross, kdec, tm, cast):
    rows = h.shape[0]
    n = rows // tm
    ntab = cos.shape[0] // tm
    row_spec = lambda width: pl.BlockSpec((tm, width), lambda i: (i, 0))
    tab_spec = pl.BlockSpec((tm, RET_KDIM), lambda i: (i % ntab, 0))
    widths = (D_MODEL, D_MODEL, D_MODEL, D_MODEL, 2 * D_MODEL, 2 * D_MODEL)
    cast_src, cast_dst, cast_shape = zip(*[_cast_specs(w32, layer, n) for w32, layer in cast])
    out = pl.pallas_call(
        _proj_kernel,
        grid=(n,),
        in_specs=[row_spec(D_MODEL), _const_spec(nw.shape), _const_spec(w_in.shape),
                  tab_spec, tab_spec, _const_spec(cross.shape), _const_spec(kdec.shape)] + list(cast_src),
        out_specs=[row_spec(w) for w in widths] + list(cast_dst),
        out_shape=[jax.ShapeDtypeStruct((rows, w), BF) for w in widths] + list(cast_shape),
        compiler_params=_params(1),
        name="ret_proj",
    )(h, nw, w_in, cos, sin, cross, kdec, *[w32 for w32, _ in cast])
    return tuple(out[:len(widths)]), tuple(out[len(widths):])


def _ret_prompt_kernel(tot_ref, kdm_ref, vm_ref, dm_ref, q_ref, qx_ref, k_ref, kd_ref, v_ref,
                       gw_ref, gb_ref, o_ref, s_ref, state):
    t = pl.program_id(1)

    @pl.when(t == 0)
    def _():
        for h in range(RET_HEADS):
            state[h] = _kt_v(kdm_ref[:, _head_cols(h, RET_KDIM)], vm_ref[:, _head_cols(h, RET_VDIM)])

    for c in range(q_ref.shape[1] // SUB):
        rows = slice(c * SUB, (c + 1) * SUB)
        for h in range(RET_HEADS):
            kc = _head_cols(h, RET_KDIM)
            vc = _head_cols(h, RET_VDIM)
            q = q_ref[0, rows, kc]
            v = v_ref[0, rows, vc]
            s_prev = state[h]
            scores = lax.dot_general(q, k_ref[0, rows, kc], (((1,), (1,)), ((), ())),
                                     preferred_element_type=F32) * dm_ref[h]
            o = _dot(scores.astype(BF), v) + _dot(qx_ref[0, rows, kc], s_prev.astype(BF))
            state[h] = tot_ref[h] * s_prev + _kt_v(kd_ref[0, rows, kc], v)
            o_ref[0, rows, vc] = _group_norm(o, gw_ref[:, vc], gb_ref[:, vc]).astype(BF)

    @pl.when(t == pl.num_programs(1) - 1)
    def _():
        s_ref[0] = state[...]


def _ret_prompt(chunk_decay, kd_meta, v_meta, dmask, q, qx, k, kd, v, gn_w, gn_b, tt=4 * SUB):
    b, t, _ = q.shape
    qk_spec = pl.BlockSpec((1, tt, D_MODEL), lambda i, j, s: (i, j, 0))
    v_spec = pl.BlockSpec((1, tt, 2 * D_MODEL), lambda i, j, s: (i, j, 0))
    st_shape = (RET_HEADS, RET_KDIM, RET_VDIM)
    return pl.pallas_call(
        _ret_prompt_kernel,
        grid_spec=pltpu.PrefetchScalarGridSpec(
            num_scalar_prefetch=1,
            grid=(b, t // tt),
            in_specs=[_const_spec(kd_meta.shape), _const_spec(v_meta.shape), _const_spec(dmask.shape),
                      qk_spec, qk_spec, qk_spec, qk_spec, v_spec,
                      _const_spec(gn_w.shape), _const_spec(gn_b.shape)],
            out_specs=[v_spec,
                       pl.BlockSpec((None, 1) + st_shape, lambda i, j, s: (0, i, 0, 0, 0))],
            scratch_shapes=[pltpu.VMEM(st_shape, F32)]),
        out_shape=[jax.ShapeDtypeStruct((b, t, 2 * D_MODEL), BF),
                   jax.ShapeDtypeStruct((1, b) + st_shape, F32)],
        compiler_params=_params(2),
        name="ret_prompt",
    )(chunk_decay, kd_meta, v_meta, dmask, q, qx, k, kd, v, gn_w, gn_b)


def _out_ffn_rows(h, on, g, wo_ref, nw_ref, wg_ref, wu_ref, wd_ref, nf_ref):
    gated = (_silu(g.astype(F32)) * on.astype(F32)).astype(BF)
    h = h + _dot(gated, wo_ref[...])
    hn = _rms(h, nw_ref[...]).astype(BF)
    for c0, c1 in FF_CHUNKS:
        act = (_silu(_dot(hn, wg_ref[:, c0:c1])) * _dot(hn, wu_ref[:, c0:c1])).astype(BF)
        h = h + _dot(act, wd_ref[c0:c1, :])
    return _rms(h, nf_ref[...])


def _out_ffn_kernel(h_ref, on_ref, g_ref, hs_ref, ons_ref, gs_ref, wo_ref, nw_ref, wg_ref, wu_ref,
                    wd_ref, nf_ref, o_ref, os_ref):
    weights = (wo_ref, nw_ref, wg_ref, wu_ref, wd_ref, nf_ref)
    last = pl.program_id(0) == pl.num_programs(0) - 1

    @pl.when(jnp.logical_not(last))
    def _():
        o_ref[...] = _out_ffn_rows(h_ref[...], on_ref[...], g_ref[...], *weights)

    @pl.when(last)
    def _():
        os_ref[...] = _out_ffn_rows(hs_ref[...], ons_ref[...], gs_ref[...], *weights)


def _out_ffn(h, on, g, h_small, on_s, proj_s, wo, nw, ffn_w, nf, tm):
    rows = h.shape[0]
    n = rows // tm
    rows_s = on_s.shape[0]
    gate_block = (proj_s.shape[1] - 2 * D_MODEL) // (2 * D_MODEL)
    row_spec = lambda width: pl.BlockSpec((tm, width), lambda i: (jnp.minimum(i, n - 1), 0))
    small_spec = lambda width, blk=0: pl.BlockSpec((rows_s, width), lambda i: (0, blk))
    consts = (wo, nw) + tuple(ffn_w) + (nf,)
    return pl.pallas_call(
        _out_ffn_kernel,
        grid=(n + 1,),
        in_specs=[row_spec(D_MODEL), row_spec(2 * D_MODEL), row_spec(2 * D_MODEL),
                  small_spec(D_MODEL), small_spec(2 * D_MODEL), small_spec(2 * D_MODEL, gate_block)]
                 + [_const_spec(c.shape) for c in consts],
        out_specs=[row_spec(D_MODEL), small_spec(D_MODEL)],
        out_shape=[jax.ShapeDtypeStruct((rows, D_MODEL), F32),
                   jax.ShapeDtypeStruct((rows_s, D_MODEL), F32)],
        compiler_params=_params(1),
        name="out_ffn",
    )(h, on, g, h_small, on_s, proj_s, *consts)


def _rotary_tables(pos):
    theta = 1.0 / (ROPE_BASE ** jnp.linspace(0.0, 1.0, RET_KDIM // 2, dtype=F32))
    ang = pos.astype(F32)[:, None] * jnp.repeat(theta, 2)[None, :]
    sign = jnp.asarray(np.tile([-1.0, 1.0], RET_KDIM // 2), F32)
    return jnp.cos(ang), jnp.sin(ang) * sign[None, :]


def _per_head_cols(table):
    return jnp.asarray(np.repeat(table, RET_KDIM, axis=1), F32)


def _decay_to_end(c):
    return np.exp((c - 1.0 - np.arange(c))[:, None] * LOG_GAMMA[None, :])


def _decay_from_start(c):
    return np.exp((np.arange(c) + 1.0)[:, None] * LOG_GAMMA[None, :])


def _decay_mask(c):
    rel = np.arange(c)[:, None] - np.arange(c)[None, :]
    return np.where(rel[None] >= 0, np.exp(np.maximum(rel, 0)[None] * LOG_GAMMA[:, None, None]), 0.0)


def kernel(x_prompt, x_sample, state_pool, state_ret, meta_tokens, norm_mix, norm_ffn, norm_final,
           w_pool, pool_scale, w_ret_in, ret_gn_w, ret_gn_b, w_ret_out, w_ffn_gate, w_ffn_up, w_ffn_down):
    b, t, _ = x_prompt.shape
    bs = x_sample.shape[0]
    log_g = jnp.asarray(LOG_GAMMA, F32)

    wp = w_pool[0].astype(BF)
    ffn_w32 = (w_ffn_gate, w_ffn_up, w_ffn_down)
    nm0, nm1 = norm_mix[0:1], norm_mix[1:2]
    nf0, nf1 = norm_ffn[0:1], norm_ffn[1:2]
    nfin = norm_final[None, :]
    scale = pool_scale[0:1]
    gn_w, gn_b = ret_gn_w[0:1], ret_gn_b[0:1]

    hs, new_pool_sample, h_meta = _pool_sample(
        x_sample, state_pool.transpose(0, 2, 1, 3), meta_tokens, nm0, wp, scale)
    new_pool_sample = new_pool_sample.transpose(0, 2, 1, 3)
    h_small, ffn_w0 = _small_ffn(hs, h_meta, nf0, ffn_w32, 0)

    pos_small = np.concatenate([np.full((bs,), PAST_LEN), np.arange(N_META)]).astype(np.int32)
    cos_s, sin_s = _rotary_tables(jnp.asarray(pos_small))
    kdec_s = _per_head_cols(np.concatenate([np.ones((bs, RET_HEADS)), _decay_to_end(N_META)], axis=0))
    proj_s, kd_s, qt_s, kt_s, v_s, w_in = _small_proj(h_small, nm1, w_ret_in, cos_s, sin_s, kdec_s, bs)

    h, tail, on_s, new_ret_sample = _layer0_stream(
        x_prompt, meta_tokens, nm0, wp, scale, nf0, ffn_w0, TM, log_g, qt_s, kt_s, v_s, gn_w, gn_b, state_ret)

    cos, sin = _rotary_tables(N_META + jnp.arange(t, dtype=jnp.int32))
    reps = (TM // SUB, 1)
    (q, qx, k, kd, v, g), (w_out, *ffn_w1) = _proj(
        h, nm1, w_in, cos, sin,
        _per_head_cols(np.tile(_decay_from_start(SUB), reps)),
        _per_head_cols(np.tile(_decay_to_end(SUB), reps)), TM,
        cast=((w_ret_out, 0), (w_ffn_gate, 1), (w_ffn_up, 1), (w_ffn_down, 1)))
    seq = lambda a: a.reshape(b, t, -1)
    on, new_ret_prompt = _ret_prompt(jnp.asarray(np.exp(SUB * LOG_GAMMA), F32), kd_s[bs:],
                                     proj_s[bs:, 2 * D_MODEL:4 * D_MODEL],
                                     jnp.asarray(_decay_mask(SUB), F32),
                                     seq(q), seq(qx), seq(k), seq(kd), seq(v), gn_w, gn_b)
    y, y_sample = _out_ffn(h, on.reshape(b * t, -1), g, h_small, on_s, proj_s, w_out, nf1, ffn_w1, nfin, TM)

    return (y.reshape(b, t, D_MODEL), y_sample.reshape(bs, 1, D_MODEL), tail[:, 1:][None],
            new_pool_sample, new_ret_prompt, new_ret_sample)
```

```python
import functools

import numpy as np
import jax
import jax.numpy as jnp
from jax import lax
from jax.experimental import pallas as pl
from jax.experimental.pallas import tpu as pltpu

D_MODEL = 1024
N_META = 16
PAST_LEN = 16384
POOL_WINDOWS = (2, 4, 8, 16)
POOL_GROUPS = len(POOL_WINDOWS)
POOL_GROUP_DIM = D_MODEL // POOL_GROUPS
POOL_BUF = max(POOL_WINDOWS) - 1
RET_HEADS = 4
RET_KDIM = D_MODEL // RET_HEADS
RET_VDIM = 2 * D_MODEL // RET_HEADS
ROPE_BASE = 10000.0
D_FF = 2816
EPS = 1e-6

BF = jnp.bfloat16
F32 = jnp.float32

SUB = 256
TM = 512
FF_CHUNKS = tuple((c, min(c + 256, D_FF)) for c in range(0, D_FF, 256))
FF_STREAM = 256
assert D_FF % FF_STREAM == 0
POOL_SAMPLE_TB = 32
BF16_TILE_ROWS = 16
V7X_VMEM_BYTES = 64 * 1024 * 1024
VMEM_LIMIT = V7X_VMEM_BYTES * 7 // 8

LOG_GAMMA = np.log(1.0 - 2.0 ** (-5.0 - np.arange(RET_HEADS)))


def _dot(a, b):
    return jnp.dot(a, b, preferred_element_type=F32)


def _kt_v(k, v):
    return lax.dot_general(k, v, (((0,), (0,)), ((), ())), preferred_element_type=F32)


def _rms(x, g):
    ms = jnp.mean(x * x, axis=-1, keepdims=True)
    return (x * lax.rsqrt(ms + EPS)) * g


def _silu(x):
    return x * jax.nn.sigmoid(x)


def _group_norm(o, gn_w, gn_b):
    mu = jnp.mean(o, axis=-1, keepdims=True)
    cen = o - mu
    var = jnp.mean(cen * cen, axis=-1, keepdims=True)
    return (cen * lax.rsqrt(var + EPS)) * gn_w + gn_b


def _rotary(x, cos, sin):
    lane = lax.broadcasted_iota(jnp.int32, x.shape, 1)
    partner = jnp.where((lane & 1) == 0, pltpu.roll(x, x.shape[1] - 1, 1), pltpu.roll(x, 1, 1))
    return x * cos + partner * sin


def _head_cols(h, width):
    return slice(h * width, (h + 1) * width)


def _group_cols(g):
    return slice(g * POOL_GROUP_DIM, (g + 1) * POOL_GROUP_DIM)


def _const_spec(shape):
    nd = len(shape)
    return pl.BlockSpec(shape, lambda *_: (0,) * nd, pipeline_mode=pl.Buffered(1))


def _params(n_grid):
    return pltpu.CompilerParams(
        dimension_semantics=("arbitrary",) * n_grid, vmem_limit_bytes=VMEM_LIMIT)


def _band_matrices():
    t = np.arange(SUB)[:, None]
    s = np.arange(SUB)[None, :]
    sh = np.arange(N_META)[None, :] - N_META
    cur = np.stack([((t - s >= 0) & (t - s < w)) for w in POOL_WINDOWS]).astype(np.float32)
    halo = np.stack([(t - sh < w) for w in POOL_WINDOWS]).astype(np.float32)
    tm = np.arange(N_META)[:, None]
    sm = np.arange(N_META)[None, :]
    meta = np.stack([((tm - sm >= 0) & (tm - sm < w)) for w in POOL_WINDOWS]).astype(np.float32)
    inv_meta = np.concatenate(
        [np.repeat(1.0 / np.minimum(w, tm + 1.0), POOL_GROUP_DIM, axis=1) for w in POOL_WINDOWS], axis=1)
    inv_w = np.concatenate([np.full((1, POOL_GROUP_DIM), 1.0 / w) for w in POOL_WINDOWS], axis=1)
    return (jnp.asarray(cur, BF), jnp.asarray(halo, BF), jnp.asarray(meta, BF),
            jnp.asarray(inv_meta, F32), jnp.asarray(inv_w, F32))


def _pool_mix(pooled_sum, inv_cnt, xn, wp_ref, scale):
    diff = (pooled_sum * inv_cnt - xn).astype(BF)
    mixed = jnp.concatenate(
        [_dot(diff[:, _group_cols(g)], wp_ref[g]) for g in range(POOL_GROUPS)], axis=1)
    return mixed * scale


def _pool_prompt_tile(x_ref, halo_ref, nw_ref, wp_ref, sc_ref, bc_ref, bh_ref, iw_ref, tail_ref):
    nw = nw_ref[...]
    scale = sc_ref[...]
    inv_w = iw_ref[...]
    halo = halo_ref[...]
    out = []
    for j in range(x_ref.shape[0] // SUB):
        x = x_ref[SUB * j:SUB * (j + 1), :]
        xn = _rms(x, nw)
        xb = xn.astype(BF)
        pooled = jnp.concatenate(
            [_dot(bc_ref[g], xb[:, _group_cols(g)]) + _dot(bh_ref[g], halo[:, _group_cols(g)])
             for g in range(POOL_GROUPS)], axis=1)
        out.append(x + _pool_mix(pooled, inv_w, xn, wp_ref, scale))
        halo = xb[SUB - N_META:, :]
    halo_ref[...] = halo
    tail_ref[0] = xn[SUB - N_META:, :]
    return jnp.concatenate(out, axis=0)


def _pool_sample_kernel(x_ref, prev_ref, meta_ref, nw_ref, wp_ref, sc_ref, iw_ref, bm_ref, icm_ref,
                        h_ref, np_ref, hm_ref):
    @pl.when(pl.program_id(0) == 0)
    def _():
        xm = meta_ref[...]
        xnm = _rms(xm, nw_ref[...])
        xbm = xnm.astype(BF)
        pooled_m = jnp.concatenate(
            [_dot(bm_ref[g], xbm[:, _group_cols(g)]) for g in range(POOL_GROUPS)], axis=1)
        hm_ref[...] = xm + _pool_mix(pooled_m, icm_ref[...], xnm, wp_ref, sc_ref[...])

    x = x_ref[:, 0, :]
    xn = _rms(x, nw_ref[...])
    sums = []
    for g, w in enumerate(POOL_WINDOWS):
        s = xn[:, _group_cols(g)]
        for j in range(1, w):
            s = s + prev_ref[POOL_BUF - j, :, _group_cols(g)]
        sums.append(s)
    pooled = jnp.concatenate(sums, axis=1)
    h_ref[...] = x + _pool_mix(pooled, iw_ref[...], xn, wp_ref, sc_ref[...])
    for r in range(POOL_BUF - 1):
        np_ref[r] = prev_ref[r + 1]
    np_ref[POOL_BUF - 1] = xn


def _pool_sample(x, prev, meta, nw, wp, scale, tb=POOL_SAMPLE_TB):
    b = x.shape[0]
    _, _, band_meta, inv_meta, inv_w = _band_matrices()
    consts = (meta, nw, wp, scale, inv_w, band_meta, inv_meta)
    st_spec = pl.BlockSpec((None, POOL_BUF, tb, D_MODEL), lambda i: (0, 0, i, 0))
    return pl.pallas_call(
        _pool_sample_kernel,
        grid=(b // tb,),
        in_specs=[pl.BlockSpec((tb, 1, D_MODEL), lambda i: (i, 0, 0)), st_spec]
                 + [_const_spec(c.shape) for c in consts],
        out_specs=[pl.BlockSpec((tb, D_MODEL), lambda i: (i, 0)), st_spec,
                   pl.BlockSpec(meta.shape, lambda i: (0, 0))],
        out_shape=[jax.ShapeDtypeStruct((b, D_MODEL), F32),
                   jax.ShapeDtypeStruct(prev.shape, F32),
                   jax.ShapeDtypeStruct(meta.shape, F32)],
        compiler_params=_params(1),
        name="pool_sample",
    )(x, prev, *consts)


def _small_ffn_kernel(hs_ref, hm_ref, nw_ref, wg_ref, wu_ref, wd_ref, o_ref, wg_bf, wu_bf, wd_bf, hn):
    @pl.when(pl.program_id(0) == 0)
    def _():
        h = jnp.concatenate([hs_ref[...], hm_ref[...]], axis=0)
        hn[...] = _rms(h, nw_ref[...]).astype(BF)
        o_ref[...] = h

    wg = wg_ref[...].astype(BF)
    wu = wu_ref[...].astype(BF)
    wd = wd_ref[...].astype(BF)
    wg_bf[...] = wg
    wu_bf[...] = wu
    wd_bf[...] = wd
    act = (_silu(_dot(hn[...], wg)) * _dot(hn[...], wu)).astype(BF)
    o_ref[...] += _dot(act, wd)


def _small_ffn(hs, hm, nw, ffn_w32, layer):
    rows = hs.shape[0] + hm.shape[0]
    whole = pl.BlockSpec((rows, D_MODEL), lambda c: (0, 0))
    out = pl.pallas_call(
        _small_ffn_kernel,
        grid=(D_FF // FF_STREAM,),
        in_specs=[_const_spec(hs.shape), _const_spec(hm.shape), _const_spec(nw.shape),
                  pl.BlockSpec((None, D_MODEL, FF_STREAM), lambda c: (layer, 0, c)),
                  pl.BlockSpec((None, D_MODEL, FF_STREAM), lambda c: (layer, 0, c)),
                  pl.BlockSpec((None, FF_STREAM, D_MODEL), lambda c: (layer, c, 0))],
        out_specs=[whole,
                   pl.BlockSpec((D_MODEL, FF_STREAM), lambda c: (0, c)),
                   pl.BlockSpec((D_MODEL, FF_STREAM), lambda c: (0, c)),
                   pl.BlockSpec((FF_STREAM, D_MODEL), lambda c: (c, 0))],
        out_shape=[jax.ShapeDtypeStruct((rows, D_MODEL), F32),
                   jax.ShapeDtypeStruct((D_MODEL, D_FF), BF), jax.ShapeDtypeStruct((D_MODEL, D_FF), BF),
                   jax.ShapeDtypeStruct((D_FF, D_MODEL), BF)],
        scratch_shapes=[pltpu.VMEM((rows, D_MODEL), BF)],
        compiler_params=_params(1),
        name="small_ffn",
    )(hs, hm, nw, *ffn_w32)
    return out[0], tuple(out[1:])


def _small_proj_kernel(h_ref, nw_ref, w_ref, cos_ref, sin_ref, kdec_ref,
                       p_ref, kd_ref, qt_ref, kt_ref, v_ref, w_bf, hn):
    c = pl.program_id(0)
    bs = qt_ref.shape[1]

    @pl.when(c == 0)
    def _():
        hn[...] = _rms(h_ref[...], nw_ref[...]).astype(BF)

    w = w_ref[...].astype(BF)
    w_bf[...] = w
    y = _dot(hn[...], w)

    def rotated():
        return _rotary(y, jnp.concatenate([cos_ref[...]] * RET_HEADS, axis=1),
                       jnp.concatenate([sin_ref[...]] * RET_HEADS, axis=1))

    @pl.when(c == 0)
    def _():
        q = (rotated() * (RET_KDIM ** -0.5)).astype(BF)
        p_ref[...] = q
        qt_ref[...] = q[:bs].astype(F32).T

    @pl.when(c == 1)
    def _():
        k = rotated()
        p_ref[...] = k.astype(BF)
        kt_ref[...] = k[:bs].astype(BF).astype(F32).T
        kd_ref[...] = (k * kdec_ref[...]).astype(BF)

    @pl.when((c == 2) | (c == 3))
    def _():
        v = y.astype(BF)
        p_ref[...] = v
        v_ref[...] = v[:bs].astype(F32)

    @pl.when(c >= 4)
    def _():
        p_ref[...] = y.astype(BF)


def _small_proj(h, nw, w_in32, cos, sin, kdec, bs):
    rows = h.shape[0]
    whole = lambda r, w: pl.BlockSpec((r, w), lambda c: (0, 0))
    v_chunk = lambda c: jnp.clip(c - 2, 0, 1)
    return pl.pallas_call(
        _small_proj_kernel,
        grid=(6,),
        in_specs=[whole(rows, D_MODEL), _const_spec(nw.shape),
                  pl.BlockSpec((None, D_MODEL, D_MODEL), lambda c: (0, 0, c)),
                  _const_spec(cos.shape), _const_spec(sin.shape), _const_spec(kdec.shape)],
        out_specs=[pl.BlockSpec((rows, D_MODEL), lambda c: (0, c)),
                   whole(rows, D_MODEL), whole(D_MODEL, bs), whole(D_MODEL, bs),
                   pl.BlockSpec((bs, D_MODEL), lambda c: (0, v_chunk(c))),
                   pl.BlockSpec((D_MODEL, D_MODEL), lambda c: (0, c))],
        out_shape=[jax.ShapeDtypeStruct((rows, 6 * D_MODEL), BF),
                   jax.ShapeDtypeStruct((rows, D_MODEL), BF),
                   jax.ShapeDtypeStruct((D_MODEL, bs), F32),
                   jax.ShapeDtypeStruct((D_MODEL, bs), F32),
                   jax.ShapeDtypeStruct((bs, 2 * D_MODEL), F32),
                   jax.ShapeDtypeStruct((D_MODEL, 6 * D_MODEL), BF)],
        scratch_shapes=[pltpu.VMEM((rows, D_MODEL), BF)],
        compiler_params=_params(1),
        name="small_proj",
    )(h, nw, w_in32, cos, sin, kdec)


STREAM_TB = 4


def _head_state_update(row, s, h, lg_ref, qt, kt, v_ref, gw_ref, gb_ref, s_in, s_out, on_ref):
    gamma = jnp.exp(jnp.full((1, 1), 1.0, F32) * lg_ref[h])
    kc = _head_cols(h, RET_KDIM)
    vc = _head_cols(h, RET_VDIM)
    q = qt[kc, s:s + 1]
    k = kt[kc, s:s + 1]
    v = v_ref[pl.ds(row, 1), vc]
    s_prev = s_in[s, h]
    qs = jnp.sum(q * s_prev, axis=0, keepdims=True)
    score = jnp.sum(q * k, axis=0, keepdims=True)
    o = score * v + qs * gamma
    s_out[s, h] = gamma * s_prev + k * v
    on_ref[pl.ds(row, 1), vc] = _group_norm(o, gw_ref[:, vc], gb_ref[:, vc])


def _layer0_stream_kernel(tiles_per_seq, lg_ref, x_ref, meta_ref, nm_ref, wp_ref, sc_ref, bc_ref, bh_ref,
                          iw_ref, nw_ref, wg_ref, wu_ref, wd_ref, qt_ref, kt_ref, v_ref, gw_ref, gb_ref,
                          s_hbm, o_ref, tail_ref, on_ref, so_hbm, halo, s_in, s_out, in_sem, out_sem):
    i = pl.program_id(0)
    n = pl.num_programs(0)

    @pl.when(i % tiles_per_seq == 0)
    def _():
        halo[...] = _rms(meta_ref[...], nm_ref[...]).astype(BF)

    def in_copy(step, s):
        return pltpu.make_async_copy(s_hbm.at[0, step * STREAM_TB + s], s_in.at[s], in_sem.at[s])

    def out_copy(step, s):
        return pltpu.make_async_copy(s_out.at[s], so_hbm.at[0, step * STREAM_TB + s], out_sem.at[s])

    @pl.when(i == 0)
    def _():
        for s in range(STREAM_TB):
            in_copy(0, s).start()

    lanes = qt_ref.shape[1]
    shift = (lanes - i * STREAM_TB) % lanes
    qt = pltpu.roll(qt_ref[...], shift, 1)
    kt = pltpu.roll(kt_ref[...], shift, 1)

    h = _pool_prompt_tile(x_ref, halo, nm_ref, wp_ref, sc_ref, bc_ref, bh_ref, iw_ref, tail_ref)
    hn = _rms(h, nw_ref[...]).astype(BF)

    ffn = {"acc": h, "act": None}

    def up_unit(cols):
        c0, c1 = cols
        gt = _dot(hn, wg_ref[:, c0:c1])
        up = _dot(hn, wu_ref[:, c0:c1])
        ffn["act"] = (_silu(gt) * up).astype(BF)

    def down_unit(cols):
        c0, c1 = cols
        ffn["acc"] = ffn["acc"] + _dot(ffn["act"], wd_ref[c0:c1, :])

    units = [functools.partial(u, cols) for cols in FF_CHUNKS for u in (up_unit, down_unit)]
    first_half = len(units) // 2
    for u in units[:first_half]:
        u()
    for s in range(STREAM_TB):
        in_copy(i, s).wait()

    @pl.when(i > 0)
    def _():
        for s in range(STREAM_TB):
            out_copy(i - 1, s).wait()

    rest = units[first_half:]
    updates = [(s, hd) for s in range(STREAM_TB) for hd in range(RET_HEADS)]
    done = 0
    for j, (s, hd) in enumerate(updates):
        _head_state_update(i * STREAM_TB + s, s, hd, lg_ref, qt, kt, v_ref, gw_ref, gb_ref,
                           s_in, s_out, on_ref)
        target = ((j + 1) * len(rest)) // len(updates)
        for u in rest[done:target]:
            u()
        done = target
    o_ref[...] = ffn["acc"]

    for s in range(STREAM_TB):
        out_copy(i, s).start(priority=s % 2)

    @pl.when(i + 1 < n)
    def _():
        for s in range(STREAM_TB):
            in_copy(i + 1, s).start(priority=s % 2)

    @pl.when(i == n - 1)
    def _():
        for s in range(STREAM_TB):
            out_copy(i, s).wait()


def _layer0_stream(x, meta, nm, wp, scale, nw, ffn_w, tm, log_g, qt, kt, v_s, gn_w, gn_b, s0):
    b, t, _ = x.shape
    rows = b * t
    n = rows // tm
    assert v_s.shape[0] == n * STREAM_TB and t % tm == 0
    tiles_per_seq = t // tm
    band_cur, band_halo, _, _, inv_w = _band_matrices()
    row_spec = pl.BlockSpec((tm, D_MODEL), lambda i, lg: (i, 0))
    tail_spec = pl.BlockSpec((1, N_META, D_MODEL), lambda i, lg: (i // tiles_per_seq, 0, 0))
    on_spec = pl.BlockSpec(v_s.shape, lambda i, lg: (0, 0))
    hbm = pl.BlockSpec(memory_space=pl.ANY)
    blk = (STREAM_TB, RET_HEADS, RET_KDIM, RET_VDIM)
    consts = (meta, nm, wp, scale, band_cur, band_halo, inv_w, nw) + tuple(ffn_w) + (qt, kt, v_s, gn_w, gn_b)
    return pl.pallas_call(
        functools.partial(_layer0_stream_kernel, tiles_per_seq),
        grid_spec=pltpu.PrefetchScalarGridSpec(
            num_scalar_prefetch=1,
            grid=(n,),
            in_specs=[row_spec] + [_const_spec(c.shape) for c in consts] + [hbm],
            out_specs=[row_spec, tail_spec, on_spec, hbm],
            scratch_shapes=[pltpu.VMEM((N_META, D_MODEL), BF),
                            pltpu.VMEM(blk, F32), pltpu.VMEM(blk, F32),
                            pltpu.SemaphoreType.DMA((STREAM_TB,)), pltpu.SemaphoreType.DMA((STREAM_TB,))]),
        out_shape=[jax.ShapeDtypeStruct((rows, D_MODEL), F32),
                   jax.ShapeDtypeStruct((b, N_META, D_MODEL), F32),
                   jax.ShapeDtypeStruct(v_s.shape, F32),
                   jax.ShapeDtypeStruct(s0.shape, F32)],
        compiler_params=_params(1),
        name="layer0_stream",
    )(log_g, x.reshape(rows, D_MODEL), *consts, s0)


def _cast_row_blocks(rows, n):
    for hold in range(1, n + 1):
        blocks = n // hold
        if n % hold == 0 and rows % blocks == 0 and (rows // blocks) % BF16_TILE_ROWS == 0:
            return rows // blocks, hold
    raise ValueError((rows, n))


def _cast_specs(w32, layer, n):
    rows, cols = w32.shape[1:]
    blk, hold = _cast_row_blocks(rows, n)
    src = pl.BlockSpec((None, blk, cols), lambda i: (layer, i // hold, 0))
    dst = pl.BlockSpec((blk, cols), lambda i: (i // hold, 0))
    return src, dst, jax.ShapeDtypeStruct((rows, cols), BF)


def _proj_kernel(h_ref, nw_ref, w_ref, cos_ref, sin_ref, cross_ref, kdec_ref,
                 c0_ref, c1_ref, c2_ref, c3_ref,
                 q_ref, qx_ref, k_ref, kd_ref, v_ref, g_ref, d0_ref, d1_ref, d2_ref, d3_ref):
    for src, dst in ((c0_ref, d0_ref), (c1_ref, d1_ref), (c2_ref, d2_ref), (c3_ref, d3_ref)):
        dst[...] = src[...].astype(BF)

    hn = _rms(h_ref[...], nw_ref[...]).astype(BF)
    cos = jnp.concatenate([cos_ref[...]] * RET_HEADS, axis=1)
    sin = jnp.concatenate([sin_ref[...]] * RET_HEADS, axis=1)

    q = _rotary(_dot(hn, w_ref[:, 0:D_MODEL]), cos, sin) * (RET_KDIM ** -0.5)
    q_ref[...] = q.astype(BF)
    qx_ref[...] = (q * cross_ref[...]).astype(BF)
    k = _rotary(_dot(hn, w_ref[:, D_MODEL:2 * D_MODEL]), cos, sin)
    k_ref[...] = k.astype(BF)
    kd_ref[...] = (k * kdec_ref[...]).astype(BF)
    v_ref[...] = _dot(hn, w_ref[:, 2 * D_MODEL:4 * D_MODEL]).astype(BF)
    g_ref[...] = _dot(hn, w_ref[:, 4 * D_MODEL:6 * D_MODEL]).astype(BF)


def _proj(h, nw, w_in, cos, sin, cross, kdec, tm, cast):
    rows = h.shape[0]
    n = rows // tm
    ntab = cos.shape[0] // tm
    row_spec = lambda width: pl.BlockSpec((tm, width), lambda i: (i, 0))
    tab_spec = pl.BlockSpec((tm, RET_KDIM), lambda i: (i % ntab, 0))
    widths = (D_MODEL, D_MODEL, D_MODEL, D_MODEL, 2 * D_MODEL, 2 * D_MODEL)
    cast_src, cast_dst, cast_shape = zip(*[_cast_specs(w32, layer, n) for w32, layer in cast])
    out = pl.pallas_call(
        _proj_kernel,
        grid=(n,),
        in_specs=[row_spec(D_MODEL), _const_spec(nw.shape), _const_spec(w_in.shape),
                  tab_spec, tab_spec, _const_spec(cross.shape), _const_spec(kdec.shape)] + list(cast_src),
        out_specs=[row_spec(w) for w in widths] + list(cast_dst),
        out_shape=[jax.ShapeDtypeStruct((rows, w), BF) for w in widths] + list(cast_shape),
        compiler_params=_params(1),
        name="ret_proj",
    )(h, nw, w_in, cos, sin, cross, kdec, *[w32 for w32, _ in cast])
    return tuple(out[:len(widths)]), tuple(out[len(widths):])


def _ret_prompt_kernel(tot_ref, kdm_ref, vm_ref, dm_ref, q_ref, qx_ref, k_ref, kd_ref, v_ref,
                       gw_ref, gb_ref, o_ref, s_ref, state):
    t = pl.program_id(1)

    @pl.when(t == 0)
    def _():
        for h in range(RET_HEADS):
            state[h] = _kt_v(kdm_ref[:, _head_cols(h, RET_KDIM)], vm_ref[:, _head_cols(h, RET_VDIM)])

    for c in range(q_ref.shape[1] // SUB):
        rows = slice(c * SUB, (c + 1) * SUB)
        for h in range(RET_HEADS):
            kc = _head_cols(h, RET_KDIM)
            vc = _head_cols(h, RET_VDIM)
            q = q_ref[0, rows, kc]
            v = v_ref[0, rows, vc]
            s_prev = state[h]
            scores = lax.dot_general(q, k_ref[0, rows, kc], (((1,), (1,)), ((), ())),
                                     preferred_element_type=F32) * dm_ref[h]
            o = _dot(scores.astype(BF), v) + _dot(qx_ref[0, rows, kc], s_prev.astype(BF))
            state[h] = tot_ref[h] * s_prev + _kt_v(kd_ref[0, rows, kc], v)
            o_ref[0, rows, vc] = _group_norm(o, gw_ref[:, vc], gb_ref[:, vc]).astype(BF)

    @pl.when(t == pl.num_programs(1) - 1)
    def _():
        s_ref[0] = state[...]


def _ret_prompt(chunk_decay, kd_meta, v_meta, dmask, q, qx, k, kd, v, gn_w, gn_b, tt=4 * SUB):
    b, t, _ = q.shape
    qk_spec = pl.BlockSpec((1, tt, D_MODEL), lambda i, j, s: (i, j, 0))
    v_spec = pl.BlockSpec((1, tt, 2 * D_MODEL), lambda i, j, s: (i, j, 0))
    st_shape = (RET_HEADS, RET_KDIM, RET_VDIM)
    return pl.pallas_call(
        _ret_prompt_kernel,
        grid_spec=pltpu.PrefetchScalarGridSpec(
            num_scalar_prefetch=1,
            grid=(b, t // tt),
            in_specs=[_const_spec(kd_meta.shape), _const_spec(v_meta.shape), _const_spec(dmask.shape),
                      qk_spec, qk_spec, qk_spec, qk_spec, v_spec,
                      _const_spec(gn_w.shape), _const_spec(gn_b.shape)],
            out_specs=[v_spec,
                       pl.BlockSpec((None, 1) + st_shape, lambda i, j, s: (0, i, 0, 0, 0))],
            scratch_shapes=[pltpu.VMEM(st_shape, F32)]),
        out_shape=[jax.ShapeDtypeStruct((b, t, 2 * D_MODEL), BF),
                   jax.ShapeDtypeStruct((1, b) + st_shape, F32)],
        compiler_params=_params(2),
        name="ret_prompt",
    )(chunk_decay, kd_meta, v_meta, dmask, q, qx, k, kd, v, gn_w, gn_b)


def _out_ffn_rows(h, on, g, wo_ref, nw_ref, wg_ref, wu_ref, wd_ref, nf_ref):
    gated = (_silu(g.astype(F32)) * on.astype(F32)).astype(BF)
    h = h + _dot(gated, wo_ref[...])
    hn = _rms(h, nw_ref[...]).astype(BF)
    for c0, c1 in FF_CHUNKS:
        act = (_silu(_dot(hn, wg_ref[:, c0:c1])) * _dot(hn, wu_ref[:, c0:c1])).astype(BF)
        h = h + _dot(act, wd_ref[c0:c1, :])
    return _rms(h, nf_ref[...])


def _out_ffn_kernel(h_ref, on_ref, g_ref, hs_ref, ons_ref, gs_ref, wo_ref, nw_ref, wg_ref, wu_ref,
                    wd_ref, nf_ref, o_ref, os_ref):
    weights = (wo_ref, nw_ref, wg_ref, wu_ref, wd_ref, nf_ref)
    last = pl.program_id(0) == pl.num_programs(0) - 1

    @pl.when(jnp.logical_not(last))
    def _():
        o_ref[...] = _out_ffn_rows(h_ref[...], on_ref[...], g_ref[...], *weights)

    @pl.when(last)
    def _():
        os_ref[...] = _out_ffn_rows(hs_ref[...], ons_ref[...], gs_ref[...], *weights)


def _out_ffn(h, on, g, h_small, on_s, proj_s, wo, nw, ffn_w, nf, tm):
    rows = h.shape[0]
    n = rows // tm
    rows_s = on_s.shape[0]
    gate_block = (proj_s.shape[1] - 2 * D_MODEL) // (2 * D_MODEL)
    row_spec = lambda width: pl.BlockSpec((tm, width), lambda i: (jnp.minimum(i, n - 1), 0))
    small_spec = lambda width, blk=0: pl.BlockSpec((rows_s, width), lambda i: (0, blk))
    consts = (wo, nw) + tuple(ffn_w) + (nf,)
    return pl.pallas_call(
        _out_ffn_kernel,
        grid=(n + 1,),
        in_specs=[row_spec(D_MODEL), row_spec(2 * D_MODEL), row_spec(2 * D_MODEL),
                  small_spec(D_MODEL), small_spec(2 * D_MODEL), small_spec(2 * D_MODEL, gate_block)]
                 + [_const_spec(c.shape) for c in consts],
        out_specs=[row_spec(D_MODEL), small_spec(D_MODEL)],
        out_shape=[jax.ShapeDtypeStruct((rows, D_MODEL), F32),
                   jax.ShapeDtypeStruct((rows_s, D_MODEL), F32)],
        compiler_params=_params(1),
        name="out_ffn",
    )(h, on, g, h_small, on_s, proj_s, *consts)


def _rotary_tables(pos):
    theta = 1.0 / (ROPE_BASE ** jnp.linspace(0.0, 1.0, RET_KDIM // 2, dtype=F32))
    ang = pos.astype(F32)[:, None] * jnp.repeat(theta, 2)[None, :]
    sign = jnp.asarray(np.tile([-1.0, 1.0], RET_KDIM // 2), F32)
    return jnp.cos(ang), jnp.sin(ang) * sign[None, :]


def _per_head_cols(table):
    return jnp.asarray(np.repeat(table, RET_KDIM, axis=1), F32)


def _decay_to_end(c):
    return np.exp((c - 1.0 - np.arange(c))[:, None] * LOG_GAMMA[None, :])


def _decay_from_start(c):
    return np.exp((np.arange(c) + 1.0)[:, None] * LOG_GAMMA[None, :])


def _decay_mask(c):
    rel = np.arange(c)[:, None] - np.arange(c)[None, :]
    return np.where(rel[None] >= 0, np.exp(np.maximum(rel, 0)[None] * LOG_GAMMA[:, None, None]), 0.0)


def kernel(x_prompt, x_sample, state_pool, state_ret, meta_tokens, norm_mix, norm_ffn, norm_final,
           w_pool, pool_scale, w_ret_in, ret_gn_w, ret_gn_b, w_ret_out, w_ffn_gate, w_ffn_up, w_ffn_down):
    b, t, _ = x_prompt.shape
    bs = x_sample.shape[0]
    log_g = jnp.asarray(LOG_GAMMA, F32)

    wp = w_pool[0].astype(BF)
    ffn_w32 = (w_ffn_gate, w_ffn_up, w_ffn_down)
    nm0, nm1 = norm_mix[0:1], norm_mix[1:2]
    nf0, nf1 = norm_ffn[0:1], norm_ffn[1:2]
    nfin = norm_final[None, :]
    scale = pool_scale[0:1]
    gn_w, gn_b = ret_gn_w[0:1], ret_gn_b[0:1]

    hs, new_pool_sample, h_meta = _pool_sample(
        x_sample, state_pool.transpose(0, 2, 1, 3), meta_tokens, nm0, wp, scale)
    new_pool_sample = new_pool_sample.transpose(0, 2, 1, 3)
    h_small, ffn_w0 = _small_ffn(hs, h_meta, nf0, ffn_w32, 0)

    pos_small = np.concatenate([np.full((bs,), PAST_LEN), np.arange(N_META)]).astype(np.int32)
    cos_s, sin_s = _rotary_tables(jnp.asarray(pos_small))
    kdec_s = _per_head_cols(np.concatenate([np.ones((bs, RET_HEADS)), _decay_to_end(N_META)], axis=0))
    proj_s, kd_s, qt_s, kt_s, v_s, w_in = _small_proj(h_small, nm1, w_ret_in, cos_s, sin_s, kdec_s, bs)

    h, tail, on_s, new_ret_sample = _layer0_stream(
        x_prompt, meta_tokens, nm0, wp, scale, nf0, ffn_w0, TM, log_g, qt_s, kt_s, v_s, gn_w, gn_b, state_ret)

    cos, sin = _rotary_tables(N_META + jnp.arange(t, dtype=jnp.int32))
    reps = (TM // SUB, 1)
    (q, qx, k, kd, v, g), (w_out, *ffn_w1) = _proj(
        h, nm1, w_in, cos, sin,
        _per_head_cols(np.tile(_decay_from_start(SUB), reps)),
        _per_head_cols(np.tile(_decay_to_end(SUB), reps)), TM,
        cast=((w_ret_out, 0), (w_ffn_gate, 1), (w_ffn_up, 1), (w_ffn_down, 1)))
    seq = lambda a: a.reshape(b, t, -1)
    on, new_ret_prompt = _ret_prompt(jnp.asarray(np.exp(SUB * LOG_GAMMA), F32), kd_s[bs:],
                                     proj_s[bs:, 2 * D_MODEL:4 * D_MODEL],
                                     jnp.asarray(_decay_mask(SUB), F32),
                                     seq(q), seq(qx), seq(k), seq(kd), seq(v), gn_w, gn_b)
    y, y_sample = _out_ffn(h, on.reshape(b * t, -1), g, h_small, on_s, proj_s, w_out, nf1, ffn_w1, nfin, TM)

    return (y.reshape(b, t, D_MODEL), y_sample.reshape(bs, 1, D_MODEL), tail[:, 1:][None],
            new_pool_sample, new_ret_prompt, new_ret_sample)
```
